```python
import jax, jax.numpy as jnp
from jax import lax
import numpy as np

D_MODEL = 2048
BATCH = 2
SEQ = 16384
DEPTH = 2

ALPHA = (2 * DEPTH) ** 0.25
BETA = (8 * DEPTH) ** -0.25
LN_EPS = 1e-5
RMS_EPS = 1e-6
ROPE_THETA = 10000.0

MLA_HEADS = 8
MLA_Q_RANK = 448
MLA_KV_RANK = 128
MLA_NOPE = 128
MLA_ROPE = 64
MLA_V = 128
Q_BLOCK = 128
RWKV_HEADS = 16
RWKV_HEAD = 64
RWKV_DIM = RWKV_HEADS * RWKV_HEAD
DECAY_LORA = 64
AAA_LORA = 64
GATE_LORA = 160
RWKV_GN_EPS = 64e-5
GDN_QK_HEADS = 4
GDN_V_HEADS = 8
GDN_DK = 128
GDN_DV = 128
GDN_CONV = 4
CHUNK = 64
RET_HEADS = 8
RET_DK = 64
RET_DV = 128
FFN_DIM = 5632
FFN_CONV = 3

MLA_COLS = MLA_Q_RANK + MLA_KV_RANK + MLA_ROPE
RWKV_COLS = 3 * RWKV_DIM + DECAY_LORA + AAA_LORA + GATE_LORA
EVEN_IN = MLA_COLS + RWKV_COLS
EVEN_OUT = MLA_HEADS * MLA_V + RWKV_DIM
GDN_COLS = 2 * GDN_QK_HEADS * GDN_DK + 2 * GDN_V_HEADS * GDN_DV + 2 * GDN_V_HEADS
RET_COLS = 2 * RET_HEADS * RET_DK + 2 * RET_HEADS * RET_DV
ODD_IN = GDN_COLS + RET_COLS
ODD_OUT = GDN_V_HEADS * GDN_DV + RET_HEADS * RET_DV

kernel_name = "hybrid_mla_rwkv7_gdn_retention_deepnorm"


def split_cols(p, sizes):
    return jnp.split(p, [int(s) for s in np.cumsum(sizes)[:-1]], axis=-1)


def layer_norm(x, g, b, eps=LN_EPS):
    xf = x.astype(jnp.float32)
    mu = xf.mean(-1, keepdims=True)
    var = jnp.square(xf - mu).mean(-1, keepdims=True)
    return ((xf - mu) * lax.rsqrt(var + eps) * g + b).astype(x.dtype)


def rms_norm(x, g, eps=RMS_EPS):
    xf = x.astype(jnp.float32)
    return (xf * lax.rsqrt(jnp.mean(xf * xf, -1, keepdims=True) + eps) * g).astype(x.dtype)


def head_norm(y, g, b, eps):
    B, S, H, d = y.shape
    yf = y.astype(jnp.float32)
    mu = yf.mean(-1, keepdims=True)
    var = jnp.square(yf - mu).mean(-1, keepdims=True)
    yn = (yf - mu) * lax.rsqrt(var + eps)
    return yn.reshape(B, S, H * d) * g + b


def l2_normalize(x, eps=1e-6):
    xf = x.astype(jnp.float32)
    return xf * lax.rsqrt(jnp.sum(xf * xf, -1, keepdims=True) + eps)


def rope(x, positions):
    d = x.shape[-1]
    inv = ROPE_THETA ** (-jnp.arange(0, d, 2, dtype=jnp.float32) / d)
    ang = positions.astype(jnp.float32)[..., None] * inv
    cos, sin = jnp.cos(ang)[:, :, None, :], jnp.sin(ang)[:, :, None, :]
    xf = x.astype(jnp.float32)
    x1, x2 = xf[..., : d // 2], xf[..., d // 2:]
    return jnp.concatenate([x1 * cos - x2 * sin, x1 * sin + x2 * cos], -1).astype(x.dtype)


def causal_dwconv(x, w):
    K, C = w.shape
    return lax.conv_general_dilated(x, w[:, None, :].astype(x.dtype), (1,), [(K - 1, 0)],
                                    dimension_numbers=("NWC", "WIO", "NWC"), feature_group_count=C)


def to_chunks(t):
    B, S = t.shape[:2]
    t = t.astype(jnp.float32).reshape((B, S // CHUNK, CHUNK) + t.shape[2:])
    return jnp.swapaxes(t, 2, 3)


def from_chunks(t):
    B, N, H, C, d = t.shape
    return jnp.swapaxes(t, 2, 3).reshape(B, N * C, H, d)


def causal_block_attention(q_nope, q_pe, k_nope, k_pe, v, scale):
    B, S, H, _ = q_nope.shape
    dv = v.shape[-1]
    kpos = jnp.arange(S)

    def one_block(i):
        start = i * Q_BLOCK
        qn = lax.dynamic_slice_in_dim(q_nope, start, Q_BLOCK, axis=1)
        qp = lax.dynamic_slice_in_dim(q_pe, start, Q_BLOCK, axis=1)
        s = (jnp.einsum('bqhd,bkhd->bhqk', qn, k_nope) +
             jnp.einsum('bqhr,bkr->bhqk', qp, k_pe)).astype(jnp.float32) * scale
        qpos = start + jnp.arange(Q_BLOCK)
        s = jnp.where(kpos[None, :] <= qpos[:, None], s, jnp.finfo(jnp.float32).min)
        p = jax.nn.softmax(s, axis=-1)
        return jnp.einsum('bhqk,bkhd->bqhd', p.astype(v.dtype), v)

    out = lax.map(one_block, jnp.arange(S // Q_BLOCK))
    return jnp.moveaxis(out, 0, 1).reshape(B, S, H, dv)


def rwkv7_recurrence(r, w, k, v, a_vec, b_vec):
    B, S, H, N = r.shape
    xs = tuple(jnp.moveaxis(t.astype(jnp.float32), 1, 0) for t in (r, w, k, v, a_vec, b_vec))

    def step(st, inp):
        r_t, w_t, k_t, v_t, a_t, b_t = inp
        sa = jnp.einsum('bhvk,bhk->bhv', st, a_t)
        st = st * w_t[:, :, None, :] + sa[..., None] * b_t[:, :, None, :] + v_t[..., None] * k_t[:, :, None, :]
        return st, jnp.einsum('bhvk,bhk->bhv', st, r_t)

    _, y = lax.scan(step, jnp.zeros((B, H, N, N), jnp.float32), xs)
    return jnp.moveaxis(y, 0, 1)


def gated_delta_rule_chunked(q, k, v, g, beta):
    B, S, H, dk = q.shape
    dv = v.shape[-1]
    q = to_chunks(q) * dk ** -0.5
    k, v, g, beta = to_chunks(k), to_chunks(v), to_chunks(g), to_chunks(beta)
    gc = jnp.cumsum(g, axis=-1)
    lower = jnp.tril(jnp.ones((CHUNK, CHUNK), bool))
    strict = jnp.tril(jnp.ones((CHUNK, CHUNK), bool), -1)
    diff = gc[..., :, None] - gc[..., None, :]
    decay = jnp.where(lower, jnp.exp(jnp.where(lower, diff, 0.0)), 0.0)
    k_beta = k * beta[..., None]
    a_mat = jnp.where(strict, jnp.einsum('bnhid,bnhjd->bnhij', k_beta, k) * decay, 0.0)
    t_mat = a_mat + jnp.eye(CHUNK, dtype=jnp.float32)
    solve = lambda rhs: lax.linalg.triangular_solve(t_mat, rhs, left_side=True, lower=True, unit_diagonal=True)
    value = solve(v * beta[..., None])
    k_cum = solve(k_beta * jnp.exp(gc)[..., None])
    attn = jnp.einsum('bnhid,bnhjd->bnhij', q, k) * decay
    q_dec = q * jnp.exp(gc)[..., None]
    k_dec = k * jnp.exp(gc[..., -1:] - gc)[..., None]
    g_last = jnp.exp(gc[..., -1])

    def step(st, inp):
        value_n, kcum_n, attn_n, qdec_n, kdec_n, glast_n = inp
        v_new = value_n - jnp.einsum('bhcd,bhde->bhce', kcum_n, st)
        o = jnp.einsum('bhcd,bhde->bhce', qdec_n, st) + jnp.einsum('bhij,bhje->bhie', attn_n, v_new)
        st = st * glast_n[..., None, None] + jnp.einsum('bhcd,bhce->bhde', kdec_n, v_new)
        return st, o

    xs = tuple(jnp.moveaxis(t, 1, 0) for t in (value, k_cum, attn, q_dec, k_dec, g_last))
    _, o = lax.scan(step, jnp.zeros((B, H, dk, dv), jnp.float32), xs)
    return from_chunks(jnp.moveaxis(o, 0, 1))


def retention_chunked(q, k, v):
    B, S, H, dk = q.shape
    dv = v.shape[-1]
    lg = jnp.log(1.0 - 2.0 ** (-5.0 - jnp.arange(H, dtype=jnp.float32)))
    q, k, v = to_chunks(q), to_chunks(k), to_chunks(v)
    idx = jnp.arange(CHUNK, dtype=jnp.float32)
    diff = idx[:, None] - idx[None, :]
    dmat = jnp.where(diff >= 0, jnp.exp(jnp.maximum(diff, 0.0)[None] * lg[:, None, None]), 0.0)
    inner = jnp.einsum('bnhij,bnhje->bnhie', jnp.einsum('bnhid,bnhjd->bnhij', q, k) * dmat, v)
    q_dec = q * jnp.exp((idx + 1.0)[None, :] * lg[:, None])[..., None]
    k_dec = k * jnp.exp((CHUNK - 1.0 - idx)[None, :] * lg[:, None])[..., None]
    u = jnp.einsum('bnhcd,bnhce->bnhde', k_dec, v)
    g_chunk = jnp.exp(CHUNK * lg)[:, None, None]

    def step(rst, u_n):
        return rst * g_chunk + u_n, rst

    _, r_prev = lax.scan(step, jnp.zeros((B, H, dk, dv), jnp.float32), jnp.moveaxis(u, 1, 0))
    cross = jnp.einsum('bnhcd,nbhde->bnhce', q_dec, r_prev)
    return from_chunks(inner + cross)


def mla_rwkv_mixer(x, positions, w_in, q_norm, w_uq, kv_norm, w_ukv, rwkv_mu, rwkv_w0, rwkv_w2,
                   rwkv_a0, rwkv_a2, rwkv_g2, rwkv_k_k, rwkv_k_a, rwkv_r_k, rwkv_gn_w, rwkv_gn_b, w_out):
    B, S, _ = x.shape
    p = x @ w_in
    mla_p, rwkv_p = p[..., :MLA_COLS], p[..., MLA_COLS:]
    c_q, c_kv, k_pe = split_cols(mla_p, (MLA_Q_RANK, MLA_KV_RANK, MLA_ROPE))
    q = (rms_norm(c_q, q_norm) @ w_uq).reshape(B, S, MLA_HEADS, MLA_NOPE + MLA_ROPE)
    q_nope, q_pe = q[..., :MLA_NOPE], rope(q[..., MLA_NOPE:], positions)
    kv = (rms_norm(c_kv, kv_norm) @ w_ukv).reshape(B, S, MLA_HEADS, MLA_NOPE + MLA_V)
    k_nope, v_mla = kv[..., :MLA_NOPE], kv[..., MLA_NOPE:]
    k_pe = rope(k_pe[:, :, None, :], positions)[:, :, 0, :]
    o_mla = causal_block_attention(q_nope, q_pe, k_nope, k_pe, v_mla, (MLA_NOPE + MLA_ROPE) ** -0.5)
    o_mla = o_mla.reshape(B, S, MLA_HEADS * MLA_V)
    prev = jnp.pad(rwkv_p, ((0, 0), (1, 0), (0, 0)))[:, :-1]
    rwkv_p = rwkv_p + rwkv_mu * (prev - rwkv_p)
    r, k, v, xw, xa, xg = split_cols(rwkv_p, (RWKV_DIM, RWKV_DIM, RWKV_DIM, DECAY_LORA, AAA_LORA, GATE_LORA))
    w_log = -jax.nn.softplus(-(rwkv_w0 + jnp.tanh(xw) @ rwkv_w2)) - 0.5
    decay = jnp.exp(-jnp.exp(w_log.astype(jnp.float32)))
    a = jax.nn.sigmoid(rwkv_a0 + xa @ rwkv_a2)
    g = jax.nn.sigmoid(xg) @ rwkv_g2
    hs = lambda t: t.reshape(B, S, RWKV_HEADS, RWKV_HEAD)
    kk = l2_normalize(hs(k * rwkv_k_k))
    k = k * (1.0 + (a - 1.0) * rwkv_k_a)
    r_h, k_h, v_h, a_h = hs(r), hs(k), hs(v), hs(a)
    y = rwkv7_recurrence(r_h, hs(decay), k_h, v_h, -kk, kk * a_h)
    y = head_norm(y, rwkv_gn_w, rwkv_gn_b, RWKV_GN_EPS)
    bonus = jnp.sum(r_h * k_h * rwkv_r_k, axis=-1, keepdims=True) * v_h
    o_rwkv = ((y + bonus.reshape(B, S, RWKV_DIM)) * g).astype(x.dtype)
    return jnp.concatenate([o_mla.astype(x.dtype), o_rwkv], -1) @ w_out


def gdn_retention_mixer(x, positions, w_in, gdn_conv_w, gdn_A_log, gdn_dt_bias, gdn_norm,
                        ret_gn_w, ret_gn_b, w_out):
    B, S, _ = x.shape
    p = x @ w_in
    gdn_p, ret_p = p[..., :GDN_COLS], p[..., GDN_COLS:]
    qk_w, v_w = GDN_QK_HEADS * GDN_DK, GDN_V_HEADS * GDN_DV
    qkv, z, b_logit, a_logit = split_cols(gdn_p, (2 * qk_w + v_w, v_w, GDN_V_HEADS, GDN_V_HEADS))
    qkv = jax.nn.silu(causal_dwconv(qkv, gdn_conv_w))
    q, k, v = split_cols(qkv, (qk_w, qk_w, v_w))
    rep = GDN_V_HEADS // GDN_QK_HEADS
    q = jnp.repeat(l2_normalize(q.reshape(B, S, GDN_QK_HEADS, GDN_DK)), rep, axis=2)
    k = jnp.repeat(l2_normalize(k.reshape(B, S, GDN_QK_HEADS, GDN_DK)), rep, axis=2)
    v = v.reshape(B, S, GDN_V_HEADS, GDN_DV)
    beta = jax.nn.sigmoid(b_logit.astype(jnp.float32))
    g = -jnp.exp(gdn_A_log.astype(jnp.float32)) * jax.nn.softplus((a_logit + gdn_dt_bias).astype(jnp.float32))
    o = gated_delta_rule_chunked(q, k, v, g, beta)
    o = rms_norm(o, gdn_norm) * jax.nn.silu(z.reshape(B, S, GDN_V_HEADS, GDN_DV).astype(jnp.float32))
    o_gdn = o.reshape(B, S, v_w).astype(x.dtype)
    rq, rk, rv, rg = split_cols(ret_p, (RET_HEADS * RET_DK, RET_HEADS * RET_DK, RET_HEADS * RET_DV, RET_HEADS * RET_DV))
    rq = rope(rq.reshape(B, S, RET_HEADS, RET_DK), positions) * RET_DK ** -0.5
    rk = rope(rk.reshape(B, S, RET_HEADS, RET_DK), positions)
    o_ret = retention_chunked(rq, rk, rv.reshape(B, S, RET_HEADS, RET_DV))
    o_ret = (head_norm(o_ret, ret_gn_w, ret_gn_b, LN_EPS) * jax.nn.silu(rg.astype(jnp.float32))).astype(x.dtype)
    return jnp.concatenate([o_gdn, o_ret], -1) @ w_out


def conv_ffn(x, w_gate, w_val, conv_w, conv_b, w_down):
    h = causal_dwconv(x @ w_gate, conv_w) + conv_b
    return (jax.nn.silu(h) * (x @ w_val)) @ w_down


def setup_inputs(seed: int = 0) -> dict:
    key = jax.random.key(seed)
    ks = iter(jax.random.split(key, 64))
    f32 = jnp.float32

    def nrm(shape, scale):
        return jax.random.normal(next(ks), shape, f32) * scale

    def gain(n):
        return 1.0 + nrm((n,), 0.02)

    def bias(n):
        return nrm((n,), 0.02)

    inp = {}
    inp["x"] = nrm((BATCH, SEQ, D_MODEL), 1.0)
    offs = jax.random.randint(next(ks), (BATCH, 1), 0, 1024, dtype=jnp.int32)
    inp["positions"] = offs + jnp.arange(SEQ, dtype=jnp.int32)[None, :]
    inp["l0_w_in"] = nrm((D_MODEL, EVEN_IN), D_MODEL ** -0.5)
    inp["l0_q_norm"] = gain(MLA_Q_RANK)
    inp["l0_w_uq"] = nrm((MLA_Q_RANK, MLA_HEADS * (MLA_NOPE + MLA_ROPE)), MLA_Q_RANK ** -0.5)
    inp["l0_kv_norm"] = gain(MLA_KV_RANK)
    inp["l0_w_ukv"] = nrm((MLA_KV_RANK, MLA_HEADS * (MLA_NOPE + MLA_V)), MLA_KV_RANK ** -0.5)
    inp["l0_rwkv_mu"] = jax.random.uniform(next(ks), (RWKV_COLS,), f32)
    w0_base = -6.0 + 5.0 * jnp.arange(RWKV_HEAD, dtype=f32) / (RWKV_HEAD - 1)
    inp["l0_rwkv_w0"] = jnp.tile(w0_base, RWKV_HEADS) + nrm((RWKV_DIM,), 0.1)
    inp["l0_rwkv_w2"] = nrm((DECAY_LORA, RWKV_DIM), 0.5 * DECAY_LORA ** -0.5)
    inp["l0_rwkv_a0"] = nrm((RWKV_DIM,), 0.1)
    inp["l0_rwkv_a2"] = nrm((AAA_LORA, RWKV_DIM), AAA_LORA ** -0.5)
    inp["l0_rwkv_g2"] = nrm((GATE_LORA, RWKV_DIM), GATE_LORA ** -0.5)
    inp["l0_rwkv_k_k"] = 0.85 + nrm((RWKV_DIM,), 0.02)
    inp["l0_rwkv_k_a"] = gain(RWKV_DIM)
    inp["l0_rwkv_r_k"] = nrm((RWKV_HEADS, RWKV_HEAD), 0.1)
    inp["l0_rwkv_gn_w"] = gain(RWKV_DIM)
    inp["l0_rwkv_gn_b"] = bias(RWKV_DIM)
    inp["l0_w_out"] = nrm((EVEN_OUT, D_MODEL), BETA * EVEN_OUT ** -0.5)
    inp["l0_ln1_g"] = gain(D_MODEL)
    inp["l0_ln1_b"] = bias(D_MODEL)
    inp["l0_ffn_w_gate"] = nrm((D_MODEL, FFN_DIM), D_MODEL ** -0.5)
    inp["l0_ffn_w_val"] = nrm((D_MODEL, FFN_DIM), D_MODEL ** -0.5)
    inp["l0_ffn_conv_w"] = nrm((FFN_CONV, FFN_DIM), FFN_CONV ** -0.5)
    inp["l0_ffn_conv_b"] = bias(FFN_DIM)
    inp["l0_ffn_w_down"] = nrm((FFN_DIM, D_MODEL), BETA * FFN_DIM ** -0.5)
    inp["l0_ln2_g"] = gain(D_MODEL)
    inp["l0_ln2_b"] = bias(D_MODEL)
    inp["l1_w_in"] = nrm((D_MODEL, ODD_IN), D_MODEL ** -0.5)
    inp["l1_gdn_conv_w"] = nrm((GDN_CONV, 2 * GDN_QK_HEADS * GDN_DK + GDN_V_HEADS * GDN_DV), GDN_CONV ** -0.5)
    inp["l1_gdn_A_log"] = jnp.log(jax.random.uniform(next(ks), (GDN_V_HEADS,), f32, 1.0, 16.0))
    dt = jnp.exp(jax.random.uniform(next(ks), (GDN_V_HEADS,), f32, float(np.log(1e-3)), float(np.log(1e-1))))
    inp["l1_gdn_dt_bias"] = dt + jnp.log(-jnp.expm1(-dt))
    inp["l1_gdn_norm"] = gain(GDN_DV)
    inp["l1_ret_gn_w"] = gain(RET_HEADS * RET_DV)
    inp["l1_ret_gn_b"] = bias(RET_HEADS * RET_DV)
    inp["l1_w_out"] = nrm((ODD_OUT, D_MODEL), BETA * ODD_OUT ** -0.5)
    inp["l1_ln1_g"] = gain(D_MODEL)
    inp["l1_ln1_b"] = bias(D_MODEL)
    inp["l1_ffn_w_gate"] = nrm((D_MODEL, FFN_DIM), D_MODEL ** -0.5)
    inp["l1_ffn_w_val"] = nrm((D_MODEL, FFN_DIM), D_MODEL ** -0.5)
    inp["l1_ffn_conv_w"] = nrm((FFN_CONV, FFN_DIM), FFN_CONV ** -0.5)
    inp["l1_ffn_conv_b"] = bias(FFN_DIM)
    inp["l1_ffn_w_down"] = nrm((FFN_DIM, D_MODEL), BETA * FFN_DIM ** -0.5)
    inp["l1_ln2_g"] = gain(D_MODEL)
    inp["l1_ln2_b"] = bias(D_MODEL)
    return inp


def reference(x, positions,
              l0_w_in, l0_q_norm, l0_w_uq, l0_kv_norm, l0_w_ukv, l0_rwkv_mu, l0_rwkv_w0, l0_rwkv_w2,
              l0_rwkv_a0, l0_rwkv_a2, l0_rwkv_g2, l0_rwkv_k_k, l0_rwkv_k_a, l0_rwkv_r_k, l0_rwkv_gn_w,
              l0_rwkv_gn_b, l0_w_out, l0_ln1_g, l0_ln1_b, l0_ffn_w_gate, l0_ffn_w_val, l0_ffn_conv_w,
              l0_ffn_conv_b, l0_ffn_w_down, l0_ln2_g, l0_ln2_b,
              l1_w_in, l1_gdn_conv_w, l1_gdn_A_log, l1_gdn_dt_bias, l1_gdn_norm, l1_ret_gn_w, l1_ret_gn_b,
              l1_w_out, l1_ln1_g, l1_ln1_b, l1_ffn_w_gate, l1_ffn_w_val, l1_ffn_conv_w, l1_ffn_conv_b,
              l1_ffn_w_down, l1_ln2_g, l1_ln2_b):
    mixers = (
        lambda h: mla_rwkv_mixer(h, positions, l0_w_in, l0_q_norm, l0_w_uq, l0_kv_norm, l0_w_ukv,
                                 l0_rwkv_mu, l0_rwkv_w0, l0_rwkv_w2, l0_rwkv_a0, l0_rwkv_a2, l0_rwkv_g2,
                                 l0_rwkv_k_k, l0_rwkv_k_a, l0_rwkv_r_k, l0_rwkv_gn_w, l0_rwkv_gn_b, l0_w_out),
        lambda h: gdn_retention_mixer(h, positions, l1_w_in, l1_gdn_conv_w, l1_gdn_A_log, l1_gdn_dt_bias,
                                      l1_gdn_norm, l1_ret_gn_w, l1_ret_gn_b, l1_w_out),
    )
    ffns = ((l0_ffn_w_gate, l0_ffn_w_val, l0_ffn_conv_w, l0_ffn_conv_b, l0_ffn_w_down),
            (l1_ffn_w_gate, l1_ffn_w_val, l1_ffn_conv_w, l1_ffn_conv_b, l1_ffn_w_down))
    ln1 = ((l0_ln1_g, l0_ln1_b), (l1_ln1_g, l1_ln1_b))
    ln2 = ((l0_ln2_g, l0_ln2_b), (l1_ln2_g, l1_ln2_b))
    for layer in range(DEPTH):
        x = layer_norm(ALPHA * x + mixers[layer](x), *ln1[layer])
        x = layer_norm(ALPHA * x + conv_ffn(x, *ffns[layer]), *ln2[layer])
    return x
```

```python
import functools
import math

import jax
import jax.numpy as jnp
from jax import lax
from jax.experimental import pallas as pl
from jax.experimental.pallas import tpu as pltpu

F32 = jnp.float32
BF16 = jnp.bfloat16

DEPTH = 2
ALPHA = (2 * DEPTH) ** 0.25
LN_EPS = 1e-5
RMS_EPS = 1e-6
ROPE_THETA = 10000.0

MLA_HEADS = 8
MLA_Q_RANK = 448
MLA_KV_RANK = 128
MLA_NOPE = 128
MLA_ROPE = 64
MLA_V = 128
RWKV_HEADS = 16
RWKV_HEAD = 64
RWKV_DIM = RWKV_HEADS * RWKV_HEAD
DECAY_LORA = 64
AAA_LORA = 64
GATE_LORA = 160
RWKV_GN_EPS = 64e-5
GDN_QK_HEADS = 4
GDN_V_HEADS = 8
GDN_DK = 128
GDN_DV = 128
GDN_CONV = 4
RET_HEADS = 8
RET_DK = 64
RET_DV = 128
FFN_CONV = 3

CHUNK = 64
LANES = 128
SUBLANES = 8
VMEM_LIMIT = 56 * 1024 * 1024


def _cparams(sem):
    return pltpu.CompilerParams(dimension_semantics=sem, vmem_limit_bytes=VMEM_LIMIT)


def _dot(a, b):
    return jnp.dot(a, b, preferred_element_type=F32)


def _dot_nt(a, b):
    return lax.dot_general(a, b, (((1,), (1,)), ((), ())), preferred_element_type=F32)


def _dot_tn(a, b):
    return lax.dot_general(a, b, (((0,), (0,)), ((), ())), preferred_element_type=F32)


def _split3(x):
    hi = x.astype(BF16)
    r1 = x - hi.astype(F32)
    mid = r1.astype(BF16)
    lo = (r1 - mid.astype(F32)).astype(BF16)
    return hi, mid, lo


def _dot_exact_lhs(m_bf16, x):
    hi, mid, lo = _split3(x)
    return _dot(m_bf16, hi) + _dot(m_bf16, mid) + _dot(m_bf16, lo)


def _mm_kernel(a_ref, w_ref, o_ref, *scratch, nk):
    if nk == 1:
        o_ref[...] = _dot(a_ref[...].astype(BF16), w_ref[...]).astype(o_ref.dtype)
        return
    acc_ref, = scratch
    k = pl.program_id(2)

    @pl.when(k == 0)
    def _():
        acc_ref[...] = jnp.zeros_like(acc_ref)

    acc_ref[...] += _dot(a_ref[...].astype(BF16), w_ref[...])

    @pl.when(k == nk - 1)
    def _():
        o_ref[...] = acc_ref[...].astype(o_ref.dtype)


def _pick(n, pref):
    for t in pref:
        if n % t == 0:
            return t
    return n


def mm(a, w, out_dtype=F32, tm=512, tn=None, tk=None):
    M, K = a.shape
    N = w.shape[1]
    tm = _pick(M, (tm, 256, 128, 64, 32, 16, 8))
    tn = tn or _pick(N, (512, 384, 256, 128))
    tk = tk or K
    nk = K // tk
    w = w.astype(BF16)
    scratch = [] if nk == 1 else [pltpu.VMEM((tm, tn), F32)]
    return pl.pallas_call(
        functools.partial(_mm_kernel, nk=nk),
        grid=(M // tm, N // tn, nk),
        in_specs=[pl.BlockSpec((tm, tk), lambda i, j, k: (i, k)),
                  pl.BlockSpec((tk, tn), lambda i, j, k: (k, j))],
        out_specs=pl.BlockSpec((tm, tn), lambda i, j, k: (i, j)),
        out_shape=jax.ShapeDtypeStruct((M, N), out_dtype),
        scratch_shapes=scratch,
        compiler_params=_cparams(("parallel", "parallel", "arbitrary")),
        name="mm",
    )(a, w)


def _mm_ln_kernel(a_ref, w_ref, x_ref, g_ref, b_ref, o_ref, acc_ref, *, nk):
    k = pl.program_id(1)

    @pl.when(k == 0)
    def _():
        acc_ref[...] = jnp.zeros_like(acc_ref)

    acc_ref[...] += _dot(a_ref[...].astype(BF16), w_ref[...])

    @pl.when(k == nk - 1)
    def _():
        y = ALPHA * x_ref[...] + acc_ref[...]
        mu = jnp.mean(y, axis=-1, keepdims=True)
        yc = y - mu
        var = jnp.mean(yc * yc, axis=-1, keepdims=True)
        o_ref[...] = yc * lax.rsqrt(var + LN_EPS) * g_ref[...] + b_ref[...]


def mm_ln(a, w, x, g, b, tm=512, tk=None):
    M, K = a.shape
    N = w.shape[1]
    tm = _pick(M, (tm, 256, 128, 64, 32, 16, 8))
    tk = tk or _pick(K, (1024, 1408, 512, 256, 128))
    nk = K // tk
    return pl.pallas_call(
        functools.partial(_mm_ln_kernel, nk=nk),
        grid=(M // tm, nk),
        in_specs=[pl.BlockSpec((tm, tk), lambda i, k: (i, k)),
                  pl.BlockSpec((tk, N), lambda i, k: (k, 0)),
                  pl.BlockSpec((tm, N), lambda i, k: (i, 0)),
                  pl.BlockSpec((1, N), lambda i, k: (0, 0)),
                  pl.BlockSpec((1, N), lambda i, k: (0, 0))],
        out_specs=pl.BlockSpec((tm, N), lambda i, k: (i, 0)),
        out_shape=jax.ShapeDtypeStruct((M, N), F32),
        scratch_shapes=[pltpu.VMEM((tm, N), F32)],
        compiler_params=_cparams(("parallel", "arbitrary")),
        name="mm_ln",
    )(a, w.astype(BF16), x, g.reshape(1, N), b.reshape(1, N))


def _ffn1_kernel(x_ref, xh_ref, wg_ref, wv_ref, cw_ref, cb_ref, o_ref, *, tiles_per_seq):
    m = pl.program_id(0)
    x = x_ref[...].astype(BF16)
    g = _dot(x, wg_ref[...])
    gh = _dot(xh_ref[...].astype(BF16), wg_ref[...])
    gh = jnp.where(m % tiles_per_seq == 0, 0.0, gh)
    rows = lax.broadcasted_iota(jnp.int32, g.shape, 0)
    g1 = jnp.where(rows == 0, gh[SUBLANES - 1:SUBLANES], pltpu.roll(g, 1, 0))
    g2 = jnp.where(rows == 0, gh[SUBLANES - 2:SUBLANES - 1],
                   jnp.where(rows == 1, gh[SUBLANES - 1:SUBLANES], pltpu.roll(g, 2, 0)))
    cw = cw_ref[...]
    h = g * cw[2:3] + g1 * cw[1:2] + g2 * cw[0:1] + cb_ref[...]
    v = _dot(x, wv_ref[...])
    o_ref[...] = (h * jax.nn.sigmoid(h) * v).astype(o_ref.dtype)


def ffn1(x, w_gate, w_val, conv_w, conv_b, seq, tm=512, tf=512):
    M, D = x.shape
    Fd = w_gate.shape[1]
    tm = _pick(seq, (tm, 256, 128, 64, 32, 16, 8))
    tf = _pick(Fd, (tf, 256, 128))
    hb = tm // SUBLANES
    return pl.pallas_call(
        functools.partial(_ffn1_kernel, tiles_per_seq=seq // tm),
        grid=(M // tm, Fd // tf),
        in_specs=[pl.BlockSpec((tm, D), lambda i, j: (i, 0)),
                  pl.BlockSpec((SUBLANES, D), lambda i, j: (jnp.maximum(i * hb - 1, 0), 0)),
                  pl.BlockSpec((D, tf), lambda i, j: (0, j)),
                  pl.BlockSpec((D, tf), lambda i, j: (0, j)),
                  pl.BlockSpec((FFN_CONV, tf), lambda i, j: (0, j)),
                  pl.BlockSpec((1, tf), lambda i, j: (0, j))],
        out_specs=pl.BlockSpec((tm, tf), lambda i, j: (i, j)),
        out_shape=jax.ShapeDtypeStruct((M, Fd), BF16),
        compiler_params=_cparams(("parallel", "parallel")),
        name="ffn1",
    )(x, x, w_gate.astype(BF16), w_val.astype(BF16), conv_w, conv_b.reshape(1, Fd))


def _flash_kernel(q_ref, k_ref, v_ref, o_ref, m_ref, l_ref, acc_ref, *, tq, tk):
    qi = pl.program_id(2)
    ki = pl.program_id(3)
    last = (qi * tq + tq - 1) // tk

    @pl.when(ki == 0)
    def _():
        m_ref[...] = jnp.full_like(m_ref, -jnp.inf)
        l_ref[...] = jnp.zeros_like(l_ref)
        acc_ref[...] = jnp.zeros_like(acc_ref)

    @pl.when(ki <= last)
    def _():
        s = _dot_nt(q_ref[0, 0], k_ref[0, 0])
        qpos = qi * tq + lax.broadcasted_iota(jnp.int32, s.shape, 0)
        kpos = ki * tk + lax.broadcasted_iota(jnp.int32, s.shape, 1)
        s = jnp.where(kpos <= qpos, s, -jnp.inf)
        m_old = m_ref[...]
        m_new = jnp.maximum(m_old, jnp.max(s, axis=-1, keepdims=True))
        p = jnp.exp(s - m_new)
        corr = jnp.exp(m_old - m_new)
        l_ref[...] = corr * l_ref[...] + jnp.sum(p, axis=-1, keepdims=True)
        acc_ref[...] = corr * acc_ref[...] + _dot(p.astype(BF16), v_ref[0, 0])
        m_ref[...] = m_new

    @pl.when(ki == last)
    def _():
        o_ref[...] = (acc_ref[...] / l_ref[...]).astype(o_ref.dtype)


def flash_attention(q, k, v, tq=512, tk=512):
    B, H, S, dq = q.shape
    dv = v.shape[-1]
    tq = _pick(S, (tq, 256, 128))
    tk = _pick(S, (tk, 256, 128))
    nq, nk = S // tq, S // tk

    def kv_map(b, h, i, j):
        return (b, h, jnp.minimum(j, (i * tq + tq - 1) // tk), 0)

    return pl.pallas_call(
        functools.partial(_flash_kernel, tq=tq, tk=tk),
        grid=(B, H, nq, nk),
        in_specs=[pl.BlockSpec((1, 1, tq, dq), lambda b, h, i, j: (b, h, i, 0)),
                  pl.BlockSpec((1, 1, tk, dq), kv_map),
                  pl.BlockSpec((1, 1, tk, dv), kv_map)],
        out_specs=pl.BlockSpec((tq, dv), lambda b, h, i, j: (b * nq + i, h)),
        out_shape=jax.ShapeDtypeStruct((B * S, H * dv), BF16),
        scratch_shapes=[pltpu.VMEM((tq, 1), F32), pltpu.VMEM((tq, 1), F32), pltpu.VMEM((tq, dv), F32)],
        compiler_params=_cparams(("parallel", "parallel", "parallel", "arbitrary")),
        name="flash",
    )(q, k, v)


def _tri_inv_kernel(a_ref, t_ref):
    C = a_ref.shape[1]
    kk = lax.broadcasted_iota(jnp.int32, (C, LANES), 0)
    for i in range(C):
        acc = jnp.where(kk == i, 1.0, 0.0).astype(F32)
        if i > 0:
            def body(j, acc, i=i):
                return acc + a_ref[0, i, pl.ds(j, 1), :] * t_ref[0, j]
            acc = lax.fori_loop(0, i, body, acc)
        t_ref[0, i] = acc


def tri_inv(a):
    n_in, C, _ = a.shape
    ng = -(-n_in // LANES)
    N = ng * LANES
    a = jnp.pad(a, ((0, N - n_in), (0, 0), (0, 0)))
    al = a.reshape(ng, LANES, C * C).transpose(0, 2, 1).reshape(ng, C, C, LANES)
    tl = pl.pallas_call(
        _tri_inv_kernel,
        grid=(ng,),
        in_specs=[pl.BlockSpec((1, C, C, LANES), lambda i: (i, 0, 0, 0))],
        out_specs=pl.BlockSpec((1, C, C, LANES), lambda i: (i, 0, 0, 0)),
        out_shape=jax.ShapeDtypeStruct((ng, C, C, LANES), F32),
        compiler_params=_cparams(("parallel",)),
        name="tri_inv",
    )(al)
    return tl.reshape(ng, C * C, LANES).transpose(0, 2, 1).reshape(N, C, C)[:n_in]


def _chunk_masks():
    C = CHUNK
    row = lax.broadcasted_iota(jnp.int32, (C, C), 0)
    col = lax.broadcasted_iota(jnp.int32, (C, C), 1)
    return row, col


def _rwkv_a_kernel(r_ref, lw_ref, k_ref, v_ref, av_ref, bv_ref,
                   aab_ref, mrb_ref, at_ref, akv_ref, rt_ref, bh_ref, yk_ref, kv_ref, egl_ref, *, G):
    C = CHUNK
    row, col = _chunk_masks()
    strict = col < row
    incl = col <= row
    tri = jnp.where(incl, 1.0, 0.0).astype(BF16)
    lane = lax.broadcasted_iota(jnp.int32, (C, LANES), 1)
    head0 = lane < RWKV_HEAD
    bd = (lax.broadcasted_iota(jnp.int32, (LANES, LANES), 0) // RWKV_HEAD ==
          lax.broadcasted_iota(jnp.int32, (LANES, LANES), 1) // RWKV_HEAD)

    def body(g, carry):
        sl = pl.ds(pl.multiple_of(g * C, C), C)
        lw = lw_ref[sl, :]
        gc = _dot_exact_lhs(tri, lw)
        gl = gc[C - 1:C, :]
        eg = jnp.exp(gc)
        ieg = jnp.exp(-gc)
        egl = jnp.exp(gl - gc)
        av, bv, k, r = av_ref[sl, :], bv_ref[sl, :], k_ref[sl, :], r_ref[sl, :]
        vb = v_ref[sl, :].astype(BF16)
        at = av * jnp.exp(gc - lw)
        rt = r * eg
        bt = (bv * ieg).astype(BF16)
        kt = (k * ieg).astype(BF16)
        akv = jnp.zeros((C, LANES), F32)
        yk = jnp.zeros((C, LANES), F32)
        for hh in range(2):
            msk = head0 if hh == 0 else jnp.logical_not(head0)
            lhs = jnp.concatenate([jnp.where(msk, at, 0.0), jnp.where(msk, rt, 0.0)], axis=0).astype(BF16)
            xb = _dot_nt(lhs, bt)
            xk = _dot_nt(lhs, kt)
            aab_ref[g, hh] = jnp.where(strict, xb[:C], 0.0)
            mrb_ref[g, hh] = jnp.where(incl, xb[C:], 0.0).astype(BF16)
            aak = jnp.where(strict, xk[:C], 0.0).astype(BF16)
            mrk = jnp.where(incl, xk[C:], 0.0).astype(BF16)
            akv = jnp.where(msk, _dot(aak, vb), akv)
            yk = jnp.where(msk, _dot(mrk, vb), yk)
        at_ref[sl, :] = at.astype(BF16)
        akv_ref[sl, :] = akv.astype(BF16)
        rt_ref[sl, :] = rt.astype(BF16)
        bh_ref[sl, :] = (bv * egl).astype(BF16)
        yk_ref[sl, :] = yk
        kv_ref[g, 0] = jnp.where(bd, _dot_tn(vb, (k * egl).astype(BF16)), 0.0)
        egl_ref[g] = jnp.exp(gl)
        return carry

    lax.fori_loop(0, G, body, 0)


def _rwkv_c_kernel(at_ref, akv_ref, rt_ref, bh_ref, yk_ref, t_ref, mrb_ref, kv_ref, egl_ref,
                   y_ref, s_ref, *, G):
    C = CHUNK
    npair = RWKV_HEADS // 2

    @pl.when(pl.program_id(1) == 0)
    def _():
        s_ref[...] = jnp.zeros_like(s_ref)

    lane = lax.broadcasted_iota(jnp.int32, (C, LANES), 1)
    head0 = lane < RWKV_HEAD
    head0w = jnp.concatenate([head0, head0], axis=1)
    bd = (lax.broadcasted_iota(jnp.int32, (LANES, LANES), 0) // RWKV_HEAD ==
          lax.broadcasted_iota(jnp.int32, (LANES, LANES), 1) // RWKV_HEAD)

    def body(g, carry):
        sl = pl.ds(pl.multiple_of(g * C, C), C)
        for p in range(npair):
            cs = slice(p * LANES, (p + 1) * LANES)
            both = jnp.concatenate([at_ref[sl, cs], akv_ref[sl, cs]], axis=1)
            w0 = _dot(t_ref[g, 2 * p].astype(BF16), both)
            w1 = _dot(t_ref[g, 2 * p + 1].astype(BF16), both)
            w = jnp.where(head0w, w0, w1)
            s = s_ref[p]
            sb = s.astype(BF16)
            u = _dot_nt(w[:, :LANES].astype(BF16), sb) + w[:, LANES:]
            ub = u.astype(BF16)
            y = _dot_nt(rt_ref[sl, cs], sb) + yk_ref[sl, cs]
            y = y + jnp.where(head0, _dot(mrb_ref[g, 2 * p], ub), _dot(mrb_ref[g, 2 * p + 1], ub))
            y_ref[sl, cs] = y
            s_ref[p] = s * egl_ref[g, :, cs] + jnp.where(bd, _dot_tn(ub, bh_ref[sl, cs]), 0.0) + kv_ref[g, p]
        return carry

    lax.fori_loop(0, G, body, 0)


def rwkv7_chunked(r, lw, k, v, av, bv, batch, G=4):
    T, HD = r.shape
    C = CHUNK
    seq = T // batch
    nct = T // C
    G = _pick(seq // C, (G, 2, 1))
    npair = HD // LANES
    nh = 2 * npair
    row_spec = pl.BlockSpec((C * G, LANES), lambda i, p: (i, p))
    outs = pl.pallas_call(
        functools.partial(_rwkv_a_kernel, G=G),
        grid=(nct // G, npair),
        in_specs=[row_spec] * 6,
        out_specs=[pl.BlockSpec((G, 2, C, C), lambda i, p: (i, p, 0, 0)),
                   pl.BlockSpec((G, 2, C, C), lambda i, p: (i, p, 0, 0)),
                   row_spec, row_spec, row_spec, row_spec, row_spec,
                   pl.BlockSpec((G, 1, LANES, LANES), lambda i, p: (i, p, 0, 0)),
                   pl.BlockSpec((G, 1, LANES), lambda i, p: (i, 0, p))],
        out_shape=[jax.ShapeDtypeStruct((nct, nh, C, C), F32),
                   jax.ShapeDtypeStruct((nct, nh, C, C), BF16),
                   jax.ShapeDtypeStruct((T, HD), BF16),
                   jax.ShapeDtypeStruct((T, HD), BF16),
                   jax.ShapeDtypeStruct((T, HD), BF16),
                   jax.ShapeDtypeStruct((T, HD), BF16),
                   jax.ShapeDtypeStruct((T, HD), F32),
                   jax.ShapeDtypeStruct((nct, npair, LANES, LANES), F32),
                   jax.ShapeDtypeStruct((nct, 1, HD), F32)],
        compiler_params=_cparams(("parallel", "parallel")),
        name="rwkv_a",
    )(r, lw, k, v, av, bv)
    aab, mrb, at, akv, rt, bh, yk, kv, egl = outs
    tinv = tri_inv(aab.reshape(nct * nh, C, C)).reshape(nct, nh, C, C)
    ncb = seq // C // G
    wide = pl.BlockSpec((C * G, HD), lambda b, n: (b * ncb + n, 0))
    return pl.pallas_call(
        functools.partial(_rwkv_c_kernel, G=G),
        grid=(batch, ncb),
        in_specs=[wide, wide, wide, wide, wide,
                  pl.BlockSpec((G, nh, C, C), lambda b, n: (b * ncb + n, 0, 0, 0)),
                  pl.BlockSpec((G, nh, C, C), lambda b, n: (b * ncb + n, 0, 0, 0)),
                  pl.BlockSpec((G, npair, LANES, LANES), lambda b, n: (b * ncb + n, 0, 0, 0)),
                  pl.BlockSpec((G, 1, HD), lambda b, n: (b * ncb + n, 0, 0))],
        out_specs=wide,
        out_shape=jax.ShapeDtypeStruct((T, HD), F32),
        scratch_shapes=[pltpu.VMEM((npair, LANES, LANES), F32)],
        compiler_params=_cparams(("arbitrary", "arbitrary")),
        name="rwkv_c",
    )(at, akv, rt, bh, yk, tinv, mrb, kv, egl)


def _gdn_a_kernel(q_ref, k_ref, v_ref, gcol_ref, grow_ref, bcol_ref,
                  xm_ref, attn_ref, vb_ref, kcp_ref, qd_ref, kd_ref, egl_ref, *, G):
    C = CHUNK
    row, col = _chunk_masks()
    strict = col < row
    incl = col <= row
    tri = jnp.where(incl, 1.0, 0.0).astype(BF16)
    triu = jnp.where(row <= col, 1.0, 0.0).astype(BF16)

    def body(g, carry):
        sl = pl.ds(pl.multiple_of(g * C, C), C)
        gb = jnp.broadcast_to(gcol_ref[0, sl, :], (C, LANES))
        gc = _dot_exact_lhs(tri, gb)
        gr = jnp.broadcast_to(grow_ref[0, g], (SUBLANES, C))
        hi, mid, lo = _split3(gr)
        gcr = (_dot(hi, triu) + _dot(mid, triu) + _dot(lo, triu))[0:1]
        diff = gc[:, :C] - gcr
        decay = jnp.where(incl, jnp.exp(jnp.where(incl, diff, 0.0)), 0.0)
        beta = jnp.broadcast_to(bcol_ref[0, sl, :], (C, LANES))
        q, k, v = q_ref[sl, :], k_ref[sl, :], v_ref[sl, :]
        kb = k * beta
        kbf = k.astype(BF16)
        xm_ref[g, 0] = jnp.where(strict, -_dot_nt(kb.astype(BF16), kbf) * decay, 0.0)
        attn_ref[g, 0] = jnp.where(incl, _dot_nt(q.astype(BF16), kbf) * decay, 0.0).astype(BF16)
        eg = jnp.exp(gc)
        gl = gc[C - 1:C, :]
        vb_ref[sl, :] = (v * beta).astype(BF16)
        kcp_ref[sl, :] = (kb * eg).astype(BF16)
        qd_ref[sl, :] = (q * eg).astype(BF16)
        kd_ref[sl, :] = (k * jnp.exp(gl - gc)).astype(BF16)
        egl_ref[g] = jnp.exp(gl)
        return carry

    lax.fori_loop(0, G, body, 0)


def _gdn_c_kernel(vb_ref, kcp_ref, qd_ref, kd_ref, t_ref, attn_ref, egl_ref, o_ref, s_ref, *, G):
    C = CHUNK

    @pl.when(pl.program_id(1) == 0)
    def _():
        s_ref[...] = jnp.zeros_like(s_ref)

    def body(g, carry):
        sl = pl.ds(pl.multiple_of(g * C, C), C)
        for h in range(GDN_V_HEADS):
            cs = slice(h * LANES, (h + 1) * LANES)
            both = jnp.concatenate([vb_ref[sl, cs], kcp_ref[sl, cs]], axis=1)
            w = _dot(t_ref[g, h].astype(BF16), both)
            s = s_ref[h]
            sb = s.astype(BF16)
            vnew = w[:, :LANES] - _dot(w[:, LANES:].astype(BF16), sb)
            vnb = vnew.astype(BF16)
            o_ref[sl, cs] = _dot(qd_ref[sl, cs], sb) + _dot(attn_ref[g, h], vnb)
            s_ref[h] = s * egl_ref[g, :, cs] + _dot_tn(kd_ref[sl, cs], vnb)
        return carry

    lax.fori_loop(0, G, body, 0)


def gdn_chunked(q, k, v, g, beta, batch, G=4):
    T, VD = v.shape
    C = CHUNK
    seq = T // batch
    nct = T // C
    G = _pick(seq // C, (G, 2, 1))
    nh = GDN_V_HEADS
    rep = GDN_V_HEADS // GDN_QK_HEADS
    gcol = g.T.reshape(nh, T, 1)
    grow = g.T.reshape(nh, nct, 1, C)
    bcol = beta.T.reshape(nh, T, 1)
    qk_spec = pl.BlockSpec((C * G, LANES), lambda i, h: (i, h // rep))
    v_spec = pl.BlockSpec((C * G, LANES), lambda i, h: (i, h))
    col_spec = pl.BlockSpec((1, C * G, 1), lambda i, h: (h, i, 0))
    outs = pl.pallas_call(
        functools.partial(_gdn_a_kernel, G=G),
        grid=(nct // G, nh),
        in_specs=[qk_spec, qk_spec, v_spec, col_spec,
                  pl.BlockSpec((1, G, 1, C), lambda i, h: (h, i, 0, 0)), col_spec],
        out_specs=[pl.BlockSpec((G, 1, C, C), lambda i, h: (i, h, 0, 0)),
                   pl.BlockSpec((G, 1, C, C), lambda i, h: (i, h, 0, 0)),
                   v_spec, v_spec, v_spec, v_spec,
                   pl.BlockSpec((G, 1, LANES), lambda i, h: (i, 0, h))],
        out_shape=[jax.ShapeDtypeStruct((nct, nh, C, C), F32),
                   jax.ShapeDtypeStruct((nct, nh, C, C), BF16),
                   jax.ShapeDtypeStruct((T, VD), BF16),
                   jax.ShapeDtypeStruct((T, VD), BF16),
                   jax.ShapeDtypeStruct((T, VD), BF16),
                   jax.ShapeDtypeStruct((T, VD), BF16),
                   jax.ShapeDtypeStruct((nct, 1, VD), F32)],
        compiler_params=_cparams(("parallel", "parallel")),
        name="gdn_a",
    )(q, k, v, gcol, grow, bcol)
    xm, attn, vb, kcp, qd, kd, egl = outs
    tinv = tri_inv(xm.reshape(nct * nh, C, C)).reshape(nct, nh, C, C)
    ncb = seq // C // G
    wide = pl.BlockSpec((C * G, VD), lambda b, n: (b * ncb + n, 0))
    mat = pl.BlockSpec((G, nh, C, C), lambda b, n: (b * ncb + n, 0, 0, 0))
    return pl.pallas_call(
        functools.partial(_gdn_c_kernel, G=G),
        grid=(batch, ncb),
        in_specs=[wide, wide, wide, wide, mat, mat,
                  pl.BlockSpec((G, 1, VD), lambda b, n: (b * ncb + n, 0, 0))],
        out_specs=wide,
        out_shape=jax.ShapeDtypeStruct((T, VD), F32),
        scratch_shapes=[pltpu.VMEM((nh, GDN_DK, GDN_DV), F32)],
        compiler_params=_cparams(("arbitrary", "arbitrary")),
        name="gdn_c",
    )(vb, kcp, qd, kd, tinv, attn, egl)


def _ret_kernel(q_ref, k_ref, v_ref, o_ref, s_ref, *, G):
    C = CHUNK

    @pl.when(pl.program_id(1) == 0)
    def _():
        s_ref[...] = jnp.zeros_like(s_ref)

    row, col = _chunk_masks()
    incl = col <= row
    dif = jnp.maximum(row - col, 0).astype(F32)
    lane = lax.broadcasted_iota(jnp.int32, (C, LANES), 1)
    head0 = lane < RET_DK
    ridx = lax.broadcasted_iota(jnp.int32, (C, LANES), 0).astype(F32)

    def body(g, carry):
        sl = pl.ds(pl.multiple_of(g * C, C), C)
        for h in range(RET_HEADS):
            lg = math.log(1.0 - 2.0 ** (-5.0 - h))
            p, hh = divmod(h, 2)
            ps = slice(p * LANES, (p + 1) * LANES)
            vs = slice(h * RET_DV, (h + 1) * RET_DV)
            msk = head0 if hh == 0 else jnp.logical_not(head0)
            q = jnp.where(msk, q_ref[sl, ps], 0.0)
            k = k_ref[sl, ps]
            vb = v_ref[sl, vs].astype(BF16)
            dmat = jnp.where(incl, jnp.exp(dif * lg), 0.0)
            sc = _dot_nt(q.astype(BF16), k.astype(BF16)) * dmat
            s = s_ref[h]
            qdec = (q * jnp.exp((ridx + 1.0) * lg)).astype(BF16)
            o_ref[sl, vs] = _dot(sc.astype(BF16), vb) + _dot(qdec, s.astype(BF16))
            kdec = (jnp.where(msk, k, 0.0) * jnp.exp((C - 1.0 - ridx) * lg)).astype(BF16)
            s_ref[h] = s * math.exp(C * lg) + _dot_tn(kdec, vb)
        return carry

    lax.fori_loop(0, G, body, 0)


def retention_chunked(q, k, v, batch, G=4):
    T, VD = v.shape
    QD = q.shape[1]
    C = CHUNK
    seq = T // batch
    G = _pick(seq // C, (G, 2, 1))
    ncb = seq // C // G
    qk = pl.BlockSpec((C * G, QD), lambda b, n: (b * ncb + n, 0))
    wide = pl.BlockSpec((C * G, VD), lambda b, n: (b * ncb + n, 0))
    return pl.pallas_call(
        functools.partial(_ret_kernel, G=G),
        grid=(batch, ncb),
        in_specs=[qk, qk, wide],
        out_specs=wide,
        out_shape=jax.ShapeDtypeStruct((T, VD), F32),
        scratch_shapes=[pltpu.VMEM((RET_HEADS, LANES, RET_DV), F32)],
        compiler_params=_cparams(("arbitrary", "arbitrary")),
        name="retention",
    )(q, k, v)


def _rms_norm(x, g, eps=RMS_EPS):
    return x * lax.rsqrt(jnp.mean(x * x, -1, keepdims=True) + eps) * g


def _rope_tables(positions, d):
    inv = ROPE_THETA ** (-jnp.arange(0, d, 2, dtype=F32) / d)
    ang = positions.astype(F32).reshape(-1, 1) * inv
    cos, sin = jnp.cos(ang), jnp.sin(ang)
    return jnp.concatenate([cos, cos], -1), jnp.concatenate([-sin, sin], -1)


def _rope(x, cos2, sin2):
    d = x.shape[-1]
    swapped = jnp.concatenate([x[..., d // 2:], x[..., : d // 2]], -1)
    return x * cos2[:, None, :] + swapped * sin2[:, None, :]


def _pad_cols(w, n):
    return jnp.pad(w, ((0, 0), (0, n - w.shape[1])))


def _mla_rwkv_mixer(xb, B, S, cos2, sin2, w_in, q_norm, w_uq, kv_norm, w_ukv, rwkv_mu, rwkv_w0, rwkv_w2,
                    rwkv_a0, rwkv_a2, rwkv_g2, rwkv_k_k, rwkv_k_a, rwkv_r_k, rwkv_gn_w, rwkv_gn_b):
    T = B * S
    n_in = w_in.shape[1]
    n_pad = -(-n_in // 512) * 512
    p = mm(xb, _pad_cols(w_in, n_pad))[:, :n_in]
    mla_cols = MLA_Q_RANK + MLA_KV_RANK + MLA_ROPE
    c_q = p[:, :MLA_Q_RANK]
    c_kv = p[:, MLA_Q_RANK:MLA_Q_RANK + MLA_KV_RANK]
    k_pe = p[:, MLA_Q_RANK + MLA_KV_RANK:mla_cols]
    rwkv_p = p[:, mla_cols:]
    dqk = MLA_NOPE + MLA_ROPE
    q = mm(_rms_norm(c_q, q_norm), w_uq).reshape(T, MLA_HEADS, dqk)
    q = jnp.concatenate([q[..., :MLA_NOPE], _rope(q[..., MLA_NOPE:], cos2, sin2)], -1) * dqk ** -0.5
    kv = mm(_rms_norm(c_kv, kv_norm), w_ukv).reshape(T, MLA_HEADS, MLA_NOPE + MLA_V)
    k_pe = _rope(k_pe[:, None, :], cos2, sin2)
    k = jnp.concatenate([kv[..., :MLA_NOPE], jnp.broadcast_to(k_pe, (T, MLA_HEADS, MLA_ROPE))], -1)
    v = kv[..., MLA_NOPE:]
    to_bhsd = lambda t: t.reshape(B, S, MLA_HEADS, -1).transpose(0, 2, 1, 3).astype(BF16)
    o_mla = flash_attention(to_bhsd(q), to_bhsd(k), to_bhsd(v))
    rp = rwkv_p.reshape(B, S, -1)
    prev = jnp.pad(rp, ((0, 0), (1, 0), (0, 0)))[:, :-1]
    rp = (rp + rwkv_mu * (prev - rp)).reshape(T, -1)
    o1, o2, o3 = RWKV_DIM, 2 * RWKV_DIM, 3 * RWKV_DIM
    r, k, v = rp[:, :o1], rp[:, o1:o2], rp[:, o2:o3]
    xw = rp[:, o3:o3 + DECAY_LORA]
    xa = rp[:, o3 + DECAY_LORA:o3 + DECAY_LORA + AAA_LORA]
    xg = rp[:, o3 + DECAY_LORA + AAA_LORA:]
    w_log = -jax.nn.softplus(-(rwkv_w0 + mm(jnp.tanh(xw), rwkv_w2))) - 0.5
    lw = -jnp.exp(w_log)
    a = jax.nn.sigmoid(rwkv_a0 + mm(xa, rwkv_a2))
    g = mm(jax.nn.sigmoid(xg), rwkv_g2)
    hs = lambda t: t.reshape(T, RWKV_HEADS, RWKV_HEAD)
    kk = hs(k * rwkv_k_k)
    kk = (kk * lax.rsqrt(jnp.sum(kk * kk, -1, keepdims=True) + 1e-6)).reshape(T, RWKV_DIM)
    k = k * (1.0 + (a - 1.0) * rwkv_k_a)
    y = rwkv7_chunked(r, lw, k, v, -kk, kk * a, B)
    yh = hs(y)
    mu = yh.mean(-1, keepdims=True)
    var = jnp.square(yh - mu).mean(-1, keepdims=True)
    yn = ((yh - mu) * lax.rsqrt(var + RWKV_GN_EPS)).reshape(T, RWKV_DIM) * rwkv_gn_w + rwkv_gn_b
    bonus = jnp.sum(hs(r) * hs(k) * rwkv_r_k, axis=-1, keepdims=True) * hs(v)
    o_rwkv = (yn + bonus.reshape(T, RWKV_DIM)) * g
    return jnp.concatenate([o_mla, o_rwkv.astype(BF16)], -1)


def _causal_dwconv(x, w):
    K = w.shape[0]
    S = x.shape[1]
    xp = jnp.pad(x, ((0, 0), (K - 1, 0), (0, 0)))
    return sum(w[j] * xp[:, j:j + S] for j in range(K))


def _gdn_retention_mixer(xb, B, S, cos2, sin2, w_in, gdn_conv_w, gdn_A_log, gdn_dt_bias, gdn_norm,
                         ret_gn_w, ret_gn_b):
    T = B * S
    n_in = w_in.shape[1]
    n_pad = -(-n_in // 512) * 512
    p = mm(xb, _pad_cols(w_in, n_pad))[:, :n_in]
    qk_w, v_w = GDN_QK_HEADS * GDN_DK, GDN_V_HEADS * GDN_DV
    gdn_cols = 2 * qk_w + 2 * v_w + 2 * GDN_V_HEADS
    gdn_p, ret_p = p[:, :gdn_cols], p[:, gdn_cols:]
    qkv = gdn_p[:, :2 * qk_w + v_w]
    z = gdn_p[:, 2 * qk_w + v_w:2 * qk_w + 2 * v_w]
    b_logit = gdn_p[:, 2 * qk_w + 2 * v_w:2 * qk_w + 2 * v_w + GDN_V_HEADS]
    a_logit = gdn_p[:, 2 * qk_w + 2 * v_w + GDN_V_HEADS:]
    qkv = jax.nn.silu(_causal_dwconv(qkv.reshape(B, S, -1), gdn_conv_w)).reshape(T, -1)
    l2n = lambda t: (lambda u: u * lax.rsqrt(jnp.sum(u * u, -1, keepdims=True) + 1e-6))(
        t.reshape(T, GDN_QK_HEADS, GDN_DK)).reshape(T, qk_w)
    q = l2n(qkv[:, :qk_w]) * GDN_DK ** -0.5
    k = l2n(qkv[:, qk_w:2 * qk_w])
    v = qkv[:, 2 * qk_w:]
    beta = jax.nn.sigmoid(b_logit)
    g = -jnp.exp(gdn_A_log) * jax.nn.softplus(a_logit + gdn_dt_bias)
    o = gdn_chunked(q, k, v, g, beta, B).reshape(T, GDN_V_HEADS, GDN_DV)
    o = _rms_norm(o, gdn_norm) * jax.nn.silu(z.reshape(T, GDN_V_HEADS, GDN_DV))
    o_gdn = o.reshape(T, v_w)
    qw = RET_HEADS * RET_DK
    vw = RET_HEADS * RET_DV
    rq = _rope(ret_p[:, :qw].reshape(T, RET_HEADS, RET_DK), cos2, sin2).reshape(T, qw) * RET_DK ** -0.5
    rk = _rope(ret_p[:, qw:2 * qw].reshape(T, RET_HEADS, RET_DK), cos2, sin2).reshape(T, qw)
    rv = ret_p[:, 2 * qw:2 * qw + vw]
    rg = ret_p[:, 2 * qw + vw:]
    o_ret = retention_chunked(rq, rk, rv, B).reshape(T, RET_HEADS, RET_DV)
    mu = o_ret.mean(-1, keepdims=True)
    var = jnp.square(o_ret - mu).mean(-1, keepdims=True)
    o_ret = ((o_ret - mu) * lax.rsqrt(var + LN_EPS)).reshape(T, vw) * ret_gn_w + ret_gn_b
    o_ret = o_ret * jax.nn.silu(rg)
    return jnp.concatenate([o_gdn, o_ret], -1).astype(BF16)


def _conv_ffn_ln(x, S, w_gate, w_val, conv_w, conv_b, w_down, ln_g, ln_b):
    h = ffn1(x, w_gate, w_val, conv_w, conv_b, S)
    return mm_ln(h, w_down, x, ln_g, ln_b)


def kernel(x, positions, l0_w_in, l0_q_norm, l0_w_uq, l0_kv_norm, l0_w_ukv, l0_rwkv_mu, l0_rwkv_w0, l0_rwkv_w2, l0_rwkv_a0, l0_rwkv_a2, l0_rwkv_g2, l0_rwkv_k_k, l0_rwkv_k_a, l0_rwkv_r_k, l0_rwkv_gn_w, l0_rwkv_gn_b, l0_w_out, l0_ln1_g, l0_ln1_b, l0_ffn_w_gate, l0_ffn_w_val, l0_ffn_conv_w, l0_ffn_conv_b, l0_ffn_w_down, l0_ln2_g, l0_ln2_b, l1_w_in, l1_gdn_conv_w, l1_gdn_A_log, l1_gdn_dt_bias, l1_gdn_norm, l1_ret_gn_w, l1_ret_gn_b, l1_w_out, l1_ln1_g, l1_ln1_b, l1_ffn_w_gate, l1_ffn_w_val, l1_ffn_conv_w, l1_ffn_conv_b, l1_ffn_w_down, l1_ln2_g, l1_ln2_b):
    B, S, D = x.shape
    T = B * S
    xb = x.reshape(T, D)
    cos2, sin2 = _rope_tables(positions, MLA_ROPE)
    m0 = _mla_rwkv_mixer(xb, B, S, cos2, sin2, l0_w_in, l0_q_norm, l0_w_uq, l0_kv_norm, l0_w_ukv, l0_rwkv_mu,
                         l0_rwkv_w0, l0_rwkv_w2, l0_rwkv_a0, l0_rwkv_a2, l0_rwkv_g2, l0_rwkv_k_k, l0_rwkv_k_a,
                         l0_rwkv_r_k, l0_rwkv_gn_w, l0_rwkv_gn_b)
    xb = mm_ln(m0, l0_w_out, xb, l0_ln1_g, l0_ln1_b)
    xb = _conv_ffn_ln(xb, S, l0_ffn_w_gate, l0_ffn_w_val, l0_ffn_conv_w, l0_ffn_conv_b, l0_ffn_w_down,
                      l0_ln2_g, l0_ln2_b)
    m1 = _gdn_retention_mixer(xb, B, S, cos2, sin2, l1_w_in, l1_gdn_conv_w, l1_gdn_A_log, l1_gdn_dt_bias,
                              l1_gdn_norm, l1_ret_gn_w, l1_ret_gn_b)
    xb = mm_ln(m1, l1_w_out, xb, l1_ln1_g, l1_ln1_b)
    xb = _conv_ffn_ln(xb, S, l1_ffn_w_gate, l1_ffn_w_val, l1_ffn_conv_w, l1_ffn_conv_b, l1_ffn_w_down,
                      l1_ln2_g, l1_ln2_b)
    return xb.reshape(B, S, D)
```

```python
import functools
import math

import jax
import jax.numpy as jnp
from jax import lax
from jax.experimental import pallas as pl
from jax.experimental.pallas import tpu as pltpu

F32 = jnp.float32
BF16 = jnp.bfloat16

DEPTH = 2
ALPHA = (2 * DEPTH) ** 0.25
LN_EPS = 1e-5
RMS_EPS = 1e-6
ROPE_THETA = 10000.0

MLA_HEADS = 8
MLA_Q_RANK = 448
MLA_KV_RANK = 128
MLA_NOPE = 128
MLA_ROPE = 64
MLA_V = 128
RWKV_HEADS = 16
RWKV_HEAD = 64
RWKV_DIM = RWKV_HEADS * RWKV_HEAD
DECAY_LORA = 64
AAA_LORA = 64
GATE_LORA = 160
RWKV_GN_EPS = 64e-5
GDN_QK_HEADS = 4
GDN_V_HEADS = 8
GDN_DK = 128
GDN_DV = 128
GDN_CONV = 4
RET_HEADS = 8
RET_DK = 64
RET_DV = 128
FFN_CONV = 3

CHUNK = 64
LANES = 128
SUBLANES = 8
VMEM_LIMIT = 56 * 1024 * 1024


def _cparams(sem):
    return pltpu.CompilerParams(dimension_semantics=sem, vmem_limit_bytes=VMEM_LIMIT)


def _dot(a, b):
    return jnp.dot(a, b, preferred_element_type=F32)


def _dot_nt(a, b):
    return lax.dot_general(a, b, (((1,), (1,)), ((), ())), preferred_element_type=F32)


def _dot_tn(a, b):
    return lax.dot_general(a, b, (((0,), (0,)), ((), ())), preferred_element_type=F32)


def _split3(x):
    hi = x.astype(BF16)
    r1 = x - hi.astype(F32)
    mid = r1.astype(BF16)
    lo = (r1 - mid.astype(F32)).astype(BF16)
    return hi, mid, lo


def _dot_exact_lhs(m_bf16, x):
    hi, mid, lo = _split3(x)
    return _dot(m_bf16, hi) + _dot(m_bf16, mid) + _dot(m_bf16, lo)


def _mm_kernel(a_ref, w_ref, o_ref, *scratch, nk):
    if nk == 1:
        o_ref[...] = _dot(a_ref[...].astype(BF16), w_ref[...]).astype(o_ref.dtype)
        return
    acc_ref, = scratch
    k = pl.program_id(2)

    @pl.when(k == 0)
    def _():
        acc_ref[...] = jnp.zeros_like(acc_ref)

    acc_ref[...] += _dot(a_ref[...].astype(BF16), w_ref[...])

    @pl.when(k == nk - 1)
    def _():
        o_ref[...] = acc_ref[...].astype(o_ref.dtype)


def _pick(n, pref):
    for t in pref:
        if n % t == 0:
            return t
    return n


def mm(a, w, out_dtype=F32, tm=512, tn=None, tk=None):
    M, K = a.shape
    N = w.shape[1]
    tm = _pick(M, (tm, 256, 128, 64, 32, 16, 8))
    tn = tn or _pick(N, (512, 384, 256, 128))
    tk = tk or K
    nk = K // tk
    w = w.astype(BF16)
    scratch = [] if nk == 1 else [pltpu.VMEM((tm, tn), F32)]
    return pl.pallas_call(
        functools.partial(_mm_kernel, nk=nk),
        grid=(M // tm, N // tn, nk),
        in_specs=[pl.BlockSpec((tm, tk), lambda i, j, k: (i, k)),
                  pl.BlockSpec((tk, tn), lambda i, j, k: (k, j))],
        out_specs=pl.BlockSpec((tm, tn), lambda i, j, k: (i, j)),
        out_shape=jax.ShapeDtypeStruct((M, N), out_dtype),
        scratch_shapes=scratch,
        compiler_params=_cparams(("parallel", "parallel", "arbitrary")),
        name="mm",
    )(a, w)


def _mm_ln_kernel(a_ref, w_ref, x_ref, g_ref, b_ref, o_ref, acc_ref, *, nk):
    k = pl.program_id(1)

    @pl.when(k == 0)
    def _():
        acc_ref[...] = jnp.zeros_like(acc_ref)

    acc_ref[...] += _dot(a_ref[...].astype(BF16), w_ref[...])

    @pl.when(k == nk - 1)
    def _():
        y = ALPHA * x_ref[...] + acc_ref[...]
        mu = jnp.mean(y, axis=-1, keepdims=True)
        yc = y - mu
        var = jnp.mean(yc * yc, axis=-1, keepdims=True)
        o_ref[...] = yc * lax.rsqrt(var + LN_EPS) * g_ref[...] + b_ref[...]


def mm_ln(a, w, x, g, b, tm=512, tk=None):
    M, K = a.shape
    N = w.shape[1]
    tm = _pick(M, (tm, 256, 128, 64, 32, 16, 8))
    tk = tk or _pick(K, (1024, 1408, 512, 256, 128))
    nk = K // tk
    return pl.pallas_call(
        functools.partial(_mm_ln_kernel, nk=nk),
        grid=(M // tm, nk),
        in_specs=[pl.BlockSpec((tm, tk), lambda i, k: (i, k)),
                  pl.BlockSpec((tk, N), lambda i, k: (k, 0)),
                  pl.BlockSpec((tm, N), lambda i, k: (i, 0)),
                  pl.BlockSpec((1, N), lambda i, k: (0, 0)),
                  pl.BlockSpec((1, N), lambda i, k: (0, 0))],
        out_specs=pl.BlockSpec((tm, N), lambda i, k: (i, 0)),
        out_shape=jax.ShapeDtypeStruct((M, N), F32),
        scratch_shapes=[pltpu.VMEM((tm, N), F32)],
        compiler_params=_cparams(("parallel", "arbitrary")),
        name="mm_ln",
    )(a, w.astype(BF16), x, g.reshape(1, N), b.reshape(1, N))


def _ffn1_kernel(x_ref, xh_ref, wg_ref, wv_ref, cw_ref, cb_ref, o_ref, *, tiles_per_seq):
    m = pl.program_id(0)
    x = x_ref[...].astype(BF16)
    g = _dot(x, wg_ref[...])
    gh = _dot(xh_ref[...].astype(BF16), wg_ref[...])
    gh = jnp.where(m % tiles_per_seq == 0, 0.0, gh)
    rows = lax.broadcasted_iota(jnp.int32, g.shape, 0)
    g1 = jnp.where(rows == 0, gh[SUBLANES - 1:SUBLANES], pltpu.roll(g, 1, 0))
    g2 = jnp.where(rows == 0, gh[SUBLANES - 2:SUBLANES - 1],
                   jnp.where(rows == 1, gh[SUBLANES - 1:SUBLANES], pltpu.roll(g, 2, 0)))
    cw = cw_ref[...]
    h = g * cw[2:3] + g1 * cw[1:2] + g2 * cw[0:1] + cb_ref[...]
    v = _dot(x, wv_ref[...])
    o_ref[...] = (h * jax.nn.sigmoid(h) * v).astype(o_ref.dtype)


def ffn1(x, w_gate, w_val, conv_w, conv_b, seq, tm=512, tf=512):
    M, D = x.shape
    Fd = w_gate.shape[1]
    tm = _pick(seq, (tm, 256, 128, 64, 32, 16, 8))
    tf = _pick(Fd, (tf, 256, 128))
    hb = tm // SUBLANES
    return pl.pallas_call(
        functools.partial(_ffn1_kernel, tiles_per_seq=seq // tm),
        grid=(M // tm, Fd // tf),
        in_specs=[pl.BlockSpec((tm, D), lambda i, j: (i, 0)),
                  pl.BlockSpec((SUBLANES, D), lambda i, j: (jnp.maximum(i * hb - 1, 0), 0)),
                  pl.BlockSpec((D, tf), lambda i, j: (0, j)),
                  pl.BlockSpec((D, tf), lambda i, j: (0, j)),
                  pl.BlockSpec((FFN_CONV, tf), lambda i, j: (0, j)),
                  pl.BlockSpec((1, tf), lambda i, j: (0, j))],
        out_specs=pl.BlockSpec((tm, tf), lambda i, j: (i, j)),
        out_shape=jax.ShapeDtypeStruct((M, Fd), BF16),
        compiler_params=_cparams(("parallel", "parallel")),
        name="ffn1",
    )(x, x, w_gate.astype(BF16), w_val.astype(BF16), conv_w, conv_b.reshape(1, Fd))


def _flash_kernel(q_ref, k_ref, v_ref, o_ref, m_ref, acc_ref, s_ref, *, tq, tk, ts, dv):
    qi = pl.program_id(2)
    nsub = tq // ts
    ndiag = tq // tk
    nfull = qi * ndiag
    m_ref[...] = jnp.full_like(m_ref, -jnp.inf)
    acc_ref[...] = jnp.zeros_like(acc_ref)

    def above_diagonal(r, d):
        return (r + 1) * ts <= d * tk

    def scores(r, blk):
        k = k_ref[0, 0, pl.ds(pl.multiple_of(blk * tk, tk), tk), :]
        return _dot_nt(q_ref[0, 0, r * ts:(r + 1) * ts, :], k)

    def consume(blk, diag, nxt_diag):
        v = v_ref[0, 0, pl.ds(pl.multiple_of(blk * tk, tk), tk), :]
        for r in range(nsub):
            rs = slice(r * ts, (r + 1) * ts)
            live = diag is None or not above_diagonal(r, diag)
            if live:
                s = s_ref[rs, :]
            if nxt_diag is None or (nxt_diag >= 0 and not above_diagonal(r, nxt_diag)):
                s_ref[rs, :] = scores(r, blk + 1)
            if not live:
                continue
            if diag is not None and r * ts < (diag + 1) * tk - 1:
                rowi = lax.broadcasted_iota(jnp.int32, s.shape, 0) + r * ts
                coli = lax.broadcasted_iota(jnp.int32, s.shape, 1) + diag * tk
                s = jnp.where(coli <= rowi, s, -jnp.inf)
            m_old = m_ref[rs, :]
            m_new = jnp.maximum(m_old, jnp.max(s, axis=-1, keepdims=True))
            p = jnp.exp2(s - m_new).astype(BF16)
            acc_ref[rs, :] = jnp.exp2(m_old - m_new) * acc_ref[rs, :] + _dot(p, v)
            m_ref[rs, :] = m_new

    for r in range(nsub):
        s_ref[r * ts:(r + 1) * ts, :] = scores(r, 0)

    def body(j, carry):
        consume(j, None, None)
        return carry

    lax.fori_loop(0, nfull, body, 0)
    for d in range(ndiag):
        consume(nfull + d, d, d + 1 if d + 1 < ndiag else -1)
    acc = acc_ref[...]
    o_ref[...] = (acc[:, :dv] / acc[:, dv:]).astype(o_ref.dtype)


def flash_attention(q, k, v, tq=1024, tk=1024, ts=512):
    B, H, S, dq = q.shape
    dv = v.shape[-1] // 2
    tq = _pick(S, (tq, 512, 256, 128))
    tk = _pick(tq, (tk, 256, 128))
    nq = S // tq
    return pl.pallas_call(
        functools.partial(_flash_kernel, tq=tq, tk=tk, ts=_pick(tq, (ts, 128)), dv=dv),
        grid=(B, H, nq),
        in_specs=[pl.BlockSpec((1, 1, tq, dq), lambda b, h, i: (b, h, i, 0)),
                  pl.BlockSpec((1, 1, S, dq), lambda b, h, i: (b, h, 0, 0)),
                  pl.BlockSpec((1, 1, S, 2 * dv), lambda b, h, i: (b, h, 0, 0))],
        out_specs=pl.BlockSpec((tq, dv), lambda b, h, i: (b * nq + i, h)),
        out_shape=jax.ShapeDtypeStruct((B * S, H * dv), BF16),
        scratch_shapes=[pltpu.VMEM((tq, 1), F32), pltpu.VMEM((tq, 2 * dv), F32), pltpu.VMEM((tq, tk), F32)],
        compiler_params=_cparams(("parallel", "parallel", "arbitrary")),
        name="flash",
    )(q, k, v)


def _tri_inv_kernel(a_ref, t_ref):
    C = a_ref.shape[1]
    kk = lax.broadcasted_iota(jnp.int32, (C, LANES), 0)
    for i in range(C):
        acc = jnp.where(kk == i, 1.0, 0.0).astype(F32)
        if i > 0:
            def body(j, acc, i=i):
                return acc + a_ref[0, i, pl.ds(j, 1), :] * t_ref[0, j]
            acc = lax.fori_loop(0, i, body, acc)
        t_ref[0, i] = acc


def tri_inv(a):
    n_in, C, _ = a.shape
    ng = -(-n_in // LANES)
    N = ng * LANES
    a = jnp.pad(a, ((0, N - n_in), (0, 0), (0, 0)))
    al = a.reshape(ng, LANES, C * C).transpose(0, 2, 1).reshape(ng, C, C, LANES)
    tl = pl.pallas_call(
        _tri_inv_kernel,
        grid=(ng,),
        in_specs=[pl.BlockSpec((1, C, C, LANES), lambda i: (i, 0, 0, 0))],
        out_specs=pl.BlockSpec((1, C, C, LANES), lambda i: (i, 0, 0, 0)),
        out_shape=jax.ShapeDtypeStruct((ng, C, C, LANES), F32),
        compiler_params=_cparams(("parallel",)),
        name="tri_inv",
    )(al)
    return tl.reshape(ng, C * C, LANES).transpose(0, 2, 1).reshape(N, C, C)[:n_in]


def _chunk_masks():
    C = CHUNK
    row = lax.broadcasted_iota(jnp.int32, (C, C), 0)
    col = lax.broadcasted_iota(jnp.int32, (C, C), 1)
    return row, col


def _rwkv_a_kernel(r_ref, lw_ref, k_ref, v_ref, av_ref, bv_ref,
                   aab_ref, mrb_ref, at_ref, akv_ref, rt_ref, bh_ref, yk_ref, kv_ref, egl_ref, *, G):
    C = CHUNK
    row, col = _chunk_masks()
    strict = col < row
    incl = col <= row
    tri = jnp.where(incl, 1.0, 0.0).astype(BF16)
    lane = lax.broadcasted_iota(jnp.int32, (C, LANES), 1)
    head0 = lane < RWKV_HEAD
    bd = (lax.broadcasted_iota(jnp.int32, (LANES, LANES), 0) // RWKV_HEAD ==
          lax.broadcasted_iota(jnp.int32, (LANES, LANES), 1) // RWKV_HEAD)

    def body(g, carry):
        sl = pl.ds(pl.multiple_of(g * C, C), C)
        lw = lw_ref[sl, :]
        gc = _dot_exact_lhs(tri, lw)
        gl = gc[C - 1:C, :]
        eg = jnp.exp(gc)
        ieg = jnp.exp(-gc)
        egl = jnp.exp(gl - gc)
        av, bv, k, r = av_ref[sl, :], bv_ref[sl, :], k_ref[sl, :], r_ref[sl, :]
        vb = v_ref[sl, :].astype(BF16)
        at = av * jnp.exp(gc - lw)
        rt = r * eg
        bt = (bv * ieg).astype(BF16)
        kt = (k * ieg).astype(BF16)
        akv = jnp.zeros((C, LANES), F32)
        yk = jnp.zeros((C, LANES), F32)
        for hh in range(2):
            msk = head0 if hh == 0 else jnp.logical_not(head0)
            lhs = jnp.concatenate([jnp.where(msk, at, 0.0), jnp.where(msk, rt, 0.0)], axis=0).astype(BF16)
            xb = _dot_nt(lhs, bt)
            xk = _dot_nt(lhs, kt)
            aab_ref[g, hh] = jnp.where(strict, xb[:C], 0.0)
            mrb_ref[g, hh] = jnp.where(incl, xb[C:], 0.0).astype(BF16)
            aak = jnp.where(strict, xk[:C], 0.0).astype(BF16)
            mrk = jnp.where(incl, xk[C:], 0.0).astype(BF16)
            akv = jnp.where(msk, _dot(aak, vb), akv)
            yk = jnp.where(msk, _dot(mrk, vb), yk)
        at_ref[sl, :] = at.astype(BF16)
        akv_ref[sl, :] = akv.astype(BF16)
        rt_ref[sl, :] = rt.astype(BF16)
        bh_ref[sl, :] = (bv * egl).astype(BF16)
        yk_ref[sl, :] = yk
        kv_ref[g, 0] = jnp.where(bd, _dot_tn(vb, (k * egl).astype(BF16)), 0.0)
        egl_ref[g] = jnp.exp(gl)
        return carry

    lax.fori_loop(0, G, body, 0)


def _rwkv_c_kernel(at_ref, akv_ref, rt_ref, bh_ref, yk_ref, t_ref, mrb_ref, kv_ref, egl_ref,
                   y_ref, s_ref, *, G):
    C = CHUNK
    npair = RWKV_HEADS // 2

    @pl.when(pl.program_id(1) == 0)
    def _():
        s_ref[...] = jnp.zeros_like(s_ref)

    lane = lax.broadcasted_iota(jnp.int32, (C, LANES), 1)
    head0 = lane < RWKV_HEAD
    head0w = jnp.concatenate([head0, head0], axis=1)
    bd = (lax.broadcasted_iota(jnp.int32, (LANES, LANES), 0) // RWKV_HEAD ==
          lax.broadcasted_iota(jnp.int32, (LANES, LANES), 1) // RWKV_HEAD)

    def body(g, carry):
        sl = pl.ds(pl.multiple_of(g * C, C), C)
        for p in range(npair):
            cs = slice(p * LANES, (p + 1) * LANES)
            both = jnp.concatenate([at_ref[sl, cs], akv_ref[sl, cs]], axis=1)
            w0 = _dot(t_ref[g, 2 * p].astype(BF16), both)
            w1 = _dot(t_ref[g, 2 * p + 1].astype(BF16), both)
            w = jnp.where(head0w, w0, w1)
            s = s_ref[p]
            sb = s.astype(BF16)
            u = _dot_nt(w[:, :LANES].astype(BF16), sb) + w[:, LANES:]
            ub = u.astype(BF16)
            y = _dot_nt(rt_ref[sl, cs], sb) + yk_ref[sl, cs]
            y = y + jnp.where(head0, _dot(mrb_ref[g, 2 * p], ub), _dot(mrb_ref[g, 2 * p + 1], ub))
            y_ref[sl, cs] = y
            s_ref[p] = s * egl_ref[g, :, cs] + jnp.where(bd, _dot_tn(ub, bh_ref[sl, cs]), 0.0) + kv_ref[g, p]
        return carry

    lax.fori_loop(0, G, body, 0)


def rwkv7_chunked(p, r_blk, v_blk, lw, k, av, bv, batch, G=4):
    T, HD = lw.shape
    C = CHUNK
    seq = T // batch
    nct = T // C
    G = _pick(seq // C, (G, 2, 1))
    npair = HD // LANES
    nh = 2 * npair
    row_spec = pl.BlockSpec((C * G, LANES), lambda i, p: (i, p))
    outs = pl.pallas_call(
        functools.partial(_rwkv_a_kernel, G=G),
        grid=(nct // G, npair),
        in_specs=[pl.BlockSpec((C * G, LANES), lambda i, p: (i, r_blk + p)), row_spec, row_spec,
                  pl.BlockSpec((C * G, LANES), lambda i, p: (i, v_blk + p)), row_spec, row_spec],
        out_specs=[pl.BlockSpec((G, 2, C, C), lambda i, p: (i, p, 0, 0)),
                   pl.BlockSpec((G, 2, C, C), lambda i, p: (i, p, 0, 0)),
                   row_spec, row_spec, row_spec, row_spec, row_spec,
                   pl.BlockSpec((G, 1, LANES, LANES), lambda i, p: (i, p, 0, 0)),
                   pl.BlockSpec((G, 1, LANES), lambda i, p: (i, 0, p))],
        out_shape=[jax.ShapeDtypeStruct((nct, nh, C, C), F32),
                   jax.ShapeDtypeStruct((nct, nh, C, C), BF16),
                   jax.ShapeDtypeStruct((T, HD), BF16),
                   jax.ShapeDtypeStruct((T, HD), BF16),
                   jax.ShapeDtypeStruct((T, HD), BF16),
                   jax.ShapeDtypeStruct((T, HD), BF16),
                   jax.ShapeDtypeStruct((T, HD), F32),
                   jax.ShapeDtypeStruct((nct, npair, LANES, LANES), F32),
                   jax.ShapeDtypeStruct((nct, 1, HD), F32)],
        compiler_params=_cparams(("parallel", "parallel")),
        name="rwkv_a",
    )(p, lw, k, p, av, bv)
    aab, mrb, at, akv, rt, bh, yk, kv, egl = outs
    tinv = tri_inv(aab.reshape(nct * nh, C, C)).reshape(nct, nh, C, C)
    ncb = seq // C // G
    wide = pl.BlockSpec((C * G, HD), lambda b, n: (b * ncb + n, 0))
    return pl.pallas_call(
        functools.partial(_rwkv_c_kernel, G=G),
        grid=(batch, ncb),
        in_specs=[wide, wide, wide, wide, wide,
                  pl.BlockSpec((G, nh, C, C), lambda b, n: (b * ncb + n, 0, 0, 0)),
                  pl.BlockSpec((G, nh, C, C), lambda b, n: (b * ncb + n, 0, 0, 0)),
                  pl.BlockSpec((G, npair, LANES, LANES), lambda b, n: (b * ncb + n, 0, 0, 0)),
                  pl.BlockSpec((G, 1, HD), lambda b, n: (b * ncb + n, 0, 0))],
        out_specs=wide,
        out_shape=jax.ShapeDtypeStruct((T, HD), F32),
        scratch_shapes=[pltpu.VMEM((npair, LANES, LANES), F32)],
        compiler_params=_cparams(("arbitrary", "arbitrary")),
        name="rwkv_c",
    )(at, akv, rt, bh, yk, tinv, mrb, kv, egl)


def _gdn_a_kernel(q_ref, k_ref, v_ref, gcol_ref, grow_ref, bcol_ref,
                  xm_ref, attn_ref, vb_ref, kcp_ref, qd_ref, kd_ref, egl_ref, *, G):
    C = CHUNK
    row, col = _chunk_masks()
    strict = col < row
    incl = col <= row
    tri = jnp.where(incl, 1.0, 0.0).astype(BF16)
    triu = jnp.where(row <= col, 1.0, 0.0).astype(BF16)

    def body(g, carry):
        sl = pl.ds(pl.multiple_of(g * C, C), C)
        gb = jnp.broadcast_to(gcol_ref[0, sl, :], (C, LANES))
        gc = _dot_exact_lhs(tri, gb)
        gr = jnp.broadcast_to(grow_ref[0, g], (SUBLANES, C))
        hi, mid, lo = _split3(gr)
        gcr = (_dot(hi, triu) + _dot(mid, triu) + _dot(lo, triu))[0:1]
        diff = gc[:, :C] - gcr
        decay = jnp.where(incl, jnp.exp(jnp.where(incl, diff, 0.0)), 0.0)
        beta = jnp.broadcast_to(bcol_ref[0, sl, :], (C, LANES))
        q, k, v = q_ref[sl, :], k_ref[sl, :], v_ref[sl, :]
        kb = k * beta
        kbf = k.astype(BF16)
        xm_ref[g, 0] = jnp.where(strict, -_dot_nt(kb.astype(BF16), kbf) * decay, 0.0)
        attn_ref[g, 0] = jnp.where(incl, _dot_nt(q.astype(BF16), kbf) * decay, 0.0).astype(BF16)
        eg = jnp.exp(gc)
        gl = gc[C - 1:C, :]
        vb_ref[sl, :] = (v * beta).astype(BF16)
        kcp_ref[sl, :] = (kb * eg).astype(BF16)
        qd_ref[sl, :] = (q * eg).astype(BF16)
        kd_ref[sl, :] = (k * jnp.exp(gl - gc)).astype(BF16)
        egl_ref[g] = jnp.exp(gl)
        return carry

    lax.fori_loop(0, G, body, 0)


def _gdn_c_kernel(vb_ref, kcp_ref, qd_ref, kd_ref, t_ref, attn_ref, egl_ref, o_ref, s_ref, *, G):
    C = CHUNK

    @pl.when(pl.program_id(1) == 0)
    def _():
        s_ref[...] = jnp.zeros_like(s_ref)

    def body(g, carry):
        sl = pl.ds(pl.multiple_of(g * C, C), C)
        for h in range(GDN_V_HEADS):
            cs = slice(h * LANES, (h + 1) * LANES)
            both = jnp.concatenate([vb_ref[sl, cs], kcp_ref[sl, cs]], axis=1)
            w = _dot(t_ref[g, h].astype(BF16), both)
            s = s_ref[h]
            sb = s.astype(BF16)
            vnew = w[:, :LANES] - _dot(w[:, LANES:].astype(BF16), sb)
            vnb = vnew.astype(BF16)
            o_ref[sl, cs] = _dot(qd_ref[sl, cs], sb) + _dot(attn_ref[g, h], vnb)
            s_ref[h] = s * egl_ref[g, :, cs] + _dot_tn(kd_ref[sl, cs], vnb)
        return carry

    lax.fori_loop(0, G, body, 0)


def gdn_chunked(q, k, v, g, beta, batch, G=4):
    T, VD = v.shape
    C = CHUNK
    seq = T // batch
    nct = T // C
    G = _pick(seq // C, (G, 2, 1))
    nh = GDN_V_HEADS
    rep = GDN_V_HEADS // GDN_QK_HEADS
    gcol = g.T.reshape(nh, T, 1)
    grow = g.T.reshape(nh, nct, 1, C)
    bcol = beta.T.reshape(nh, T, 1)
    qk_spec = pl.BlockSpec((C * G, LANES), lambda i, h: (i, h // rep))
    v_spec = pl.BlockSpec((C * G, LANES), lambda i, h: (i, h))
    col_spec = pl.BlockSpec((1, C * G, 1), lambda i, h: (h, i, 0))
    outs = pl.pallas_call(
        functools.partial(_gdn_a_kernel, G=G),
        grid=(nct // G, nh),
        in_specs=[qk_spec, qk_spec, v_spec, col_spec,
                  pl.BlockSpec((1, G, 1, C), lambda i, h: (h, i, 0, 0)), col_spec],
        out_specs=[pl.BlockSpec((G, 1, C, C), lambda i, h: (i, h, 0, 0)),
                   pl.BlockSpec((G, 1, C, C), lambda i, h: (i, h, 0, 0)),
                   v_spec, v_spec, v_spec, v_spec,
                   pl.BlockSpec((G, 1, LANES), lambda i, h: (i, 0, h))],
        out_shape=[jax.ShapeDtypeStruct((nct, nh, C, C), F32),
                   jax.ShapeDtypeStruct((nct, nh, C, C), BF16),
                   jax.ShapeDtypeStruct((T, VD), BF16),
                   jax.ShapeDtypeStruct((T, VD), BF16),
                   jax.ShapeDtypeStruct((T, VD), BF16),
                   jax.ShapeDtypeStruct((T, VD), BF16),
                   jax.ShapeDtypeStruct((nct, 1, VD), F32)],
        compiler_params=_cparams(("parallel", "parallel")),
        name="gdn_a",
    )(q, k, v, gcol, grow, bcol)
    xm, attn, vb, kcp, qd, kd, egl = outs
    tinv = tri_inv(xm.reshape(nct * nh, C, C)).reshape(nct, nh, C, C)
    ncb = seq // C // G
    wide = pl.BlockSpec((C * G, VD), lambda b, n: (b * ncb + n, 0))
    mat = pl.BlockSpec((G, nh, C, C), lambda b, n: (b * ncb + n, 0, 0, 0))
    return pl.pallas_call(
        functools.partial(_gdn_c_kernel, G=G),
        grid=(batch, ncb),
        in_specs=[wide, wide, wide, wide, mat, mat,
                  pl.BlockSpec((G, 1, VD), lambda b, n: (b * ncb + n, 0, 0))],
        out_specs=wide,
        out_shape=jax.ShapeDtypeStruct((T, VD), F32),
        scratch_shapes=[pltpu.VMEM((nh, GDN_DK, GDN_DV), F32)],
        compiler_params=_cparams(("arbitrary", "arbitrary")),
        name="gdn_c",
    )(vb, kcp, qd, kd, tinv, attn, egl)


def _rope_pair(x, ct, st, low):
    swapped = jnp.where(low, pltpu.roll(x, LANES - RET_DK // 2, 1), pltpu.roll(x, RET_DK // 2, 1))
    return x * ct + swapped * st


def _ret_kernel(q_ref, k_ref, v_ref, ct_ref, st_ref, o_ref, s_ref, *, G):
    C = CHUNK

    @pl.when(pl.program_id(1) == 0)
    def _():
        s_ref[...] = jnp.zeros_like(s_ref)

    row, col = _chunk_masks()
    incl = col <= row
    dif = jnp.maximum(row - col, 0).astype(F32)
    lane = lax.broadcasted_iota(jnp.int32, (C, LANES), 1)
    head0 = lane < RET_DK
    low = lane % RET_DK < RET_DK // 2
    ridx = lax.broadcasted_iota(jnp.int32, (C, LANES), 0).astype(F32)

    def body(g, carry):
        sl = pl.ds(pl.multiple_of(g * C, C), C)
        ct, st = ct_ref[sl, :], st_ref[sl, :]
        for p in range(RET_HEADS // 2):
            ps = slice(p * LANES, (p + 1) * LANES)
            qp = _rope_pair(q_ref[sl, ps], ct, st, low) * RET_DK ** -0.5
            kp = _rope_pair(k_ref[sl, ps], ct, st, low)
            kb = kp.astype(BF16)
            for hh in range(2):
                h = 2 * p + hh
                lg = math.log(1.0 - 2.0 ** (-5.0 - h))
                vs = slice(h * RET_DV, (h + 1) * RET_DV)
                msk = head0 if hh == 0 else jnp.logical_not(head0)
                q = jnp.where(msk, qp, 0.0)
                vb = v_ref[sl, vs].astype(BF16)
                dmat = jnp.where(incl, jnp.exp(dif * lg), 0.0)
                sc = _dot_nt(q.astype(BF16), kb) * dmat
                s = s_ref[h]
                qdec = (q * jnp.exp((ridx + 1.0) * lg)).astype(BF16)
                o_ref[sl, vs] = _dot(sc.astype(BF16), vb) + _dot(qdec, s.astype(BF16))
                kdec = (jnp.where(msk, kp, 0.0) * jnp.exp((C - 1.0 - ridx) * lg)).astype(BF16)
                s_ref[h] = s * math.exp(C * lg) + _dot_tn(kdec, vb)
        return carry

    lax.fori_loop(0, G, body, 0)


def retention_chunked(p, q_blk, k_blk, v_blk, ct, st, batch, G=4):
    T = p.shape[0]
    QD, VD = RET_HEADS * RET_DK, RET_HEADS * RET_DV
    C = CHUNK
    seq = T // batch
    G = _pick(seq // C, (G, 2, 1))
    ncb = seq // C // G
    tab = pl.BlockSpec((C * G, LANES), lambda b, n: (b * ncb + n, 0))
    wide = pl.BlockSpec((C * G, VD), lambda b, n: (b * ncb + n, 0))
    return pl.pallas_call(
        functools.partial(_ret_kernel, G=G),
        grid=(batch, ncb),
        in_specs=[pl.BlockSpec((C * G, QD), lambda b, n: (b * ncb + n, q_blk)),
                  pl.BlockSpec((C * G, QD), lambda b, n: (b * ncb + n, k_blk)),
                  pl.BlockSpec((C * G, VD), lambda b, n: (b * ncb + n, v_blk)),
                  tab, tab],
        out_specs=wide,
        out_shape=jax.ShapeDtypeStruct((T, VD), F32),
        scratch_shapes=[pltpu.VMEM((RET_HEADS, LANES, RET_DV), F32)],
        compiler_params=_cparams(("arbitrary", "arbitrary")),
        name="retention",
    )(p, p, p, ct, st)


def _rope_tables(positions, d):
    inv = ROPE_THETA ** (-jnp.arange(0, d, 2, dtype=F32) / d)
    ang = positions.astype(F32).reshape(-1, 1) * inv
    cos, sin = jnp.cos(ang), jnp.sin(ang)
    reps = LANES // d
    return jnp.concatenate([cos, cos] * reps, -1), jnp.concatenate([-sin, sin] * reps, -1)


L0_MLA_W = 768
L0_LORA_W = 384
L0_N = 3 * RWKV_DIM + L0_MLA_W + L0_LORA_W
CQ_PAD = 512


def _proj_shift_kernel(x_ref, xh_ref, w_ref, mu_ref, o_ref, *, tiles_per_seq):
    m = pl.program_id(0)
    w = w_ref[...]
    p = _dot(x_ref[...].astype(BF16), w)
    ph = _dot(xh_ref[...].astype(BF16), w)
    ph = jnp.where(m % tiles_per_seq == 0, 0.0, ph)
    rows = lax.broadcasted_iota(jnp.int32, p.shape, 0)
    prev = jnp.where(rows == 0, ph[SUBLANES - 1:SUBLANES], pltpu.roll(p, 1, 0))
    o_ref[...] = p + mu_ref[...] * (prev - p)


def proj_shift(x, w, mu, seq, tm=512, tn=1408):
    M, D = x.shape
    N = w.shape[1]
    tm = _pick(seq, (tm, 256, 128, 64, 32, 16, 8))
    tn = _pick(N, (tn, 512, 384, 256, 128))
    hb = tm // SUBLANES
    return pl.pallas_call(
        functools.partial(_proj_shift_kernel, tiles_per_seq=seq // tm),
        grid=(M // tm, N // tn),
        in_specs=[pl.BlockSpec((tm, D), lambda i, j: (i, 0)),
                  pl.BlockSpec((SUBLANES, D), lambda i, j: (jnp.maximum(i * hb - 1, 0), 0)),
                  pl.BlockSpec((D, tn), lambda i, j: (0, j)),
                  pl.BlockSpec((1, tn), lambda i, j: (0, j))],
        out_specs=pl.BlockSpec((tm, tn), lambda i, j: (i, j)),
        out_shape=jax.ShapeDtypeStruct((M, N), F32),
        compiler_params=_cparams(("parallel", "parallel")),
        name="proj_shift",
    )(x, x, w, mu.reshape(1, N))


def _mla_pre_kernel(p_ref, ct_ref, st_ref, qn_ref, kn_ref, wq_ref, wkv_ref, q_ref, k_ref, v_ref):
    p = p_ref[...]
    cq = p[:, :CQ_PAD]
    ckv = p[:, CQ_PAD:CQ_PAD + MLA_KV_RANK]
    kp = p[:, CQ_PAD + MLA_KV_RANK:]
    cqn = cq * lax.rsqrt(jnp.sum(cq * cq, -1, keepdims=True) * (1.0 / MLA_Q_RANK) + RMS_EPS) * qn_ref[...]
    ckvn = ckv * lax.rsqrt(jnp.mean(ckv * ckv, -1, keepdims=True) + RMS_EPS) * kn_ref[...]
    qa = _dot(cqn.astype(BF16), wq_ref[...])
    kva = _dot(ckvn.astype(BF16), wkv_ref[...])
    ct, st = ct_ref[...], st_ref[...]
    kr = (kp * ct + pltpu.roll(kp, MLA_ROPE, 1) * st)[:, :MLA_ROPE].astype(BF16)
    ones = jnp.ones((p.shape[0], MLA_V), BF16)
    hw = 2 * LANES
    for h in range(MLA_HEADS):
        qh = qa[:, h * hw:(h + 1) * hw]
        pe = qh[:, MLA_NOPE:]
        q_ref[0, h, :, :MLA_NOPE] = qh[:, :MLA_NOPE].astype(BF16)
        q_ref[0, h, :, MLA_NOPE:] = (pe * ct + pltpu.roll(pe, MLA_ROPE, 1) * st)[:, :MLA_ROPE].astype(BF16)
        kvh = kva[:, h * hw:(h + 1) * hw]
        k_ref[0, h, :, :MLA_NOPE] = kvh[:, :MLA_NOPE].astype(BF16)
        k_ref[0, h, :, MLA_NOPE:] = kr
        v_ref[0, h, :, :MLA_V] = kvh[:, MLA_NOPE:].astype(BF16)
        v_ref[0, h, :, MLA_V:] = ones


def mla_pre(p, mla_blk, ct, st, qn, kn, wq, wkv, batch, tm=512):
    T = p.shape[0]
    seq = T // batch
    tm = _pick(seq, (tm, 256, 128))
    nt = seq // tm
    dqk = MLA_NOPE + MLA_ROPE
    head_spec = lambda d: pl.BlockSpec((1, MLA_HEADS, tm, d), lambda i: (i // nt, 0, i % nt, 0))
    full = lambda a: pl.BlockSpec(a.shape, lambda i: (0,) * a.ndim)
    return pl.pallas_call(
        _mla_pre_kernel,
        grid=(T // tm,),
        in_specs=[pl.BlockSpec((tm, L0_MLA_W), lambda i: (i, mla_blk)),
                  pl.BlockSpec((tm, LANES), lambda i: (i, 0)),
                  pl.BlockSpec((tm, LANES), lambda i: (i, 0)),
                  full(qn), full(kn), full(wq), full(wkv)],
        out_specs=[head_spec(dqk), head_spec(dqk), head_spec(2 * MLA_V)],
        out_shape=[jax.ShapeDtypeStruct((batch, MLA_HEADS, seq, dqk), BF16),
                   jax.ShapeDtypeStruct((batch, MLA_HEADS, seq, dqk), BF16),
                   jax.ShapeDtypeStruct((batch, MLA_HEADS, seq, 2 * MLA_V), BF16)],
        compiler_params=_cparams(("parallel",)),
        name="mla_pre",
    )(p, ct, st, qn, kn, wq, wkv)


def _softplus(x):
    return jnp.maximum(x, 0.0) + jnp.log(1.0 + jnp.exp(-jnp.abs(x)))


def _group_sum(x, ones_bd, terms):
    parts = _split3(x)[:terms]
    out = _dot(parts[0], ones_bd)
    for t in parts[1:]:
        out = out + _dot(t, ones_bd)
    return out


def _rwkv_pre_kernel(k_ref, lora_ref, wl_ref, g2_ref, w0_ref, a0_ref, kk_ref, ka_ref, ones_ref,
                     lw_ref, k2_ref, av_ref, bv_ref, g_ref):
    lora = lora_ref[...]
    lane = lax.broadcasted_iota(jnp.int32, (lora.shape[0], LANES), 1)
    wa_in = jnp.where(lane < DECAY_LORA, jnp.tanh(lora[:, :LANES]), lora[:, :LANES])
    wa = _dot(wa_in.astype(BF16), wl_ref[...])
    g_ref[...] = _dot(jax.nn.sigmoid(lora[:, LANES:]).astype(BF16), g2_ref[...])
    w_log = -_softplus(-(w0_ref[...] + wa[:, :RWKV_DIM])) - 0.5
    lw_ref[...] = -jnp.exp(w_log)
    a = jax.nn.sigmoid(a0_ref[...] + wa[:, RWKV_DIM:])
    k = k_ref[...]
    kk = k * kk_ref[...]
    kkn = kk * lax.rsqrt(_group_sum(kk * kk, ones_ref[...], 2) + 1e-6)
    k2_ref[...] = k * (1.0 + (a - 1.0) * ka_ref[...])
    av_ref[...] = -kkn
    bv_ref[...] = kkn * a


def rwkv_pre(p, k_blk, lora_blk, wl, g2, w0, a0, kk, ka, ones_bd, tm=512):
    T = p.shape[0]
    tm = _pick(T, (tm, 256, 128, 64))
    row = pl.BlockSpec((tm, RWKV_DIM), lambda i: (i, 0))
    full = lambda a: pl.BlockSpec(a.shape, lambda i: (0,) * a.ndim)
    vec = lambda a: a.reshape(1, RWKV_DIM)
    args = (wl, g2, vec(w0), vec(a0), vec(kk), vec(ka), ones_bd)
    return pl.pallas_call(
        _rwkv_pre_kernel,
        grid=(T // tm,),
        in_specs=[pl.BlockSpec((tm, RWKV_DIM), lambda i: (i, k_blk)),
                  pl.BlockSpec((tm, L0_LORA_W), lambda i: (i, lora_blk))] + [full(a) for a in args],
        out_specs=[row] * 5,
        out_shape=[jax.ShapeDtypeStruct((T, RWKV_DIM), F32)] * 5,
        compiler_params=_cparams(("parallel",)),
        name="rwkv_pre",
    )(p, p, *args)


def _rwkv_post_kernel(y_ref, r_ref, k2_ref, v_ref, g_ref, rk_ref, gw_ref, gb_ref, ones_ref, o_ref):
    ones_bd = ones_ref[...]
    y = y_ref[...]
    yc = y - _group_sum(y, ones_bd, 2) * (1.0 / RWKV_HEAD)
    var = _group_sum(yc * yc, ones_bd, 1) * (1.0 / RWKV_HEAD)
    yn = yc * lax.rsqrt(var + RWKV_GN_EPS) * gw_ref[...] + gb_ref[...]
    bonus = _group_sum(r_ref[...] * k2_ref[...] * rk_ref[...], ones_bd, 2) * v_ref[...]
    o_ref[...] = ((yn + bonus) * g_ref[...]).astype(o_ref.dtype)


def rwkv_post(y, p, r_blk, v_blk, k2, g, rk, gw, gb, ones_bd, tm=512):
    T = y.shape[0]
    tm = _pick(T, (tm, 256, 128, 64))
    row = pl.BlockSpec((tm, RWKV_DIM), lambda i: (i, 0))
    full = lambda a: pl.BlockSpec(a.shape, lambda i: (0,) * a.ndim)
    vec = lambda a: a.reshape(1, RWKV_DIM)
    args = (vec(rk), vec(gw), vec(gb), ones_bd)
    return pl.pallas_call(
        _rwkv_post_kernel,
        grid=(T // tm,),
        in_specs=[row, pl.BlockSpec((tm, RWKV_DIM), lambda i: (i, r_blk)), row,
                  pl.BlockSpec((tm, RWKV_DIM), lambda i: (i, v_blk)), row] + [full(a) for a in args],
        out_specs=row,
        out_shape=jax.ShapeDtypeStruct((T, RWKV_DIM), BF16),
        compiler_params=_cparams(("parallel",)),
        name="rwkv_post",
    )(y, p, k2, p, g, *args)


def _mm2_ln_kernel(a1_ref, a2_ref, w1_ref, w2_ref, x_ref, g_ref, b_ref, o_ref):
    y = ALPHA * x_ref[...] + _dot(a1_ref[...], w1_ref[...]) + _dot(a2_ref[...], w2_ref[...])
    mu = jnp.mean(y, axis=-1, keepdims=True)
    yc = y - mu
    var = jnp.mean(yc * yc, axis=-1, keepdims=True)
    o_ref[...] = yc * lax.rsqrt(var + LN_EPS) * g_ref[...] + b_ref[...]


def mm2_ln(a1, a2, w, x, g, b, tm=512):
    M, K1 = a1.shape
    K2 = a2.shape[1]
    N = w.shape[1]
    tm = _pick(M, (tm, 256, 128, 64, 32, 16, 8))
    w = w.astype(BF16)
    row = lambda k: pl.BlockSpec((tm, k), lambda i: (i, 0))
    return pl.pallas_call(
        _mm2_ln_kernel,
        grid=(M // tm,),
        in_specs=[row(K1), row(K2),
                  pl.BlockSpec((K1, N), lambda i: (0, 0)),
                  pl.BlockSpec((K2, N), lambda i: (0, 0)),
                  row(N),
                  pl.BlockSpec((1, N), lambda i: (0, 0)),
                  pl.BlockSpec((1, N), lambda i: (0, 0))],
        out_specs=row(N),
        out_shape=jax.ShapeDtypeStruct((M, N), F32),
        compiler_params=_cparams(("parallel",)),
        name="mm2_ln",
    )(a1, a2, w[:K1], w[K1:], x, g.reshape(1, N), b.reshape(1, N))


def _half_swap(n):
    return jnp.concatenate([jnp.arange(n // 2, n), jnp.arange(0, n // 2)])


def _mla_rwkv_mixer(xb, B, S, ct, st, w_in, q_norm, w_uq, kv_norm, w_ukv, rwkv_mu, rwkv_w0, rwkv_w2,
                    rwkv_a0, rwkv_a2, rwkv_g2, rwkv_k_k, rwkv_k_a, rwkv_r_k, rwkv_gn_w, rwkv_gn_b,
                    w_out, ln_g, ln_b):
    T = B * S
    D = xb.shape[1]
    o_kv = MLA_Q_RANK
    o_pe = o_kv + MLA_KV_RANK
    o_r = o_pe + MLA_ROPE
    o_k, o_v = o_r + RWKV_DIM, o_r + 2 * RWKV_DIM
    o_xw = o_r + 3 * RWKV_DIM
    o_xg = o_xw + DECAY_LORA + AAA_LORA
    zc = lambda n: jnp.zeros((D, n), F32)
    w_pe = w_in[:, o_pe:o_r]
    w0p = jnp.concatenate([w_in[:, o_r:o_xw], w_in[:, :o_kv], zc(CQ_PAD - MLA_Q_RANK), w_in[:, o_kv:o_pe],
                           w_pe, w_pe[:, _half_swap(MLA_ROPE)], w_in[:, o_xw:], zc(L0_N - L0_MLA_W - 3 * RWKV_DIM
                                                                                   - (w_in.shape[1] - o_xw))],
                          axis=1).astype(BF16)
    mu = rwkv_mu
    mu_p = jnp.concatenate([mu[:3 * RWKV_DIM], jnp.zeros((L0_MLA_W,), F32), mu[3 * RWKV_DIM:],
                            jnp.zeros((L0_N - L0_MLA_W - mu.shape[0],), F32)])
    p = proj_shift(xb, w0p, mu_p, S)
    dqk = MLA_NOPE + MLA_ROPE
    qs = dqk ** -0.5 * math.log2(math.e)
    wq = w_uq.reshape(MLA_Q_RANK, MLA_HEADS, dqk) * qs
    wq = jnp.concatenate([wq, wq[..., MLA_NOPE:][..., _half_swap(MLA_ROPE)]], -1).reshape(MLA_Q_RANK, -1)
    wq = jnp.pad(wq, ((0, CQ_PAD - MLA_Q_RANK), (0, 0))).astype(BF16)
    qn = jnp.pad(q_norm, (0, CQ_PAD - MLA_Q_RANK)).reshape(1, CQ_PAD)
    q, k, v = mla_pre(p, 3 * RWKV_DIM // L0_MLA_W, ct, st, qn, kv_norm.reshape(1, -1), wq,
                      w_ukv.astype(BF16), B)
    o_mla = flash_attention(q, k, v)
    zl = jnp.zeros((DECAY_LORA, RWKV_DIM), F32)
    wl = jnp.concatenate([jnp.concatenate([rwkv_w2, zl], 1), jnp.concatenate([zl, rwkv_a2], 1)], 0).astype(BF16)
    g2 = jnp.pad(rwkv_g2, ((0, L0_LORA_W - LANES - GATE_LORA), (0, 0))).astype(BF16)
    hid = jnp.arange(RWKV_DIM) // RWKV_HEAD
    ones_bd = (hid[:, None] == hid[None, :]).astype(BF16)
    lw, k2, av, bv, g = rwkv_pre(p, 1, (3 * RWKV_DIM + L0_MLA_W) // L0_LORA_W, wl, g2, rwkv_w0, rwkv_a0,
                                 rwkv_k_k, rwkv_k_a, ones_bd)
    y = rwkv7_chunked(p, 0, 2 * RWKV_DIM // LANES, lw, k2, av, bv, B)
    o_rwkv = rwkv_post(y, p, 0, 2, k2, g, rwkv_r_k.reshape(-1), rwkv_gn_w, rwkv_gn_b, ones_bd)
    return mm2_ln(o_mla, o_rwkv, w_out, xb, ln_g, ln_b)


GDN_QKV_W = 2 * GDN_QK_HEADS * GDN_DK + GDN_V_HEADS * GDN_DV


def _gdn_pre_kernel(x_ref, xh_ref, ba_ref, cw_ref, al_ref, dt_ref, q_ref, k_ref, v_ref, bg_ref, *,
                    tiles_per_seq):
    m = pl.program_id(0)
    x = x_ref[...]
    xh = jnp.where(m % tiles_per_seq == 0, 0.0, xh_ref[...])
    rows = lax.broadcasted_iota(jnp.int32, x.shape, 0)
    cw = cw_ref[...]
    acc = x * cw[GDN_CONV - 1:GDN_CONV]
    for j in range(1, GDN_CONV):
        sh = pltpu.roll(x, j, 0)
        for i in range(j):
            sh = jnp.where(rows == i, xh[SUBLANES - j + i:SUBLANES - j + i + 1], sh)
        acc = acc + sh * cw[GDN_CONV - 1 - j:GDN_CONV - j]
    y = acc * jax.nn.sigmoid(acc)
    qk_w = GDN_QK_HEADS * GDN_DK
    for h in range(GDN_QK_HEADS):
        cs = slice(h * GDN_DK, (h + 1) * GDN_DK)
        yq = y[:, cs]
        q_ref[:, cs] = yq * (lax.rsqrt(jnp.sum(yq * yq, -1, keepdims=True) + 1e-6) * GDN_DK ** -0.5)
        yk = y[:, qk_w + h * GDN_DK:qk_w + (h + 1) * GDN_DK]
        k_ref[:, cs] = yk * lax.rsqrt(jnp.sum(yk * yk, -1, keepdims=True) + 1e-6)
    v_ref[...] = y[:, 2 * qk_w:]
    ba = ba_ref[...]
    lane = lax.broadcasted_iota(jnp.int32, ba.shape, 1)
    g = -jnp.exp(al_ref[...]) * _softplus(ba + dt_ref[...])
    bg_ref[...] = jnp.where(lane < GDN_V_HEADS, jax.nn.sigmoid(ba), g)


def gdn_pre(p, ba_blk, conv_w, a_log, dt_bias, seq, tm=256):
    T = p.shape[0]
    tm = _pick(seq, (tm, 128, 64))
    hb = tm // SUBLANES
    qk_w = GDN_QK_HEADS * GDN_DK
    v_w = GDN_V_HEADS * GDN_DV
    lane_pad = lambda a: jnp.pad(a, (GDN_V_HEADS, LANES - 2 * GDN_V_HEADS)).reshape(1, LANES)
    row = lambda n: pl.BlockSpec((tm, n), lambda i: (i, 0))
    full = lambda shp: pl.BlockSpec(shp, lambda i: (0, 0))
    return pl.pallas_call(
        functools.partial(_gdn_pre_kernel, tiles_per_seq=seq // tm),
        grid=(T // tm,),
        in_specs=[row(GDN_QKV_W),
                  pl.BlockSpec((SUBLANES, GDN_QKV_W), lambda i: (jnp.maximum(i * hb - 1, 0), 0)),
                  pl.BlockSpec((tm, LANES), lambda i: (i, ba_blk)),
                  full((GDN_CONV, GDN_QKV_W)), full((1, LANES)), full((1, LANES))],
        out_specs=[row(qk_w), row(qk_w), row(v_w), row(LANES)],
        out_shape=[jax.ShapeDtypeStruct((T, qk_w), F32), jax.ShapeDtypeStruct((T, qk_w), F32),
                   jax.ShapeDtypeStruct((T, v_w), F32), jax.ShapeDtypeStruct((T, LANES), F32)],
        compiler_params=_cparams(("parallel",)),
        name="gdn_pre",
    )(p, p, p, conv_w, lane_pad(a_log), lane_pad(dt_bias))


def _l1_post_kernel(og_ref, z_ref, or_ref, rg_ref, gn_ref, rw_ref, rb_ref, o_ref):
    v_w = GDN_V_HEADS * GDN_DV
    for h in range(GDN_V_HEADS):
        cs = slice(h * LANES, (h + 1) * LANES)
        o = og_ref[:, cs]
        z = z_ref[:, cs]
        on = o * lax.rsqrt(jnp.mean(o * o, -1, keepdims=True) + RMS_EPS) * gn_ref[...]
        o_ref[:, cs] = (on * (z * jax.nn.sigmoid(z))).astype(o_ref.dtype)
        r = or_ref[:, cs]
        rc = r - jnp.mean(r, -1, keepdims=True)
        rn = rc * lax.rsqrt(jnp.mean(rc * rc, -1, keepdims=True) + LN_EPS) * rw_ref[:, cs] + rb_ref[:, cs]
        g = rg_ref[:, cs]
        o_ref[:, v_w + h * LANES:v_w + (h + 1) * LANES] = (rn * (g * jax.nn.sigmoid(g))).astype(o_ref.dtype)


def l1_post(o_gdn, o_ret, p, z_blk, rg_blk, gdn_norm, ret_gn_w, ret_gn_b, tm=512):
    T, W = o_gdn.shape
    tm = _pick(T, (tm, 256, 128, 64))
    row = pl.BlockSpec((tm, W), lambda i: (i, 0))
    return pl.pallas_call(
        _l1_post_kernel,
        grid=(T // tm,),
        in_specs=[row, pl.BlockSpec((tm, W), lambda i: (i, z_blk)), row,
                  pl.BlockSpec((tm, W), lambda i: (i, rg_blk)),
                  pl.BlockSpec((1, LANES), lambda i: (0, 0)),
                  pl.BlockSpec((1, W), lambda i: (0, 0)),
                  pl.BlockSpec((1, W), lambda i: (0, 0))],
        out_specs=pl.BlockSpec((tm, 2 * W), lambda i: (i, 0)),
        out_shape=jax.ShapeDtypeStruct((T, 2 * W), BF16),
        compiler_params=_cparams(("parallel",)),
        name="l1_post",
    )(o_gdn, p, o_ret, p, gdn_norm.reshape(1, LANES), ret_gn_w.reshape(1, W), ret_gn_b.reshape(1, W))


def _gdn_retention_mixer(xb, B, S, ct, st, w_in, gdn_conv_w, gdn_A_log, gdn_dt_bias, gdn_norm,
                         ret_gn_w, ret_gn_b, w_out, ln_g, ln_b):
    D = xb.shape[1]
    v_w = GDN_V_HEADS * GDN_DV
    n_gdn = GDN_QKV_W + v_w
    w1p = jnp.concatenate([w_in[:, :n_gdn], w_in[:, n_gdn + 2 * GDN_V_HEADS:],
                           w_in[:, n_gdn:n_gdn + 2 * GDN_V_HEADS],
                           jnp.zeros((D, LANES - 2 * GDN_V_HEADS), F32)], axis=1)
    p = mm(xb, w1p, tn=896)
    q, k, v, bg = gdn_pre(p, (w1p.shape[1] - LANES) // LANES, gdn_conv_w, gdn_A_log, gdn_dt_bias, S)
    o_gdn = gdn_chunked(q, k, v, bg[:, GDN_V_HEADS:2 * GDN_V_HEADS], bg[:, :GDN_V_HEADS], B)
    qd = RET_HEADS * RET_DK
    o_ret = retention_chunked(p, n_gdn // qd, n_gdn // qd + 1, (n_gdn + 2 * qd) // v_w, ct, st, B)
    m1 = l1_post(o_gdn, o_ret, p, GDN_QKV_W // v_w, (n_gdn + 2 * qd + v_w) // v_w, gdn_norm, ret_gn_w, ret_gn_b)
    return mm_ln(m1, w_out, xb, ln_g, ln_b)


def _conv_ffn_ln(x, S, w_gate, w_val, conv_w, conv_b, w_down, ln_g, ln_b):
    h = ffn1(x, w_gate, w_val, conv_w, conv_b, S)
    return mm_ln(h, w_down, x, ln_g, ln_b)


def kernel(x, positions, l0_w_in, l0_q_norm, l0_w_uq, l0_kv_norm, l0_w_ukv, l0_rwkv_mu, l0_rwkv_w0, l0_rwkv_w2, l0_rwkv_a0, l0_rwkv_a2, l0_rwkv_g2, l0_rwkv_k_k, l0_rwkv_k_a, l0_rwkv_r_k, l0_rwkv_gn_w, l0_rwkv_gn_b, l0_w_out, l0_ln1_g, l0_ln1_b, l0_ffn_w_gate, l0_ffn_w_val, l0_ffn_conv_w, l0_ffn_conv_b, l0_ffn_w_down, l0_ln2_g, l0_ln2_b, l1_w_in, l1_gdn_conv_w, l1_gdn_A_log, l1_gdn_dt_bias, l1_gdn_norm, l1_ret_gn_w, l1_ret_gn_b, l1_w_out, l1_ln1_g, l1_ln1_b, l1_ffn_w_gate, l1_ffn_w_val, l1_ffn_conv_w, l1_ffn_conv_b, l1_ffn_w_down, l1_ln2_g, l1_ln2_b):
    B, S, D = x.shape
    T = B * S
    xb = x.reshape(T, D)
    assert MLA_ROPE == RET_DK
    ct, st = _rope_tables(positions, MLA_ROPE)
    xb = _mla_rwkv_mixer(xb, B, S, ct, st, l0_w_in, l0_q_norm, l0_w_uq, l0_kv_norm, l0_w_ukv, l0_rwkv_mu,
                         l0_rwkv_w0, l0_rwkv_w2, l0_rwkv_a0, l0_rwkv_a2, l0_rwkv_g2, l0_rwkv_k_k, l0_rwkv_k_a,
                         l0_rwkv_r_k, l0_rwkv_gn_w, l0_rwkv_gn_b, l0_w_out, l0_ln1_g, l0_ln1_b)
    xb = _conv_ffn_ln(xb, S, l0_ffn_w_gate, l0_ffn_w_val, l0_ffn_conv_w, l0_ffn_conv_b, l0_ffn_w_down,
                      l0_ln2_g, l0_ln2_b)
    xb = _gdn_retention_mixer(xb, B, S, ct, st, l1_w_in, l1_gdn_conv_w, l1_gdn_A_log, l1_gdn_dt_bias,
                              l1_gdn_norm, l1_ret_gn_w, l1_ret_gn_b, l1_w_out, l1_ln1_g, l1_ln1_b)
    xb = _conv_ffn_ln(xb, S, l1_ffn_w_gate, l1_ffn_w_val, l1_ffn_conv_w, l1_ffn_conv_b, l1_ffn_w_down,
                      l1_ln2_g, l1_ln2_b)
    return xb.reshape(B, S, D)
```

```python
import functools
import math

import jax
import jax.numpy as jnp
from jax import lax
from jax.experimental import pallas as pl
from jax.experimental.pallas import tpu as pltpu

F32 = jnp.float32
BF16 = jnp.bfloat16

DEPTH = 2
ALPHA = (2 * DEPTH) ** 0.25
LN_EPS = 1e-5
RMS_EPS = 1e-6
ROPE_THETA = 10000.0

MLA_HEADS = 8
MLA_Q_RANK = 448
MLA_KV_RANK = 128
MLA_NOPE = 128
MLA_ROPE = 64
MLA_V = 128
RWKV_HEADS = 16
RWKV_HEAD = 64
RWKV_DIM = RWKV_HEADS * RWKV_HEAD
DECAY_LORA = 64
AAA_LORA = 64
GATE_LORA = 160
RWKV_GN_EPS = 64e-5
GDN_QK_HEADS = 4
GDN_V_HEADS = 8
GDN_DK = 128
GDN_DV = 128
GDN_CONV = 4
RET_HEADS = 8
RET_DK = 64
RET_DV = 128
FFN_CONV = 3

CHUNK = 64
LANES = 128
SUBLANES = 8
VMEM_LIMIT = 56 * 1024 * 1024


def _cparams(sem):
    return pltpu.CompilerParams(dimension_semantics=sem, vmem_limit_bytes=VMEM_LIMIT)


def _dot(a, b):
    return jnp.dot(a, b, preferred_element_type=F32)


def _dot_nt(a, b):
    return lax.dot_general(a, b, (((1,), (1,)), ((), ())), preferred_element_type=F32)


def _dot_tn(a, b):
    return lax.dot_general(a, b, (((0,), (0,)), ((), ())), preferred_element_type=F32)


def _split3(x):
    hi = x.astype(BF16)
    r1 = x - hi.astype(F32)
    mid = r1.astype(BF16)
    lo = (r1 - mid.astype(F32)).astype(BF16)
    return hi, mid, lo


def _dot_exact_lhs(m_bf16, x):
    hi, mid, lo = _split3(x)
    return _dot(m_bf16, hi) + _dot(m_bf16, mid) + _dot(m_bf16, lo)


def _mm_kernel(a_ref, w_ref, o_ref, *scratch, nk):
    if nk == 1:
        o_ref[...] = _dot(a_ref[...].astype(BF16), w_ref[...]).astype(o_ref.dtype)
        return
    acc_ref, = scratch
    k = pl.program_id(2)

    @pl.when(k == 0)
    def _():
        acc_ref[...] = jnp.zeros_like(acc_ref)

    acc_ref[...] += _dot(a_ref[...].astype(BF16), w_ref[...])

    @pl.when(k == nk - 1)
    def _():
        o_ref[...] = acc_ref[...].astype(o_ref.dtype)


def _pick(n, pref):
    for t in pref:
        if n % t == 0:
            return t
    return n


def mm(a, w, out_dtype=F32, tm=512, tn=None, tk=None):
    M, K = a.shape
    N = w.shape[1]
    tm = _pick(M, (tm, 256, 128, 64, 32, 16, 8))
    tn = tn or _pick(N, (512, 384, 256, 128))
    tk = tk or K
    nk = K // tk
    w = w.astype(BF16)
    scratch = [] if nk == 1 else [pltpu.VMEM((tm, tn), F32)]
    return pl.pallas_call(
        functools.partial(_mm_kernel, nk=nk),
        grid=(M // tm, N // tn, nk),
        in_specs=[pl.BlockSpec((tm, tk), lambda i, j, k: (i, k)),
                  pl.BlockSpec((tk, tn), lambda i, j, k: (k, j))],
        out_specs=pl.BlockSpec((tm, tn), lambda i, j, k: (i, j)),
        out_shape=jax.ShapeDtypeStruct((M, N), out_dtype),
        scratch_shapes=scratch,
        compiler_params=_cparams(("parallel", "parallel", "arbitrary")),
        name="mm",
    )(a, w)


def _mm_ln_kernel(a_ref, w_ref, x_ref, g_ref, b_ref, o_ref, acc_ref, *, nk):
    k = pl.program_id(1)

    @pl.when(k == 0)
    def _():
        acc_ref[...] = jnp.zeros_like(acc_ref)

    acc_ref[...] += _dot(a_ref[...].astype(BF16), w_ref[...])

    @pl.when(k == nk - 1)
    def _():
        y = ALPHA * x_ref[...] + acc_ref[...]
        mu = jnp.mean(y, axis=-1, keepdims=True)
        yc = y - mu
        var = jnp.mean(yc * yc, axis=-1, keepdims=True)
        o_ref[...] = yc * lax.rsqrt(var + LN_EPS) * g_ref[...] + b_ref[...]


def mm_ln(a, w, x, g, b, tm=512, tk=None):
    M, K = a.shape
    N = w.shape[1]
    tm = _pick(M, (tm, 256, 128, 64, 32, 16, 8))
    tk = tk or _pick(K, (1024, 1408, 512, 256, 128))
    nk = K // tk
    return pl.pallas_call(
        functools.partial(_mm_ln_kernel, nk=nk),
        grid=(M // tm, nk),
        in_specs=[pl.BlockSpec((tm, tk), lambda i, k: (i, k)),
                  pl.BlockSpec((tk, N), lambda i, k: (k, 0)),
                  pl.BlockSpec((tm, N), lambda i, k: (i, 0)),
                  pl.BlockSpec((1, N), lambda i, k: (0, 0)),
                  pl.BlockSpec((1, N), lambda i, k: (0, 0))],
        out_specs=pl.BlockSpec((tm, N), lambda i, k: (i, 0)),
        out_shape=jax.ShapeDtypeStruct((M, N), F32),
        scratch_shapes=[pltpu.VMEM((tm, N), F32)],
        compiler_params=_cparams(("parallel", "arbitrary")),
        name="mm_ln",
    )(a, w.astype(BF16), x, g.reshape(1, N), b.reshape(1, N))


def _ffn1_kernel(x_ref, xh_ref, wg_ref, wv_ref, cw_ref, cb_ref, o_ref, *, tiles_per_seq):
    m = pl.program_id(0)
    x = x_ref[...].astype(BF16)
    g = _dot(x, wg_ref[...])
    gh = _dot(xh_ref[...].astype(BF16), wg_ref[...])
    gh = jnp.where(m % tiles_per_seq == 0, 0.0, gh)
    rows = lax.broadcasted_iota(jnp.int32, g.shape, 0)
    g1 = jnp.where(rows == 0, gh[SUBLANES - 1:SUBLANES], pltpu.roll(g, 1, 0))
    g2 = jnp.where(rows == 0, gh[SUBLANES - 2:SUBLANES - 1],
                   jnp.where(rows == 1, gh[SUBLANES - 1:SUBLANES], pltpu.roll(g, 2, 0)))
    cw = cw_ref[...]
    h = g * cw[2:3] + g1 * cw[1:2] + g2 * cw[0:1] + cb_ref[...]
    v = _dot(x, wv_ref[...])
    o_ref[...] = (h * jax.nn.sigmoid(h) * v).astype(o_ref.dtype)


def ffn1(x, w_gate, w_val, conv_w, conv_b, seq, tm=1024, tf=512):
    M, D = x.shape
    Fd = w_gate.shape[1]
    tm = _pick(seq, (tm, 256, 128, 64, 32, 16, 8))
    tf = _pick(Fd, (tf, 256, 128))
    hb = tm // SUBLANES
    return pl.pallas_call(
        functools.partial(_ffn1_kernel, tiles_per_seq=seq // tm),
        grid=(M // tm, Fd // tf),
        in_specs=[pl.BlockSpec((tm, D), lambda i, j: (i, 0)),
                  pl.BlockSpec((SUBLANES, D), lambda i, j: (jnp.maximum(i * hb - 1, 0), 0)),
                  pl.BlockSpec((D, tf), lambda i, j: (0, j)),
                  pl.BlockSpec((D, tf), lambda i, j: (0, j)),
                  pl.BlockSpec((FFN_CONV, tf), lambda i, j: (0, j)),
                  pl.BlockSpec((1, tf), lambda i, j: (0, j))],
        out_specs=pl.BlockSpec((tm, tf), lambda i, j: (i, j)),
        out_shape=jax.ShapeDtypeStruct((M, Fd), BF16),
        compiler_params=_cparams(("parallel", "parallel")),
        name="ffn1",
    )(x, x, w_gate.astype(BF16), w_val.astype(BF16), conv_w, conv_b.reshape(1, Fd))


def _flash_kernel(q_ref, k_ref, v_ref, o_ref, m_ref, acc_ref, s_ref, *, tq, tk, ts, dv):
    qi = pl.program_id(2)
    nsub = tq // ts
    ndiag = tq // tk
    nfull = qi * ndiag
    m_ref[...] = jnp.full_like(m_ref, -jnp.inf)
    acc_ref[...] = jnp.zeros_like(acc_ref)

    def above_diagonal(r, d):
        return (r + 1) * ts <= d * tk

    def scores(r, blk):
        k = k_ref[0, 0, pl.ds(pl.multiple_of(blk * tk, tk), tk), :]
        return _dot_nt(q_ref[0, 0, r * ts:(r + 1) * ts, :], k)

    def consume(blk, diag, nxt_diag):
        v = v_ref[0, 0, pl.ds(pl.multiple_of(blk * tk, tk), tk), :]
        for r in range(nsub):
            rs = slice(r * ts, (r + 1) * ts)
            live = diag is None or not above_diagonal(r, diag)
            if live:
                s = s_ref[rs, :]
            if nxt_diag is None or (nxt_diag >= 0 and not above_diagonal(r, nxt_diag)):
                s_ref[rs, :] = scores(r, blk + 1)
            if not live:
                continue
            if diag is not None and r * ts < (diag + 1) * tk - 1:
                rowi = lax.broadcasted_iota(jnp.int32, s.shape, 0) + r * ts
                coli = lax.broadcasted_iota(jnp.int32, s.shape, 1) + diag * tk
                s = jnp.where(coli <= rowi, s, -jnp.inf)
            m_old = m_ref[rs, :]
            m_new = jnp.maximum(m_old, jnp.max(s, axis=-1, keepdims=True))
            p = jnp.exp2(s - m_new).astype(BF16)
            acc_ref[rs, :] = jnp.exp2(m_old - m_new) * acc_ref[rs, :] + _dot(p, v)
            m_ref[rs, :] = m_new

    for r in range(nsub):
        s_ref[r * ts:(r + 1) * ts, :] = scores(r, 0)

    def body(j, carry):
        consume(j, None, None)
        return carry

    lax.fori_loop(0, nfull, body, 0)
    for d in range(ndiag):
        consume(nfull + d, d, d + 1 if d + 1 < ndiag else -1)
    acc = acc_ref[...]
    o_ref[...] = (acc[:, :dv] / acc[:, dv:]).astype(o_ref.dtype)


def flash_attention(q, k, v, tq=1024, tk=1024, ts=512):
    B, H, S, dq = q.shape
    dv = v.shape[-1] // 2
    tq = _pick(S, (tq, 512, 256, 128))
    tk = _pick(tq, (tk, 256, 128))
    nq = S // tq
    return pl.pallas_call(
        functools.partial(_flash_kernel, tq=tq, tk=tk, ts=_pick(tq, (ts, 128)), dv=dv),
        grid=(B, H, nq),
        in_specs=[pl.BlockSpec((1, 1, tq, dq), lambda b, h, i: (b, h, i, 0)),
                  pl.BlockSpec((1, 1, S, dq), lambda b, h, i: (b, h, 0, 0)),
                  pl.BlockSpec((1, 1, S, 2 * dv), lambda b, h, i: (b, h, 0, 0))],
        out_specs=pl.BlockSpec((tq, dv), lambda b, h, i: (b * nq + i, h)),
        out_shape=jax.ShapeDtypeStruct((B * S, H * dv), BF16),
        scratch_shapes=[pltpu.VMEM((tq, 1), F32), pltpu.VMEM((tq, 2 * dv), F32), pltpu.VMEM((tq, tk), F32)],
        compiler_params=_cparams(("parallel", "parallel", "arbitrary")),
        name="flash",
    )(q, k, v)


def _tri_inv_kernel(a_ref, t_ref):
    C = a_ref.shape[1]
    kk = lax.broadcasted_iota(jnp.int32, (SUBLANES, LANES), 0)
    zero = jnp.zeros((SUBLANES, LANES), F32)
    for i in range(C):
        ng = i // SUBLANES + 1
        acc = [jnp.where(kk == i - g * SUBLANES, 1.0, 0.0).astype(F32) for g in range(ng)]
        for j in range(i):
            a_ij = a_ref[0, i, j:j + 1, :]
            for g in range(j // SUBLANES + 1):
                acc[g] = acc[g] + a_ij * t_ref[0, j, g * SUBLANES:(g + 1) * SUBLANES, :]
        for g in range(C // SUBLANES):
            t_ref[0, i, g * SUBLANES:(g + 1) * SUBLANES, :] = acc[g] if g < ng else zero


def tri_inv(a):
    n_in, C, _ = a.shape
    ng = -(-n_in // LANES)
    N = ng * LANES
    a = jnp.pad(a, ((0, N - n_in), (0, 0), (0, 0)))
    al = a.reshape(ng, LANES, C * C).transpose(0, 2, 1).reshape(ng, C, C, LANES)
    tl = pl.pallas_call(
        _tri_inv_kernel,
        grid=(ng,),
        in_specs=[pl.BlockSpec((1, C, C, LANES), lambda i: (i, 0, 0, 0))],
        out_specs=pl.BlockSpec((1, C, C, LANES), lambda i: (i, 0, 0, 0)),
        out_shape=jax.ShapeDtypeStruct((ng, C, C, LANES), F32),
        compiler_params=_cparams(("parallel",)),
        name="tri_inv",
    )(al)
    return tl.reshape(ng, C * C, LANES).transpose(0, 2, 1).reshape(N, C, C)[:n_in]


def _chunk_masks():
    C = CHUNK
    row = lax.broadcasted_iota(jnp.int32, (C, C), 0)
    col = lax.broadcasted_iota(jnp.int32, (C, C), 1)
    return row, col


def _rwkv_a_kernel(r_ref, lw_ref, k_ref, v_ref, av_ref, bv_ref,
                   aab_ref, mrb_ref, at_ref, akv_ref, rt_ref, bh_ref, yk_ref, kv_ref, egl_ref, *, G):
    C = CHUNK
    row, col = _chunk_masks()
    strict = col < row
    incl = col <= row
    tri = jnp.where(incl, 1.0, 0.0).astype(BF16)
    lane = lax.broadcasted_iota(jnp.int32, (C, LANES), 1)
    head0 = lane < RWKV_HEAD
    bd = (lax.broadcasted_iota(jnp.int32, (LANES, LANES), 0) // RWKV_HEAD ==
          lax.broadcasted_iota(jnp.int32, (LANES, LANES), 1) // RWKV_HEAD)

    n = C * G
    rown = lax.broadcasted_iota(jnp.int32, (n, n), 0)
    coln = lax.broadcasted_iota(jnp.int32, (n, n), 1)
    trin = jnp.where((coln <= rown) & (rown // C == coln // C), 1.0, 0.0).astype(BF16)
    lw_all = lw_ref[...]
    gc_all = _dot_exact_lhs(trin, lw_all)
    sls = [slice(g * C, (g + 1) * C) for g in range(G)]
    masks = (head0, jnp.logical_not(head0))
    at, rt, bt, kt, vb, egl, gl = [], [], [], [], [], [], []
    for sl in sls:
        gc = gc_all[sl]
        gl.append(gc[C - 1:C, :])
        ieg = jnp.exp(-gc)
        egl.append(jnp.exp(gl[-1] - gc))
        at.append(av_ref[sl, :] * jnp.exp(gc - lw_all[sl]))
        rt.append(r_ref[sl, :] * jnp.exp(gc))
        bt.append((bv_ref[sl, :] * ieg).astype(BF16))
        kt.append((k_ref[sl, :] * ieg).astype(BF16))
        vb.append(v_ref[sl, :].astype(BF16))
    xb, xk = [], []
    for g in range(G):
        for msk in masks:
            lhs = jnp.concatenate([jnp.where(msk, at[g], 0.0), jnp.where(msk, rt[g], 0.0)], axis=0).astype(BF16)
            xb.append(_dot_nt(lhs, bt[g]))
            xk.append(_dot_nt(lhs, kt[g]))
    for g, sl in enumerate(sls):
        akv = jnp.zeros((C, LANES), F32)
        yk = jnp.zeros((C, LANES), F32)
        for hh, msk in enumerate(masks):
            b, k_ = xb[2 * g + hh], xk[2 * g + hh]
            aab_ref[g, hh] = jnp.where(strict, b[:C], 0.0)
            mrb_ref[g, hh] = jnp.where(incl, b[C:], 0.0).astype(BF16)
            akv = jnp.where(msk, _dot(jnp.where(strict, k_[:C], 0.0).astype(BF16), vb[g]), akv)
            yk = jnp.where(msk, _dot(jnp.where(incl, k_[C:], 0.0).astype(BF16), vb[g]), yk)
        akv_ref[sl, :] = akv.astype(BF16)
        yk_ref[sl, :] = yk
    for g, sl in enumerate(sls):
        at_ref[sl, :] = at[g].astype(BF16)
        rt_ref[sl, :] = rt[g].astype(BF16)
        bh_ref[sl, :] = (bv_ref[sl, :] * egl[g]).astype(BF16)
        kv_ref[g, 0] = jnp.where(bd, _dot_tn(vb[g], (k_ref[sl, :] * egl[g]).astype(BF16)), 0.0)
        egl_ref[g] = jnp.exp(gl[g])


def _rwkv_c_kernel(at_ref, akv_ref, rt_ref, bh_ref, yk_ref, t_ref, mrb_ref, kv_ref, egl_ref,
                   y_ref, s_ref, *, G):
    C = CHUNK
    npair = RWKV_HEADS // 2

    @pl.when(pl.program_id(1) == 0)
    def _():
        s_ref[...] = jnp.zeros_like(s_ref)

    lane = lax.broadcasted_iota(jnp.int32, (C, LANES), 1)
    head0 = lane < RWKV_HEAD
    head0w = jnp.concatenate([head0, head0], axis=1)
    bd = (lax.broadcasted_iota(jnp.int32, (LANES, LANES), 0) // RWKV_HEAD ==
          lax.broadcasted_iota(jnp.int32, (LANES, LANES), 1) // RWKV_HEAD)

    def body(g, carry):
        sl = pl.ds(pl.multiple_of(g * C, C), C)
        cols = [slice(p * LANES, (p + 1) * LANES) for p in range(npair)]
        w, s, sb, u, ub, y = [], [], [], [], [], []
        for p, cs in enumerate(cols):
            both = jnp.concatenate([at_ref[sl, cs], akv_ref[sl, cs]], axis=1)
            w0 = _dot(t_ref[g, 2 * p].astype(BF16), both)
            w1 = _dot(t_ref[g, 2 * p + 1].astype(BF16), both)
            w.append(jnp.where(head0w, w0, w1))
            s.append(s_ref[p])
            sb.append(s[p].astype(BF16))
        for p, cs in enumerate(cols):
            u.append(_dot_nt(w[p][:, :LANES].astype(BF16), sb[p]) + w[p][:, LANES:])
            ub.append(u[p].astype(BF16))
            y.append(_dot_nt(rt_ref[sl, cs], sb[p]) + yk_ref[sl, cs])
        for p, cs in enumerate(cols):
            y_ref[sl, cs] = y[p] + jnp.where(head0, _dot(mrb_ref[g, 2 * p], ub[p]),
                                             _dot(mrb_ref[g, 2 * p + 1], ub[p]))
            s_ref[p] = (s[p] * egl_ref[g, :, cs] + jnp.where(bd, _dot_tn(ub[p], bh_ref[sl, cs]), 0.0)
                        + kv_ref[g, p])
        return carry

    lax.fori_loop(0, G, body, 0)


def rwkv7_chunked(p, r_blk, v_blk, lw, k, av, bv, batch, G=4):
    T, HD = lw.shape
    C = CHUNK
    seq = T // batch
    nct = T // C
    G = _pick(seq // C, (G, 2, 1))
    npair = HD // LANES
    nh = 2 * npair
    row_spec = pl.BlockSpec((C * G, LANES), lambda i, p: (i, p))
    outs = pl.pallas_call(
        functools.partial(_rwkv_a_kernel, G=G),
        grid=(nct // G, npair),
        in_specs=[pl.BlockSpec((C * G, LANES), lambda i, p: (i, r_blk + p)), row_spec, row_spec,
                  pl.BlockSpec((C * G, LANES), lambda i, p: (i, v_blk + p)), row_spec, row_spec],
        out_specs=[pl.BlockSpec((G, 2, C, C), lambda i, p: (i, p, 0, 0)),
                   pl.BlockSpec((G, 2, C, C), lambda i, p: (i, p, 0, 0)),
                   row_spec, row_spec, row_spec, row_spec, row_spec,
                   pl.BlockSpec((G, 1, LANES, LANES), lambda i, p: (i, p, 0, 0)),
                   pl.BlockSpec((G, 1, LANES), lambda i, p: (i, 0, p))],
        out_shape=[jax.ShapeDtypeStruct((nct, nh, C, C), F32),
                   jax.ShapeDtypeStruct((nct, nh, C, C), BF16),
                   jax.ShapeDtypeStruct((T, HD), BF16),
                   jax.ShapeDtypeStruct((T, HD), BF16),
                   jax.ShapeDtypeStruct((T, HD), BF16),
                   jax.ShapeDtypeStruct((T, HD), BF16),
                   jax.ShapeDtypeStruct((T, HD), F32),
                   jax.ShapeDtypeStruct((nct, npair, LANES, LANES), F32),
                   jax.ShapeDtypeStruct((nct, 1, HD), F32)],
        compiler_params=_cparams(("parallel", "parallel")),
        name="rwkv_a",
    )(p, lw, k, p, av, bv)
    aab, mrb, at, akv, rt, bh, yk, kv, egl = outs
    tinv = tri_inv(aab.reshape(nct * nh, C, C)).reshape(nct, nh, C, C)
    ncb = seq // C // G
    wide = pl.BlockSpec((C * G, HD), lambda b, n: (b * ncb + n, 0))
    return pl.pallas_call(
        functools.partial(_rwkv_c_kernel, G=G),
        grid=(batch, ncb),
        in_specs=[wide, wide, wide, wide, wide,
                  pl.BlockSpec((G, nh, C, C), lambda b, n: (b * ncb + n, 0, 0, 0)),
                  pl.BlockSpec((G, nh, C, C), lambda b, n: (b * ncb + n, 0, 0, 0)),
                  pl.BlockSpec((G, npair, LANES, LANES), lambda b, n: (b * ncb + n, 0, 0, 0)),
                  pl.BlockSpec((G, 1, HD), lambda b, n: (b * ncb + n, 0, 0))],
        out_specs=wide,
        out_shape=jax.ShapeDtypeStruct((T, HD), F32),
        scratch_shapes=[pltpu.VMEM((npair, LANES, LANES), F32)],
        compiler_params=_cparams(("arbitrary", "arbitrary")),
        name="rwkv_c",
    )(at, akv, rt, bh, yk, tinv, mrb, kv, egl)


def _gdn_a_kernel(q_ref, k_ref, v_ref, gcol_ref, grow_ref, bcol_ref,
                  xm_ref, attn_ref, vb_ref, kcp_ref, qd_ref, kd_ref, egl_ref, *, G):
    C = CHUNK
    row, col = _chunk_masks()
    strict = col < row
    incl = col <= row
    triu = jnp.where(row <= col, 1.0, 0.0).astype(BF16)
    n = C * G
    rown = lax.broadcasted_iota(jnp.int32, (n, n), 0)
    coln = lax.broadcasted_iota(jnp.int32, (n, n), 1)
    trin = jnp.where((coln <= rown) & (rown // C == coln // C), 1.0, 0.0).astype(BF16)
    gc_all = _dot_exact_lhs(trin, jnp.broadcast_to(gcol_ref[0], (n, LANES)))
    hi, mid, lo = _split3(grow_ref[0, 0])
    gcr_all = _dot(hi, triu) + _dot(mid, triu) + _dot(lo, triu)
    beta_all = jnp.broadcast_to(bcol_ref[0], (n, LANES))
    sls = [slice(g * C, (g + 1) * C) for g in range(G)]
    kk, qk = [], []
    for sl in sls:
        k = k_ref[sl, :]
        kbf = k.astype(BF16)
        kk.append(_dot_nt((k * beta_all[sl]).astype(BF16), kbf))
        qk.append(_dot_nt(q_ref[sl, :].astype(BF16), kbf))
    for g, sl in enumerate(sls):
        gc = gc_all[sl]
        diff = gc[:, :C] - gcr_all[g:g + 1]
        decay = jnp.where(incl, jnp.exp(jnp.where(incl, diff, 0.0)), 0.0)
        xm_ref[g, 0] = jnp.where(strict, -kk[g] * decay, 0.0)
        attn_ref[g, 0] = jnp.where(incl, qk[g] * decay, 0.0).astype(BF16)
        eg = jnp.exp(gc)
        gl = gc[C - 1:C, :]
        k = k_ref[sl, :]
        beta = beta_all[sl]
        vb_ref[sl, :] = (v_ref[sl, :] * beta).astype(BF16)
        kcp_ref[sl, :] = (k * beta * eg).astype(BF16)
        qd_ref[sl, :] = (q_ref[sl, :] * eg).astype(BF16)
        kd_ref[sl, :] = (k * jnp.exp(gl - gc)).astype(BF16)
        egl_ref[g] = jnp.exp(gl)


def _gdn_c_kernel(vb_ref, kcp_ref, qd_ref, kd_ref, t_ref, attn_ref, egl_ref, o_ref, s_ref, *, G):
    C = CHUNK

    @pl.when(pl.program_id(1) == 0)
    def _():
        s_ref[...] = jnp.zeros_like(s_ref)

    def body(g, carry):
        sl = pl.ds(pl.multiple_of(g * C, C), C)
        cols = [slice(h * LANES, (h + 1) * LANES) for h in range(GDN_V_HEADS)]
        w, s, sb, vnb, o = [], [], [], [], []
        for h, cs in enumerate(cols):
            both = jnp.concatenate([vb_ref[sl, cs], kcp_ref[sl, cs]], axis=1)
            w.append(_dot(t_ref[g, h].astype(BF16), both))
            s.append(s_ref[h])
            sb.append(s[h].astype(BF16))
        for h, cs in enumerate(cols):
            vnb.append((w[h][:, :LANES] - _dot(w[h][:, LANES:].astype(BF16), sb[h])).astype(BF16))
            o.append(_dot(qd_ref[sl, cs], sb[h]))
        for h, cs in enumerate(cols):
            o_ref[sl, cs] = o[h] + _dot(attn_ref[g, h], vnb[h])
            s_ref[h] = s[h] * egl_ref[g, :, cs] + _dot_tn(kd_ref[sl, cs], vnb[h])
        return carry

    lax.fori_loop(0, G, body, 0)


def gdn_chunked(q, k, v, g, beta, batch, G=4):
    T, VD = v.shape
    C = CHUNK
    seq = T // batch
    nct = T // C
    G = _pick(seq // C, (G, 2, 1))
    nh = GDN_V_HEADS
    rep = GDN_V_HEADS // GDN_QK_HEADS
    gcol = g.T.reshape(nh, T, 1)
    grow = jnp.pad(g.T.reshape(nh, nct // G, G, C), ((0, 0), (0, 0), (0, SUBLANES - G), (0, 0)))
    bcol = beta.T.reshape(nh, T, 1)
    qk_spec = pl.BlockSpec((C * G, LANES), lambda i, h: (i, h // rep))
    v_spec = pl.BlockSpec((C * G, LANES), lambda i, h: (i, h))
    col_spec = pl.BlockSpec((1, C * G, 1), lambda i, h: (h, i, 0))
    outs = pl.pallas_call(
        functools.partial(_gdn_a_kernel, G=G),
        grid=(nct // G, nh),
        in_specs=[qk_spec, qk_spec, v_spec, col_spec,
                  pl.BlockSpec((1, 1, SUBLANES, C), lambda i, h: (h, i, 0, 0)), col_spec],
        out_specs=[pl.BlockSpec((G, 1, C, C), lambda i, h: (i, h, 0, 0)),
                   pl.BlockSpec((G, 1, C, C), lambda i, h: (i, h, 0, 0)),
                   v_spec, v_spec, v_spec, v_spec,
                   pl.BlockSpec((G, 1, LANES), lambda i, h: (i, 0, h))],
        out_shape=[jax.ShapeDtypeStruct((nct, nh, C, C), F32),
                   jax.ShapeDtypeStruct((nct, nh, C, C), BF16),
                   jax.ShapeDtypeStruct((T, VD), BF16),
                   jax.ShapeDtypeStruct((T, VD), BF16),
                   jax.ShapeDtypeStruct((T, VD), BF16),
                   jax.ShapeDtypeStruct((T, VD), BF16),
                   jax.ShapeDtypeStruct((nct, 1, VD), F32)],
        compiler_params=_cparams(("parallel", "parallel")),
        name="gdn_a",
    )(q, k, v, gcol, grow, bcol)
    xm, attn, vb, kcp, qd, kd, egl = outs
    tinv = tri_inv(xm.reshape(nct * nh, C, C)).reshape(nct, nh, C, C)
    ncb = seq // C // G
    wide = pl.BlockSpec((C * G, VD), lambda b, n: (b * ncb + n, 0))
    mat = pl.BlockSpec((G, nh, C, C), lambda b, n: (b * ncb + n, 0, 0, 0))
    return pl.pallas_call(
        functools.partial(_gdn_c_kernel, G=G),
        grid=(batch, ncb),
        in_specs=[wide, wide, wide, wide, mat, mat,
                  pl.BlockSpec((G, 1, VD), lambda b, n: (b * ncb + n, 0, 0))],
        out_specs=wide,
        out_shape=jax.ShapeDtypeStruct((T, VD), F32),
        scratch_shapes=[pltpu.VMEM((nh, GDN_DK, GDN_DV), F32)],
        compiler_params=_cparams(("arbitrary", "arbitrary")),
        name="gdn_c",
    )(vb, kcp, qd, kd, tinv, attn, egl)


def _rope_pair(x, ct, st, low):
    swapped = jnp.where(low, pltpu.roll(x, LANES - RET_DK // 2, 1), pltpu.roll(x, RET_DK // 2, 1))
    return x * ct + swapped * st


def _ret_kernel(q_ref, k_ref, v_ref, ct_ref, st_ref, o_ref, s_ref, *, G):
    C = CHUNK

    @pl.when(pl.program_id(1) == 0)
    def _():
        s_ref[...] = jnp.zeros_like(s_ref)

    row, col = _chunk_masks()
    incl = col <= row
    dif = jnp.maximum(row - col, 0).astype(F32)
    lane = lax.broadcasted_iota(jnp.int32, (C, LANES), 1)
    head0 = lane < RET_DK
    low = lane % RET_DK < RET_DK // 2
    ridx = lax.broadcasted_iota(jnp.int32, (C, LANES), 0).astype(F32)

    def body(g, carry):
        sl = pl.ds(pl.multiple_of(g * C, C), C)
        ct, st = ct_ref[sl, :], st_ref[sl, :]
        lgs = [math.log(1.0 - 2.0 ** (-5.0 - h)) for h in range(RET_HEADS)]
        vss = [slice(h * RET_DV, (h + 1) * RET_DV) for h in range(RET_HEADS)]
        sc, cross, upd, vb = [], [], [], []
        for p in range(RET_HEADS // 2):
            ps = slice(p * LANES, (p + 1) * LANES)
            qp = _rope_pair(q_ref[sl, ps], ct, st, low) * RET_DK ** -0.5
            kp = _rope_pair(k_ref[sl, ps], ct, st, low)
            kb = kp.astype(BF16)
            for hh in range(2):
                h = 2 * p + hh
                msk = head0 if hh == 0 else jnp.logical_not(head0)
                q = jnp.where(msk, qp, 0.0)
                vb.append(v_ref[sl, vss[h]].astype(BF16))
                sc.append(_dot_nt(q.astype(BF16), kb))
                qdec = (q * jnp.exp((ridx + 1.0) * lgs[h])).astype(BF16)
                cross.append(_dot(qdec, s_ref[h].astype(BF16)))
                kdec = (jnp.where(msk, kp, 0.0) * jnp.exp((C - 1.0 - ridx) * lgs[h])).astype(BF16)
                upd.append(_dot_tn(kdec, vb[h]))
        for h in range(RET_HEADS):
            dmat = jnp.where(incl, jnp.exp(dif * lgs[h]), 0.0)
            o_ref[sl, vss[h]] = _dot((sc[h] * dmat).astype(BF16), vb[h]) + cross[h]
            s_ref[h] = s_ref[h] * math.exp(C * lgs[h]) + upd[h]
        return carry

    lax.fori_loop(0, G, body, 0)


def retention_chunked(p, q_blk, k_blk, v_blk, ct, st, batch, G=4):
    T = p.shape[0]
    QD, VD = RET_HEADS * RET_DK, RET_HEADS * RET_DV
    C = CHUNK
    seq = T // batch
    G = _pick(seq // C, (G, 2, 1))
    ncb = seq // C // G
    tab = pl.BlockSpec((C * G, LANES), lambda b, n: (b * ncb + n, 0))
    wide = pl.BlockSpec((C * G, VD), lambda b, n: (b * ncb + n, 0))
    return pl.pallas_call(
        functools.partial(_ret_kernel, G=G),
        grid=(batch, ncb),
        in_specs=[pl.BlockSpec((C * G, QD), lambda b, n: (b * ncb + n, q_blk)),
                  pl.BlockSpec((C * G, QD), lambda b, n: (b * ncb + n, k_blk)),
                  pl.BlockSpec((C * G, VD), lambda b, n: (b * ncb + n, v_blk)),
                  tab, tab],
        out_specs=wide,
        out_shape=jax.ShapeDtypeStruct((T, VD), F32),
        scratch_shapes=[pltpu.VMEM((RET_HEADS, LANES, RET_DV), F32)],
        compiler_params=_cparams(("arbitrary", "arbitrary")),
        name="retention",
    )(p, p, p, ct, st)


def _rope_tables(positions, d):
    inv = ROPE_THETA ** (-jnp.arange(0, d, 2, dtype=F32) / d)
    ang = positions.astype(F32).reshape(-1, 1) * inv
    cos, sin = jnp.cos(ang), jnp.sin(ang)
    reps = LANES // d
    return jnp.concatenate([cos, cos] * reps, -1), jnp.concatenate([-sin, sin] * reps, -1)


L0_MLA_W = 768
L0_LORA_W = 384
L0_N = 3 * RWKV_DIM + L0_MLA_W + L0_LORA_W
CQ_PAD = 512


def _proj_shift_kernel(x_ref, xh_ref, w_ref, mu_ref, o_ref, *, tiles_per_seq):
    m = pl.program_id(0)
    w = w_ref[...]
    p = _dot(x_ref[...].astype(BF16), w)
    ph = _dot(xh_ref[...].astype(BF16), w)
    ph = jnp.where(m % tiles_per_seq == 0, 0.0, ph)
    rows = lax.broadcasted_iota(jnp.int32, p.shape, 0)
    prev = jnp.where(rows == 0, ph[SUBLANES - 1:SUBLANES], pltpu.roll(p, 1, 0))
    o_ref[...] = p + mu_ref[...] * (prev - p)


def proj_shift(x, w, mu, seq, tm=1024, tn=1408):
    M, D = x.shape
    N = w.shape[1]
    tm = _pick(seq, (tm, 256, 128, 64, 32, 16, 8))
    tn = _pick(N, (tn, 512, 384, 256, 128))
    hb = tm // SUBLANES
    return pl.pallas_call(
        functools.partial(_proj_shift_kernel, tiles_per_seq=seq // tm),
        grid=(M // tm, N // tn),
        in_specs=[pl.BlockSpec((tm, D), lambda i, j: (i, 0)),
                  pl.BlockSpec((SUBLANES, D), lambda i, j: (jnp.maximum(i * hb - 1, 0), 0)),
                  pl.BlockSpec((D, tn), lambda i, j: (0, j)),
                  pl.BlockSpec((1, tn), lambda i, j: (0, j))],
        out_specs=pl.BlockSpec((tm, tn), lambda i, j: (i, j)),
        out_shape=jax.ShapeDtypeStruct((M, N), F32),
        compiler_params=_cparams(("parallel", "parallel")),
        name="proj_shift",
    )(x, x, w, mu.reshape(1, N))


def _mla_pre_kernel(p_ref, ct_ref, st_ref, qn_ref, kn_ref, wq_ref, wkv_ref, q_ref, k_ref, v_ref):
    p = p_ref[...]
    cq = p[:, :CQ_PAD]
    ckv = p[:, CQ_PAD:CQ_PAD + MLA_KV_RANK]
    kp = p[:, CQ_PAD + MLA_KV_RANK:]
    cqn = cq * lax.rsqrt(jnp.sum(cq * cq, -1, keepdims=True) * (1.0 / MLA_Q_RANK) + RMS_EPS) * qn_ref[...]
    ckvn = ckv * lax.rsqrt(jnp.mean(ckv * ckv, -1, keepdims=True) + RMS_EPS) * kn_ref[...]
    qa = _dot(cqn.astype(BF16), wq_ref[...])
    kva = _dot(ckvn.astype(BF16), wkv_ref[...])
    ct, st = ct_ref[...], st_ref[...]
    kr = (kp * ct + pltpu.roll(kp, MLA_ROPE, 1) * st)[:, :MLA_ROPE].astype(BF16)
    ones = jnp.ones((p.shape[0], MLA_V), BF16)
    hw = 2 * LANES
    for h in range(MLA_HEADS):
        qh = qa[:, h * hw:(h + 1) * hw]
        pe = qh[:, MLA_NOPE:]
        q_ref[0, h, :, :MLA_NOPE] = qh[:, :MLA_NOPE].astype(BF16)
        q_ref[0, h, :, MLA_NOPE:] = (pe * ct + pltpu.roll(pe, MLA_ROPE, 1) * st)[:, :MLA_ROPE].astype(BF16)
        kvh = kva[:, h * hw:(h + 1) * hw]
        k_ref[0, h, :, :MLA_NOPE] = kvh[:, :MLA_NOPE].astype(BF16)
        k_ref[0, h, :, MLA_NOPE:] = kr
        v_ref[0, h, :, :MLA_V] = kvh[:, MLA_NOPE:].astype(BF16)
        v_ref[0, h, :, MLA_V:] = ones


def mla_pre(p, mla_blk, ct, st, qn, kn, wq, wkv, batch, tm=512):
    T = p.shape[0]
    seq = T // batch
    tm = _pick(seq, (tm, 256, 128))
    nt = seq // tm
    dqk = MLA_NOPE + MLA_ROPE
    head_spec = lambda d: pl.BlockSpec((1, MLA_HEADS, tm, d), lambda i: (i // nt, 0, i % nt, 0))
    full = lambda a: pl.BlockSpec(a.shape, lambda i: (0,) * a.ndim)
    return pl.pallas_call(
        _mla_pre_kernel,
        grid=(T // tm,),
        in_specs=[pl.BlockSpec((tm, L0_MLA_W), lambda i: (i, mla_blk)),
                  pl.BlockSpec((tm, LANES), lambda i: (i, 0)),
                  pl.BlockSpec((tm, LANES), lambda i: (i, 0)),
                  full(qn), full(kn), full(wq), full(wkv)],
        out_specs=[head_spec(dqk), head_spec(dqk), head_spec(2 * MLA_V)],
        out_shape=[jax.ShapeDtypeStruct((batch, MLA_HEADS, seq, dqk), BF16),
                   jax.ShapeDtypeStruct((batch, MLA_HEADS, seq, dqk), BF16),
                   jax.ShapeDtypeStruct((batch, MLA_HEADS, seq, 2 * MLA_V), BF16)],
        compiler_params=_cparams(("parallel",)),
        name="mla_pre",
    )(p, ct, st, qn, kn, wq, wkv)


def _softplus(x):
    return jnp.maximum(x, 0.0) + jnp.log(1.0 + jnp.exp(-jnp.abs(x)))


def _group_sum(x, ones_bd, terms):
    parts = _split3(x)[:terms]
    out = _dot(parts[0], ones_bd)
    for t in parts[1:]:
        out = out + _dot(t, ones_bd)
    return out


def _rwkv_pre_kernel(k_ref, lora_ref, wl_ref, g2_ref, w0_ref, a0_ref, kk_ref, ka_ref, ones_ref,
                     lw_ref, k2_ref, av_ref, bv_ref, g_ref):
    lora = lora_ref[...]
    lane = lax.broadcasted_iota(jnp.int32, (lora.shape[0], LANES), 1)
    wa_in = jnp.where(lane < DECAY_LORA, jnp.tanh(lora[:, :LANES]), lora[:, :LANES])
    wa = _dot(wa_in.astype(BF16), wl_ref[...])
    g_ref[...] = _dot(jax.nn.sigmoid(lora[:, LANES:]).astype(BF16), g2_ref[...])
    w_log = -_softplus(-(w0_ref[...] + wa[:, :RWKV_DIM])) - 0.5
    lw_ref[...] = -jnp.exp(w_log)
    a = jax.nn.sigmoid(a0_ref[...] + wa[:, RWKV_DIM:])
    k = k_ref[...]
    kk = k * kk_ref[...]
    kkn = kk * lax.rsqrt(_group_sum(kk * kk, ones_ref[...], 2) + 1e-6)
    k2_ref[...] = k * (1.0 + (a - 1.0) * ka_ref[...])
    av_ref[...] = -kkn
    bv_ref[...] = kkn * a


def rwkv_pre(p, k_blk, lora_blk, wl, g2, w0, a0, kk, ka, ones_bd, tm=512):
    T = p.shape[0]
    tm = _pick(T, (tm, 256, 128, 64))
    row = pl.BlockSpec((tm, RWKV_DIM), lambda i: (i, 0))
    full = lambda a: pl.BlockSpec(a.shape, lambda i: (0,) * a.ndim)
    vec = lambda a: a.reshape(1, RWKV_DIM)
    args = (wl, g2, vec(w0), vec(a0), vec(kk), vec(ka), ones_bd)
    return pl.pallas_call(
        _rwkv_pre_kernel,
        grid=(T // tm,),
        in_specs=[pl.BlockSpec((tm, RWKV_DIM), lambda i: (i, k_blk)),
                  pl.BlockSpec((tm, L0_LORA_W), lambda i: (i, lora_blk))] + [full(a) for a in args],
        out_specs=[row] * 5,
        out_shape=[jax.ShapeDtypeStruct((T, RWKV_DIM), F32)] * 5,
        compiler_params=_cparams(("parallel",)),
        name="rwkv_pre",
    )(p, p, *args)


def _rwkv_post_kernel(y_ref, r_ref, k2_ref, v_ref, g_ref, rk_ref, gw_ref, gb_ref, ones_ref, o_ref):
    ones_bd = ones_ref[...]
    y = y_ref[...]
    yc = y - _group_sum(y, ones_bd, 2) * (1.0 / RWKV_HEAD)
    var = _group_sum(yc * yc, ones_bd, 1) * (1.0 / RWKV_HEAD)
    yn = yc * lax.rsqrt(var + RWKV_GN_EPS) * gw_ref[...] + gb_ref[...]
    bonus = _group_sum(r_ref[...] * k2_ref[...] * rk_ref[...], ones_bd, 2) * v_ref[...]
    o_ref[...] = ((yn + bonus) * g_ref[...]).astype(o_ref.dtype)


def rwkv_post(y, p, r_blk, v_blk, k2, g, rk, gw, gb, ones_bd, tm=512):
    T = y.shape[0]
    tm = _pick(T, (tm, 256, 128, 64))
    row = pl.BlockSpec((tm, RWKV_DIM), lambda i: (i, 0))
    full = lambda a: pl.BlockSpec(a.shape, lambda i: (0,) * a.ndim)
    vec = lambda a: a.reshape(1, RWKV_DIM)
    args = (vec(rk), vec(gw), vec(gb), ones_bd)
    return pl.pallas_call(
        _rwkv_post_kernel,
        grid=(T // tm,),
        in_specs=[row, pl.BlockSpec((tm, RWKV_DIM), lambda i: (i, r_blk)), row,
                  pl.BlockSpec((tm, RWKV_DIM), lambda i: (i, v_blk)), row] + [full(a) for a in args],
        out_specs=row,
        out_shape=jax.ShapeDtypeStruct((T, RWKV_DIM), BF16),
        compiler_params=_cparams(("parallel",)),
        name="rwkv_post",
    )(y, p, k2, p, g, *args)


def _mm2_ln_kernel(a1_ref, a2_ref, w1_ref, w2_ref, x_ref, g_ref, b_ref, o_ref):
    y = ALPHA * x_ref[...] + _dot(a1_ref[...], w1_ref[...]) + _dot(a2_ref[...], w2_ref[...])
    mu = jnp.mean(y, axis=-1, keepdims=True)
    yc = y - mu
    var = jnp.mean(yc * yc, axis=-1, keepdims=True)
    o_ref[...] = yc * lax.rsqrt(var + LN_EPS) * g_ref[...] + b_ref[...]


def mm2_ln(a1, a2, w, x, g, b, tm=512):
    M, K1 = a1.shape
    K2 = a2.shape[1]
    N = w.shape[1]
    tm = _pick(M, (tm, 256, 128, 64, 32, 16, 8))
    w = w.astype(BF16)
    row = lambda k: pl.BlockSpec((tm, k), lambda i: (i, 0))
    return pl.pallas_call(
        _mm2_ln_kernel,
        grid=(M // tm,),
        in_specs=[row(K1), row(K2),
                  pl.BlockSpec((K1, N), lambda i: (0, 0)),
                  pl.BlockSpec((K2, N), lambda i: (0, 0)),
                  row(N),
                  pl.BlockSpec((1, N), lambda i: (0, 0)),
                  pl.BlockSpec((1, N), lambda i: (0, 0))],
        out_specs=row(N),
        out_shape=jax.ShapeDtypeStruct((M, N), F32),
        compiler_params=_cparams(("parallel",)),
        name="mm2_ln",
    )(a1, a2, w[:K1], w[K1:], x, g.reshape(1, N), b.reshape(1, N))


def _half_swap(n):
    return jnp.concatenate([jnp.arange(n // 2, n), jnp.arange(0, n // 2)])


def _mla_rwkv_mixer(xb, B, S, ct, st, w_in, q_norm, w_uq, kv_norm, w_ukv, rwkv_mu, rwkv_w0, rwkv_w2,
                    rwkv_a0, rwkv_a2, rwkv_g2, rwkv_k_k, rwkv_k_a, rwkv_r_k, rwkv_gn_w, rwkv_gn_b,
                    w_out, ln_g, ln_b):
    T = B * S
    D = xb.shape[1]
    o_kv = MLA_Q_RANK
    o_pe = o_kv + MLA_KV_RANK
    o_r = o_pe + MLA_ROPE
    o_k, o_v = o_r + RWKV_DIM, o_r + 2 * RWKV_DIM
    o_xw = o_r + 3 * RWKV_DIM
    o_xg = o_xw + DECAY_LORA + AAA_LORA
    zc = lambda n: jnp.zeros((D, n), F32)
    w_pe = w_in[:, o_pe:o_r]
    w0p = jnp.concatenate([w_in[:, o_r:o_xw], w_in[:, :o_kv], zc(CQ_PAD - MLA_Q_RANK), w_in[:, o_kv:o_pe],
                           w_pe, w_pe[:, _half_swap(MLA_ROPE)], w_in[:, o_xw:], zc(L0_N - L0_MLA_W - 3 * RWKV_DIM
                                                                                   - (w_in.shape[1] - o_xw))],
                          axis=1).astype(BF16)
    mu = rwkv_mu
    mu_p = jnp.concatenate([mu[:3 * RWKV_DIM], jnp.zeros((L0_MLA_W,), F32), mu[3 * RWKV_DIM:],
                            jnp.zeros((L0_N - L0_MLA_W - mu.shape[0],), F32)])
    p = proj_shift(xb, w0p, mu_p, S)
    dqk = MLA_NOPE + MLA_ROPE
    qs = dqk ** -0.5 * math.log2(math.e)
    wq = w_uq.reshape(MLA_Q_RANK, MLA_HEADS, dqk) * qs
    wq = jnp.concatenate([wq, wq[..., MLA_NOPE:][..., _half_swap(MLA_ROPE)]], -1).reshape(MLA_Q_RANK, -1)
    wq = jnp.pad(wq, ((0, CQ_PAD - MLA_Q_RANK), (0, 0))).astype(BF16)
    qn = jnp.pad(q_norm, (0, CQ_PAD - MLA_Q_RANK)).reshape(1, CQ_PAD)
    q, k, v = mla_pre(p, 3 * RWKV_DIM // L0_MLA_W, ct, st, qn, kv_norm.reshape(1, -1), wq,
                      w_ukv.astype(BF16), B)
    o_mla = flash_attention(q, k, v)
    zl = jnp.zeros((DECAY_LORA, RWKV_DIM), F32)
    wl = jnp.concatenate([jnp.concatenate([rwkv_w2, zl], 1), jnp.concatenate([zl, rwkv_a2], 1)], 0).astype(BF16)
    g2 = jnp.pad(rwkv_g2, ((0, L0_LORA_W - LANES - GATE_LORA), (0, 0))).astype(BF16)
    hid = jnp.arange(RWKV_DIM) // RWKV_HEAD
    ones_bd = (hid[:, None] == hid[None, :]).astype(BF16)
    lw, k2, av, bv, g = rwkv_pre(p, 1, (3 * RWKV_DIM + L0_MLA_W) // L0_LORA_W, wl, g2, rwkv_w0, rwkv_a0,
                                 rwkv_k_k, rwkv_k_a, ones_bd)
    y = rwkv7_chunked(p, 0, 2 * RWKV_DIM // LANES, lw, k2, av, bv, B)
    o_rwkv = rwkv_post(y, p, 0, 2, k2, g, rwkv_r_k.reshape(-1), rwkv_gn_w, rwkv_gn_b, ones_bd)
    return mm2_ln(o_mla, o_rwkv, w_out, xb, ln_g, ln_b)


GDN_QKV_W = 2 * GDN_QK_HEADS * GDN_DK + GDN_V_HEADS * GDN_DV


def _gdn_pre_kernel(x_ref, xh_ref, ba_ref, cw_ref, al_ref, dt_ref, q_ref, k_ref, v_ref, bg_ref, *,
                    tiles_per_seq):
    m = pl.program_id(0)
    x = x_ref[...]
    xh = jnp.where(m % tiles_per_seq == 0, 0.0, xh_ref[...])
    rows = lax.broadcasted_iota(jnp.int32, x.shape, 0)
    cw = cw_ref[...]
    acc = x * cw[GDN_CONV - 1:GDN_CONV]
    for j in range(1, GDN_CONV):
        sh = pltpu.roll(x, j, 0)
        for i in range(j):
            sh = jnp.where(rows == i, xh[SUBLANES - j + i:SUBLANES - j + i + 1], sh)
        acc = acc + sh * cw[GDN_CONV - 1 - j:GDN_CONV - j]
    y = acc * jax.nn.sigmoid(acc)
    qk_w = GDN_QK_HEADS * GDN_DK
    for h in range(GDN_QK_HEADS):
        cs = slice(h * GDN_DK, (h + 1) * GDN_DK)
        yq = y[:, cs]
        q_ref[:, cs] = yq * (lax.rsqrt(jnp.sum(yq * yq, -1, keepdims=True) + 1e-6) * GDN_DK ** -0.5)
        yk = y[:, qk_w + h * GDN_DK:qk_w + (h + 1) * GDN_DK]
        k_ref[:, cs] = yk * lax.rsqrt(jnp.sum(yk * yk, -1, keepdims=True) + 1e-6)
    v_ref[...] = y[:, 2 * qk_w:]
    ba = ba_ref[...]
    lane = lax.broadcasted_iota(jnp.int32, ba.shape, 1)
    g = -jnp.exp(al_ref[...]) * _softplus(ba + dt_ref[...])
    bg_ref[...] = jnp.where(lane < GDN_V_HEADS, jax.nn.sigmoid(ba), g)


def gdn_pre(p, ba_blk, conv_w, a_log, dt_bias, seq, tm=256):
    T = p.shape[0]
    tm = _pick(seq, (tm, 128, 64))
    hb = tm // SUBLANES
    qk_w = GDN_QK_HEADS * GDN_DK
    v_w = GDN_V_HEADS * GDN_DV
    lane_pad = lambda a: jnp.pad(a, (GDN_V_HEADS, LANES - 2 * GDN_V_HEADS)).reshape(1, LANES)
    row = lambda n: pl.BlockSpec((tm, n), lambda i: (i, 0))
    full = lambda shp: pl.BlockSpec(shp, lambda i: (0, 0))
    return pl.pallas_call(
        functools.partial(_gdn_pre_kernel, tiles_per_seq=seq // tm),
        grid=(T // tm,),
        in_specs=[row(GDN_QKV_W),
                  pl.BlockSpec((SUBLANES, GDN_QKV_W), lambda i: (jnp.maximum(i * hb - 1, 0), 0)),
                  pl.BlockSpec((tm, LANES), lambda i: (i, ba_blk)),
                  full((GDN_CONV, GDN_QKV_W)), full((1, LANES)), full((1, LANES))],
        out_specs=[row(qk_w), row(qk_w), row(v_w), row(LANES)],
        out_shape=[jax.ShapeDtypeStruct((T, qk_w), F32), jax.ShapeDtypeStruct((T, qk_w), F32),
                   jax.ShapeDtypeStruct((T, v_w), F32), jax.ShapeDtypeStruct((T, LANES), F32)],
        compiler_params=_cparams(("parallel",)),
        name="gdn_pre",
    )(p, p, p, conv_w, lane_pad(a_log), lane_pad(dt_bias))


def _l1_post_kernel(og_ref, z_ref, or_ref, rg_ref, gn_ref, rw_ref, rb_ref, o_ref):
    v_w = GDN_V_HEADS * GDN_DV
    for h in range(GDN_V_HEADS):
        cs = slice(h * LANES, (h + 1) * LANES)
        o = og_ref[:, cs]
        z = z_ref[:, cs]
        on = o * lax.rsqrt(jnp.mean(o * o, -1, keepdims=True) + RMS_EPS) * gn_ref[...]
        o_ref[:, cs] = (on * (z * jax.nn.sigmoid(z))).astype(o_ref.dtype)
        r = or_ref[:, cs]
        rc = r - jnp.mean(r, -1, keepdims=True)
        rn = rc * lax.rsqrt(jnp.mean(rc * rc, -1, keepdims=True) + LN_EPS) * rw_ref[:, cs] + rb_ref[:, cs]
        g = rg_ref[:, cs]
        o_ref[:, v_w + h * LANES:v_w + (h + 1) * LANES] = (rn * (g * jax.nn.sigmoid(g))).astype(o_ref.dtype)


def l1_post(o_gdn, o_ret, p, z_blk, rg_blk, gdn_norm, ret_gn_w, ret_gn_b, tm=512):
    T, W = o_gdn.shape
    tm = _pick(T, (tm, 256, 128, 64))
    row = pl.BlockSpec((tm, W), lambda i: (i, 0))
    return pl.pallas_call(
        _l1_post_kernel,
        grid=(T // tm,),
        in_specs=[row, pl.BlockSpec((tm, W), lambda i: (i, z_blk)), row,
                  pl.BlockSpec((tm, W), lambda i: (i, rg_blk)),
                  pl.BlockSpec((1, LANES), lambda i: (0, 0)),
                  pl.BlockSpec((1, W), lambda i: (0, 0)),
                  pl.BlockSpec((1, W), lambda i: (0, 0))],
        out_specs=pl.BlockSpec((tm, 2 * W), lambda i: (i, 0)),
        out_shape=jax.ShapeDtypeStruct((T, 2 * W), BF16),
        compiler_params=_cparams(("parallel",)),
        name="l1_post",
    )(o_gdn, p, o_ret, p, gdn_norm.reshape(1, LANES), ret_gn_w.reshape(1, W), ret_gn_b.reshape(1, W))


def _gdn_retention_mixer(xb, B, S, ct, st, w_in, gdn_conv_w, gdn_A_log, gdn_dt_bias, gdn_norm,
                         ret_gn_w, ret_gn_b, w_out, ln_g, ln_b):
    D = xb.shape[1]
    v_w = GDN_V_HEADS * GDN_DV
    n_gdn = GDN_QKV_W + v_w
    w1p = jnp.concatenate([w_in[:, :n_gdn], w_in[:, n_gdn + 2 * GDN_V_HEADS:],
                           w_in[:, n_gdn:n_gdn + 2 * GDN_V_HEADS],
                           jnp.zeros((D, LANES - 2 * GDN_V_HEADS), F32)], axis=1)
    p = mm(xb, w1p, tm=1024, tn=896)
    q, k, v, bg = gdn_pre(p, (w1p.shape[1] - LANES) // LANES, gdn_conv_w, gdn_A_log, gdn_dt_bias, S)
    o_gdn = gdn_chunked(q, k, v, bg[:, GDN_V_HEADS:2 * GDN_V_HEADS], bg[:, :GDN_V_HEADS], B)
    qd = RET_HEADS * RET_DK
    o_ret = retention_chunked(p, n_gdn // qd, n_gdn // qd + 1, (n_gdn + 2 * qd) // v_w, ct, st, B)
    m1 = l1_post(o_gdn, o_ret, p, GDN_QKV_W // v_w, (n_gdn + 2 * qd + v_w) // v_w, gdn_norm, ret_gn_w, ret_gn_b)
    return mm_ln(m1, w_out, xb, ln_g, ln_b)


def _conv_ffn_ln(x, S, w_gate, w_val, conv_w, conv_b, w_down, ln_g, ln_b):
    h = ffn1(x, w_gate, w_val, conv_w, conv_b, S)
    return mm_ln(h, w_down, x, ln_g, ln_b)


def kernel(x, positions, l0_w_in, l0_q_norm, l0_w_uq, l0_kv_norm, l0_w_ukv, l0_rwkv_mu, l0_rwkv_w0, l0_rwkv_w2, l0_rwkv_a0, l0_rwkv_a2, l0_rwkv_g2, l0_rwkv_k_k, l0_rwkv_k_a, l0_rwkv_r_k, l0_rwkv_gn_w, l0_rwkv_gn_b, l0_w_out, l0_ln1_g, l0_ln1_b, l0_ffn_w_gate, l0_ffn_w_val, l0_ffn_conv_w, l0_ffn_conv_b, l0_ffn_w_down, l0_ln2_g, l0_ln2_b, l1_w_in, l1_gdn_conv_w, l1_gdn_A_log, l1_gdn_dt_bias, l1_gdn_norm, l1_ret_gn_w, l1_ret_gn_b, l1_w_out, l1_ln1_g, l1_ln1_b, l1_ffn_w_gate, l1_ffn_w_val, l1_ffn_conv_w, l1_ffn_conv_b, l1_ffn_w_down, l1_ln2_g, l1_ln2_b):
    B, S, D = x.shape
    T = B * S
    xb = x.reshape(T, D)
    assert MLA_ROPE == RET_DK
    ct, st = _rope_tables(positions, MLA_ROPE)
    xb = _mla_rwkv_mixer(xb, B, S, ct, st, l0_w_in, l0_q_norm, l0_w_uq, l0_kv_norm, l0_w_ukv, l0_rwkv_mu,
                         l0_rwkv_w0, l0_rwkv_w2, l0_rwkv_a0, l0_rwkv_a2, l0_rwkv_g2, l0_rwkv_k_k, l0_rwkv_k_a,
                         l0_rwkv_r_k, l0_rwkv_gn_w, l0_rwkv_gn_b, l0_w_out, l0_ln1_g, l0_ln1_b)
    xb = _conv_ffn_ln(xb, S, l0_ffn_w_gate, l0_ffn_w_val, l0_ffn_conv_w, l0_ffn_conv_b, l0_ffn_w_down,
                      l0_ln2_g, l0_ln2_b)
    xb = _gdn_retention_mixer(xb, B, S, ct, st, l1_w_in, l1_gdn_conv_w, l1_gdn_A_log, l1_gdn_dt_bias,
                              l1_gdn_norm, l1_ret_gn_w, l1_ret_gn_b, l1_w_out, l1_ln1_g, l1_ln1_b)
    xb = _conv_ffn_ln(xb, S, l1_ffn_w_gate, l1_ffn_w_val, l1_ffn_conv_w, l1_ffn_conv_b, l1_ffn_w_down,
                      l1_ln2_g, l1_ln2_b)
    return xb.reshape(B, S, D)
```

```python
import functools
import math

import jax
import jax.numpy as jnp
from jax import lax
from jax.experimental import pallas as pl
from jax.experimental.pallas import tpu as pltpu

F32 = jnp.float32
BF16 = jnp.bfloat16

DEPTH = 2
ALPHA = (2 * DEPTH) ** 0.25
LN_EPS = 1e-5
RMS_EPS = 1e-6
ROPE_THETA = 10000.0

MLA_HEADS = 8
MLA_Q_RANK = 448
MLA_KV_RANK = 128
MLA_NOPE = 128
MLA_ROPE = 64
MLA_V = 128
RWKV_HEADS = 16
RWKV_HEAD = 64
RWKV_DIM = RWKV_HEADS * RWKV_HEAD
DECAY_LORA = 64
AAA_LORA = 64
GATE_LORA = 160
RWKV_GN_EPS = 64e-5
GDN_QK_HEADS = 4
GDN_V_HEADS = 8
GDN_DK = 128
GDN_DV = 128
GDN_CONV = 4
RET_HEADS = 8
RET_DK = 64
RET_DV = 128
FFN_CONV = 3

CHUNK = 64
LANES = 128
SUBLANES = 8
VMEM_LIMIT = 56 * 1024 * 1024


def _cparams(sem):
    return pltpu.CompilerParams(dimension_semantics=sem, vmem_limit_bytes=VMEM_LIMIT)


def _dot(a, b):
    return jnp.dot(a, b, preferred_element_type=F32)


def _dot_nt(a, b):
    return lax.dot_general(a, b, (((1,), (1,)), ((), ())), preferred_element_type=F32)


def _dot_tn(a, b):
    return lax.dot_general(a, b, (((0,), (0,)), ((), ())), preferred_element_type=F32)


def _split3(x):
    hi = x.astype(BF16)
    r1 = x - hi.astype(F32)
    mid = r1.astype(BF16)
    lo = (r1 - mid.astype(F32)).astype(BF16)
    return hi, mid, lo


def _dot_exact_lhs(m_bf16, x):
    hi, mid, lo = _split3(x)
    return _dot(m_bf16, hi) + _dot(m_bf16, mid) + _dot(m_bf16, lo)


def _mm_kernel(a_ref, w_ref, o_ref, *scratch, nk):
    if nk == 1:
        o_ref[...] = _dot(a_ref[...].astype(BF16), w_ref[...]).astype(o_ref.dtype)
        return
    acc_ref, = scratch
    k = pl.program_id(2)

    @pl.when(k == 0)
    def _():
        acc_ref[...] = jnp.zeros_like(acc_ref)

    acc_ref[...] += _dot(a_ref[...].astype(BF16), w_ref[...])

    @pl.when(k == nk - 1)
    def _():
        o_ref[...] = acc_ref[...].astype(o_ref.dtype)


def _pick(n, pref):
    for t in pref:
        if n % t == 0:
            return t
    return n


def mm(a, w, out_dtype=F32, tm=512, tn=None, tk=None):
    M, K = a.shape
    N = w.shape[1]
    tm = _pick(M, (tm, 256, 128, 64, 32, 16, 8))
    tn = tn or _pick(N, (512, 384, 256, 128))
    tk = tk or K
    nk = K // tk
    w = w.astype(BF16)
    scratch = [] if nk == 1 else [pltpu.VMEM((tm, tn), F32)]
    return pl.pallas_call(
        functools.partial(_mm_kernel, nk=nk),
        grid=(M // tm, N // tn, nk),
        in_specs=[pl.BlockSpec((tm, tk), lambda i, j, k: (i, k)),
                  pl.BlockSpec((tk, tn), lambda i, j, k: (k, j))],
        out_specs=pl.BlockSpec((tm, tn), lambda i, j, k: (i, j)),
        out_shape=jax.ShapeDtypeStruct((M, N), out_dtype),
        scratch_shapes=scratch,
        compiler_params=_cparams(("parallel", "parallel", "arbitrary")),
        name="mm",
    )(a, w)


def _mm_ln_kernel(a_ref, w_ref, x_ref, g_ref, b_ref, o_ref, acc_ref, *, nk):
    k = pl.program_id(1)

    @pl.when(k == 0)
    def _():
        acc_ref[...] = jnp.zeros_like(acc_ref)

    acc_ref[...] += _dot(a_ref[...].astype(BF16), w_ref[...])

    @pl.when(k == nk - 1)
    def _():
        y = ALPHA * x_ref[...] + acc_ref[...]
        mu = jnp.mean(y, axis=-1, keepdims=True)
        yc = y - mu
        var = jnp.mean(yc * yc, axis=-1, keepdims=True)
        o_ref[...] = yc * lax.rsqrt(var + LN_EPS) * g_ref[...] + b_ref[...]


def _mm_ln_full_kernel(a_ref, w_ref, x_ref, g_ref, b_ref, o_ref):
    y = ALPHA * x_ref[...] + _dot(a_ref[...].astype(BF16), w_ref[...])
    mu = jnp.mean(y, axis=-1, keepdims=True)
    yc = y - mu
    var = jnp.mean(yc * yc, axis=-1, keepdims=True)
    o_ref[...] = yc * lax.rsqrt(var + LN_EPS) * g_ref[...] + b_ref[...]


def mm_ln_full(a, w, x, g, b, tm=256):
    M, K = a.shape
    N = w.shape[1]
    tm = _pick(M, (tm, 128, 64, 32, 16, 8))
    const = lambda shp: pl.BlockSpec(shp, lambda i: (0, 0), pipeline_mode=pl.Buffered(1))
    return pl.pallas_call(
        _mm_ln_full_kernel,
        grid=(M // tm,),
        in_specs=[pl.BlockSpec((tm, K), lambda i: (i, 0)),
                  const((K, N)),
                  pl.BlockSpec((tm, N), lambda i: (i, 0)),
                  const((1, N)), const((1, N))],
        out_specs=pl.BlockSpec((tm, N), lambda i: (i, 0)),
        out_shape=jax.ShapeDtypeStruct((M, N), F32),
        compiler_params=_cparams(("parallel",)),
        name="mm_ln_full",
    )(a, w.astype(BF16), x, g.reshape(1, N), b.reshape(1, N))


def mm_ln(a, w, x, g, b, tm=512, tk=None):
    M, K = a.shape
    N = w.shape[1]
    tm = _pick(M, (tm, 256, 128, 64, 32, 16, 8))
    tk = tk or _pick(K, (1024, 1408, 512, 256, 128))
    nk = K // tk
    return pl.pallas_call(
        functools.partial(_mm_ln_kernel, nk=nk),
        grid=(M // tm, nk),
        in_specs=[pl.BlockSpec((tm, tk), lambda i, k: (i, k)),
                  pl.BlockSpec((tk, N), lambda i, k: (k, 0)),
                  pl.BlockSpec((tm, N), lambda i, k: (i, 0)),
                  pl.BlockSpec((1, N), lambda i, k: (0, 0)),
                  pl.BlockSpec((1, N), lambda i, k: (0, 0))],
        out_specs=pl.BlockSpec((tm, N), lambda i, k: (i, 0)),
        out_shape=jax.ShapeDtypeStruct((M, N), F32),
        scratch_shapes=[pltpu.VMEM((tm, N), F32)],
        compiler_params=_cparams(("parallel", "arbitrary")),
        name="mm_ln",
    )(a, w.astype(BF16), x, g.reshape(1, N), b.reshape(1, N))


def _ffn1_kernel(x_ref, xh_ref, wg_ref, wv_ref, cw_ref, cb_ref, o_ref, *, tiles_per_seq):
    m = pl.program_id(0)
    x = x_ref[...].astype(BF16)
    g = _dot(x, wg_ref[...])
    gh = _dot(xh_ref[...].astype(BF16), wg_ref[...])
    gh = jnp.where(m % tiles_per_seq == 0, 0.0, gh)
    rows = lax.broadcasted_iota(jnp.int32, g.shape, 0)
    g1 = jnp.where(rows == 0, gh[SUBLANES - 1:SUBLANES], pltpu.roll(g, 1, 0))
    g2 = jnp.where(rows == 0, gh[SUBLANES - 2:SUBLANES - 1],
                   jnp.where(rows == 1, gh[SUBLANES - 1:SUBLANES], pltpu.roll(g, 2, 0)))
    cw = cw_ref[...]
    h = g * cw[2:3] + g1 * cw[1:2] + g2 * cw[0:1] + cb_ref[...]
    v = _dot(x, wv_ref[...])
    o_ref[...] = (h * jax.nn.sigmoid(h) * v).astype(o_ref.dtype)


def ffn1(x, w_gate, w_val, conv_w, conv_b, seq, tm=1024, tf=512):
    M, D = x.shape
    Fd = w_gate.shape[1]
    tm = _pick(seq, (tm, 256, 128, 64, 32, 16, 8))
    tf = _pick(Fd, (tf, 256, 128))
    hb = tm // SUBLANES
    return pl.pallas_call(
        functools.partial(_ffn1_kernel, tiles_per_seq=seq // tm),
        grid=(M // tm, Fd // tf),
        in_specs=[pl.BlockSpec((tm, D), lambda i, j: (i, 0)),
                  pl.BlockSpec((SUBLANES, D), lambda i, j: (jnp.maximum(i * hb - 1, 0), 0)),
                  pl.BlockSpec((D, tf), lambda i, j: (0, j)),
                  pl.BlockSpec((D, tf), lambda i, j: (0, j)),
                  pl.BlockSpec((FFN_CONV, tf), lambda i, j: (0, j)),
                  pl.BlockSpec((1, tf), lambda i, j: (0, j))],
        out_specs=pl.BlockSpec((tm, tf), lambda i, j: (i, j)),
        out_shape=jax.ShapeDtypeStruct((M, Fd), BF16),
        compiler_params=_cparams(("parallel", "parallel")),
        name="ffn1",
    )(x, x, w_gate.astype(BF16), w_val.astype(BF16), conv_w, conv_b.reshape(1, Fd))


def _flash_kernel(q_ref, k_ref, v_ref, o_ref, m_ref, acc_ref, s_ref, *, tq, tk, ts, dv):
    qi = pl.program_id(2)
    nsub = tq // ts
    ndiag = tq // tk
    nfull = qi * ndiag
    m_ref[...] = jnp.full_like(m_ref, -jnp.inf)
    acc_ref[...] = jnp.zeros_like(acc_ref)

    def above_diagonal(r, d):
        return (r + 1) * ts <= d * tk

    def scores(r, blk):
        k = k_ref[0, 0, pl.ds(pl.multiple_of(blk * tk, tk), tk), :]
        return _dot_nt(q_ref[0, 0, r * ts:(r + 1) * ts, :], k)

    def consume(blk, diag, nxt_diag):
        v = v_ref[0, 0, pl.ds(pl.multiple_of(blk * tk, tk), tk), :]
        for r in range(nsub):
            rs = slice(r * ts, (r + 1) * ts)
            live = diag is None or not above_diagonal(r, diag)
            if live:
                s = s_ref[rs, :]
            if nxt_diag is None or (nxt_diag >= 0 and not above_diagonal(r, nxt_diag)):
                s_ref[rs, :] = scores(r, blk + 1)
            if not live:
                continue
            if diag is not None and r * ts < (diag + 1) * tk - 1:
                rowi = lax.broadcasted_iota(jnp.int32, s.shape, 0) + r * ts
                coli = lax.broadcasted_iota(jnp.int32, s.shape, 1) + diag * tk
                s = jnp.where(coli <= rowi, s, -jnp.inf)
            m_old = m_ref[rs, :]
            m_new = jnp.maximum(m_old, jnp.max(s, axis=-1, keepdims=True))
            p = jnp.exp2(s - m_new).astype(BF16)
            acc_ref[rs, :] = jnp.exp2(m_old - m_new) * acc_ref[rs, :] + _dot(p, v)
            m_ref[rs, :] = m_new

    for r in range(nsub):
        s_ref[r * ts:(r + 1) * ts, :] = scores(r, 0)

    def body(j, carry):
        consume(2 * j, None, None)
        consume(2 * j + 1, None, None)
        return carry

    lax.fori_loop(0, nfull // 2, body, 0)

    @pl.when(nfull % 2 == 1)
    def _():
        consume(nfull - 1, None, None)

    for d in range(ndiag):
        consume(nfull + d, d, d + 1 if d + 1 < ndiag else -1)
    acc = acc_ref[...]
    o_ref[...] = (acc[:, :dv] / acc[:, dv:]).astype(o_ref.dtype)


def flash_attention(q, k, v, tq=1024, tk=1024, ts=512):
    B, H, S, dq = q.shape
    dv = v.shape[-1] // 2
    tq = _pick(S, (tq, 512, 256, 128))
    tk = _pick(tq, (tk, 256, 128))
    nq = S // tq
    return pl.pallas_call(
        functools.partial(_flash_kernel, tq=tq, tk=tk, ts=_pick(tq, (ts, 128)), dv=dv),
        grid=(B, H, nq),
        in_specs=[pl.BlockSpec((1, 1, tq, dq), lambda b, h, i: (b, h, i, 0)),
                  pl.BlockSpec((1, 1, S, dq), lambda b, h, i: (b, h, 0, 0)),
                  pl.BlockSpec((1, 1, S, 2 * dv), lambda b, h, i: (b, h, 0, 0))],
        out_specs=pl.BlockSpec((tq, dv), lambda b, h, i: (b * nq + i, h)),
        out_shape=jax.ShapeDtypeStruct((B * S, H * dv), BF16),
        scratch_shapes=[pltpu.VMEM((tq, 1), F32), pltpu.VMEM((tq, 2 * dv), F32), pltpu.VMEM((tq, tk), F32)],
        compiler_params=_cparams(("parallel", "parallel", "arbitrary")),
        name="flash",
    )(q, k, v)


def _tri_inv_kernel(a_ref, t_ref):
    C = a_ref.shape[1]
    kk = lax.broadcasted_iota(jnp.int32, (SUBLANES, LANES), 0)
    zero = jnp.zeros((SUBLANES, LANES), F32)
    for i in range(C):
        ng = i // SUBLANES + 1
        acc = [jnp.where(kk == i - g * SUBLANES, 1.0, 0.0).astype(F32) for g in range(ng)]
        for j in range(i):
            a_ij = a_ref[0, i, j:j + 1, :]
            for g in range(j // SUBLANES + 1):
                acc[g] = acc[g] + a_ij * t_ref[0, j, g * SUBLANES:(g + 1) * SUBLANES, :]
        for g in range(C // SUBLANES):
            t_ref[0, i, g * SUBLANES:(g + 1) * SUBLANES, :] = acc[g] if g < ng else zero


def tri_inv(a):
    n_in, C, _ = a.shape
    ng = -(-n_in // LANES)
    N = ng * LANES
    a = jnp.pad(a, ((0, N - n_in), (0, 0), (0, 0)))
    al = a.reshape(ng, LANES, C * C).transpose(0, 2, 1).reshape(ng, C, C, LANES)
    tl = pl.pallas_call(
        _tri_inv_kernel,
        grid=(ng,),
        in_specs=[pl.BlockSpec((1, C, C, LANES), lambda i: (i, 0, 0, 0))],
        out_specs=pl.BlockSpec((1, C, C, LANES), lambda i: (i, 0, 0, 0)),
        out_shape=jax.ShapeDtypeStruct((ng, C, C, LANES), F32),
        compiler_params=_cparams(("parallel",)),
        name="tri_inv",
    )(al)
    return tl.reshape(ng, C * C, LANES).transpose(0, 2, 1).reshape(N, C, C)[:n_in]


def tri_inv_pairs(a):
    n, m, C, _ = a.shape
    half = LANES // 2
    tiles = n * m
    ng = -(-tiles // half)
    a = jnp.pad(a.reshape(tiles, C, 2, C), ((0, ng * half - tiles), (0, 0), (0, 0), (0, 0)))
    al = a.reshape(ng, half, C, 2, C).transpose(0, 2, 4, 1, 3).reshape(ng, C, C, LANES)
    tl = pl.pallas_call(
        _tri_inv_kernel,
        grid=(ng,),
        in_specs=[pl.BlockSpec((1, C, C, LANES), lambda i: (i, 0, 0, 0))],
        out_specs=pl.BlockSpec((1, C, C, LANES), lambda i: (i, 0, 0, 0)),
        out_shape=jax.ShapeDtypeStruct((ng, C, C, LANES), F32),
        compiler_params=_cparams(("parallel",)),
        name="tri_inv",
    )(al)
    t = tl.reshape(ng, C, C, half, 2).transpose(0, 3, 1, 4, 2).reshape(ng * half, C, 2 * C)
    return t[:tiles].reshape(n, m, C, 2 * C)


def _chunk_tri(C, G):
    idx = jnp.arange(C * G)
    return ((idx[None, :] <= idx[:, None]) & (idx[None, :] // C == idx[:, None] // C)).astype(BF16)


def _chunk_masks():
    C = CHUNK
    row = lax.broadcasted_iota(jnp.int32, (C, C), 0)
    col = lax.broadcasted_iota(jnp.int32, (C, C), 1)
    return row, col


def _rwkv_a_kernel(r_ref, lw_ref, k_ref, v_ref, av_ref, bv_ref, tri_ref,
                   aab_ref, mrb_ref, at_ref, akv_ref, rt_ref, bh_ref, yk_ref, kv_ref, egl_ref, *, G):
    C = CHUNK
    rowp = lax.broadcasted_iota(jnp.int32, (C, LANES), 0)
    lane = lax.broadcasted_iota(jnp.int32, (C, LANES), 1)
    strict = lane % C < rowp
    incl = lane % C <= rowp
    head0 = lane < RWKV_HEAD
    bd = (lax.broadcasted_iota(jnp.int32, (LANES, LANES), 0) // RWKV_HEAD ==
          lax.broadcasted_iota(jnp.int32, (LANES, LANES), 1) // RWKV_HEAD)

    def by_head(x):
        return jnp.concatenate([jnp.where(head0, x, 0.0), jnp.where(head0, 0.0, x)], axis=0).astype(BF16)

    lw_all = lw_ref[...]
    gc_all = _dot_exact_lhs(tri_ref[...], lw_all)
    sls = [slice(g * C, (g + 1) * C) for g in range(G)]
    at, rt, vb, egl, gl, xb, xk = [], [], [], [], [], [], []
    for sl in sls:
        gc = gc_all[sl]
        gl.append(gc[C - 1:C, :])
        ieg = jnp.exp(-gc)
        egl.append(jnp.exp(gl[-1] - gc))
        at.append(av_ref[sl, :] * jnp.exp(gc - lw_all[sl]))
        rt.append(r_ref[sl, :] * jnp.exp(gc))
        vb.append(by_head(v_ref[sl, :]))
        lhs = jnp.concatenate([at[-1], rt[-1]], axis=0).astype(BF16)
        xb.append(_dot_nt(lhs, by_head(bv_ref[sl, :] * ieg)))
        xk.append(_dot_nt(lhs, by_head(k_ref[sl, :] * ieg)))
    for g, sl in enumerate(sls):
        aab_ref[g, 0] = jnp.where(strict, xb[g][:C], 0.0)
        mrb_ref[g, 0] = jnp.where(incl, xb[g][C:], 0.0).astype(BF16)
        akv_ref[sl, :] = _dot(jnp.where(strict, xk[g][:C], 0.0).astype(BF16), vb[g]).astype(BF16)
        yk_ref[sl, :] = _dot(jnp.where(incl, xk[g][C:], 0.0).astype(BF16), vb[g])
    for g, sl in enumerate(sls):
        at_ref[sl, :] = at[g].astype(BF16)
        rt_ref[sl, :] = rt[g].astype(BF16)
        bh_ref[sl, :] = (bv_ref[sl, :] * egl[g]).astype(BF16)
        kv_ref[g, 0] = jnp.where(bd, _dot_tn(v_ref[sl, :].astype(BF16), (k_ref[sl, :] * egl[g]).astype(BF16)), 0.0)
        egl_ref[g] = jnp.exp(gl[g])


def _rwkv_c_kernel(at_ref, akv_ref, rt_ref, bh_ref, yk_ref, t_ref, mrb_ref, kv_ref, egl_ref,
                   y_ref, s_ref, *, G):
    C = CHUNK
    npair = RWKV_HEADS // 2

    @pl.when(pl.program_id(1) == 0)
    def _():
        s_ref[...] = jnp.zeros_like(s_ref)

    lane = lax.broadcasted_iota(jnp.int32, (C, LANES), 1)
    head0 = lane < RWKV_HEAD
    head0w = jnp.concatenate([head0, head0], axis=1)
    bd = (lax.broadcasted_iota(jnp.int32, (LANES, LANES), 0) // RWKV_HEAD ==
          lax.broadcasted_iota(jnp.int32, (LANES, LANES), 1) // RWKV_HEAD)
    zero = jnp.zeros((), BF16)

    def by_head(x, m):
        return jnp.concatenate([jnp.where(m, x, zero), jnp.where(m, zero, x)], axis=0)

    def body(g, carry):
        sl = pl.ds(pl.multiple_of(g * C, C), C)
        cols = [slice(p * LANES, (p + 1) * LANES) for p in range(npair)]
        w, s, sb, u, ub, y = [], [], [], [], [], []
        for p, cs in enumerate(cols):
            both = jnp.concatenate([at_ref[sl, cs], akv_ref[sl, cs]], axis=1)
            w.append(_dot(t_ref[g, p].astype(BF16), by_head(both, head0w)))
            s.append(s_ref[p])
            sb.append(s[p].astype(BF16))
        for p, cs in enumerate(cols):
            u.append(_dot_nt(w[p][:, :LANES].astype(BF16), sb[p]) + w[p][:, LANES:])
            ub.append(u[p].astype(BF16))
            y.append(_dot_nt(rt_ref[sl, cs], sb[p]) + yk_ref[sl, cs])
        for p, cs in enumerate(cols):
            y_ref[sl, cs] = y[p] + _dot(mrb_ref[g, p], by_head(ub[p], head0))
            s_ref[p] = (s[p] * egl_ref[g, :, cs] + jnp.where(bd, _dot_tn(ub[p], bh_ref[sl, cs]), 0.0)
                        + kv_ref[g, p])
        return carry

    lax.fori_loop(0, G, body, 0)


def rwkv7_chunked(p, r_blk, v_blk, lw, k, av, bv, batch, G=4, Ga=4):
    T, HD = lw.shape
    C = CHUNK
    seq = T // batch
    nct = T // C
    G = _pick(seq // C, (G, 2, 1))
    Ga = _pick(seq // C, (Ga, 4, 2, 1))
    npair = HD // LANES
    row_spec = pl.BlockSpec((C * Ga, LANES), lambda i, p: (i, p))
    pair_mat = pl.BlockSpec((Ga, 1, C, LANES), lambda i, p: (i, p, 0, 0))
    outs = pl.pallas_call(
        functools.partial(_rwkv_a_kernel, G=Ga),
        grid=(nct // Ga, npair),
        in_specs=[pl.BlockSpec((C * Ga, LANES), lambda i, p: (i, r_blk + p)), row_spec, row_spec,
                  pl.BlockSpec((C * Ga, LANES), lambda i, p: (i, v_blk + p)), row_spec, row_spec,
                  pl.BlockSpec((C * Ga, C * Ga), lambda i, p: (0, 0))],
        out_specs=[pair_mat, pair_mat,
                   row_spec, row_spec, row_spec, row_spec, row_spec,
                   pl.BlockSpec((Ga, 1, LANES, LANES), lambda i, p: (i, p, 0, 0)),
                   pl.BlockSpec((Ga, 1, LANES), lambda i, p: (i, 0, p))],
        out_shape=[jax.ShapeDtypeStruct((nct, npair, C, LANES), F32),
                   jax.ShapeDtypeStruct((nct, npair, C, LANES), BF16),
                   jax.ShapeDtypeStruct((T, HD), BF16),
                   jax.ShapeDtypeStruct((T, HD), BF16),
                   jax.ShapeDtypeStruct((T, HD), BF16),
                   jax.ShapeDtypeStruct((T, HD), BF16),
                   jax.ShapeDtypeStruct((T, HD), F32),
                   jax.ShapeDtypeStruct((nct, npair, LANES, LANES), F32),
                   jax.ShapeDtypeStruct((nct, 1, HD), F32)],
        compiler_params=_cparams(("parallel", "parallel")),
        name="rwkv_a",
    )(p, lw, k, p, av, bv, _chunk_tri(C, Ga))
    aab, mrb, at, akv, rt, bh, yk, kv, egl = outs
    tinv = tri_inv_pairs(aab)
    ncb = seq // C // G
    wide = pl.BlockSpec((C * G, HD), lambda b, n: (b * ncb + n, 0))
    return pl.pallas_call(
        functools.partial(_rwkv_c_kernel, G=G),
        grid=(batch, ncb),
        in_specs=[wide, wide, wide, wide, wide,
                  pl.BlockSpec((G, npair, C, LANES), lambda b, n: (b * ncb + n, 0, 0, 0)),
                  pl.BlockSpec((G, npair, C, LANES), lambda b, n: (b * ncb + n, 0, 0, 0)),
                  pl.BlockSpec((G, npair, LANES, LANES), lambda b, n: (b * ncb + n, 0, 0, 0)),
                  pl.BlockSpec((G, 1, HD), lambda b, n: (b * ncb + n, 0, 0))],
        out_specs=wide,
        out_shape=jax.ShapeDtypeStruct((T, HD), F32),
        scratch_shapes=[pltpu.VMEM((npair, LANES, LANES), F32)],
        compiler_params=_cparams(("arbitrary", "arbitrary")),
        name="rwkv_c",
    )(at, akv, rt, bh, yk, tinv, mrb, kv, egl)


def _gdn_a_kernel(q_ref, k_ref, v_ref, gcol_ref, grow_ref, bcol_ref, tri_ref,
                  xm_ref, attn_ref, vb_ref, kcp_ref, qd_ref, kd_ref, egl_ref, *, G):
    C = CHUNK
    row, col = _chunk_masks()
    strict = col < row
    incl = col <= row
    triu = jnp.where(row <= col, 1.0, 0.0).astype(BF16)
    n = C * G
    gc_all = _dot_exact_lhs(tri_ref[...], jnp.broadcast_to(gcol_ref[0], (n, LANES)))
    hi, mid, lo = _split3(grow_ref[0, 0])
    gcr_all = _dot(hi, triu) + _dot(mid, triu) + _dot(lo, triu)
    beta_all = jnp.broadcast_to(bcol_ref[0], (n, LANES))
    sls = [slice(g * C, (g + 1) * C) for g in range(G)]
    kk, qk = [], []
    for sl in sls:
        k = k_ref[sl, :]
        kbf = k.astype(BF16)
        kk.append(_dot_nt((k * beta_all[sl]).astype(BF16), kbf))
        qk.append(_dot_nt(q_ref[sl, :].astype(BF16), kbf))
    for g, sl in enumerate(sls):
        gc = gc_all[sl]
        diff = gc[:, :C] - gcr_all[g:g + 1]
        decay = jnp.where(incl, jnp.exp(jnp.where(incl, diff, 0.0)), 0.0)
        xm_ref[g, 0] = jnp.where(strict, -kk[g] * decay, 0.0)
        attn_ref[g, 0] = jnp.where(incl, qk[g] * decay, 0.0).astype(BF16)
        eg = jnp.exp(gc)
        gl = gc[C - 1:C, :]
        k = k_ref[sl, :]
        beta = beta_all[sl]
        vb_ref[sl, :] = (v_ref[sl, :] * beta).astype(BF16)
        kcp_ref[sl, :] = (k * beta * eg).astype(BF16)
        qd_ref[sl, :] = (q_ref[sl, :] * eg).astype(BF16)
        kd_ref[sl, :] = (k * jnp.exp(gl - gc)).astype(BF16)
        egl_ref[g] = jnp.exp(gl)


def _gdn_c_kernel(vb_ref, kcp_ref, qd_ref, kd_ref, t_ref, attn_ref, egl_ref, o_ref, s_ref, *, G):
    C = CHUNK

    @pl.when(pl.program_id(1) == 0)
    def _():
        s_ref[...] = jnp.zeros_like(s_ref)

    def body(g, carry):
        sl = pl.ds(pl.multiple_of(g * C, C), C)
        cols = [slice(h * LANES, (h + 1) * LANES) for h in range(GDN_V_HEADS)]
        w, s, sb, vnb, o = [], [], [], [], []
        for h, cs in enumerate(cols):
            both = jnp.concatenate([vb_ref[sl, cs], kcp_ref[sl, cs]], axis=1)
            w.append(_dot(t_ref[g, h].astype(BF16), both))
            s.append(s_ref[h])
            sb.append(s[h].astype(BF16))
        for h, cs in enumerate(cols):
            vnb.append((w[h][:, :LANES] - _dot(w[h][:, LANES:].astype(BF16), sb[h])).astype(BF16))
            o.append(_dot(qd_ref[sl, cs], sb[h]))
        for h, cs in enumerate(cols):
            o_ref[sl, cs] = o[h] + _dot(attn_ref[g, h], vnb[h])
            s_ref[h] = s[h] * egl_ref[g, :, cs] + _dot_tn(kd_ref[sl, cs], vnb[h])
        return carry

    lax.fori_loop(0, G, body, 0)


def gdn_chunked(q, k, v, g, beta, batch, G=4, Ga=8):
    T, VD = v.shape
    C = CHUNK
    seq = T // batch
    nct = T // C
    G = _pick(seq // C, (G, 2, 1))
    Ga = _pick(seq // C, (Ga, 4, 2, 1))
    nh = GDN_V_HEADS
    rep = GDN_V_HEADS // GDN_QK_HEADS
    gcol = g.T.reshape(nh, T, 1)
    grow = jnp.pad(g.T.reshape(nh, nct // Ga, Ga, C), ((0, 0), (0, 0), (0, SUBLANES - Ga), (0, 0)))
    bcol = beta.T.reshape(nh, T, 1)
    qk_spec = pl.BlockSpec((C * Ga, LANES), lambda i, h: (i, h // rep))
    v_spec = pl.BlockSpec((C * Ga, LANES), lambda i, h: (i, h))
    col_spec = pl.BlockSpec((1, C * Ga, 1), lambda i, h: (h, i, 0))
    outs = pl.pallas_call(
        functools.partial(_gdn_a_kernel, G=Ga),
        grid=(nct // Ga, nh),
        in_specs=[qk_spec, qk_spec, v_spec, col_spec,
                  pl.BlockSpec((1, 1, SUBLANES, C), lambda i, h: (h, i, 0, 0)), col_spec,
                  pl.BlockSpec((C * Ga, C * Ga), lambda i, h: (0, 0))],
        out_specs=[pl.BlockSpec((Ga, 1, C, C), lambda i, h: (i, h, 0, 0)),
                   pl.BlockSpec((Ga, 1, C, C), lambda i, h: (i, h, 0, 0)),
                   v_spec, v_spec, v_spec, v_spec,
                   pl.BlockSpec((Ga, 1, LANES), lambda i, h: (i, 0, h))],
        out_shape=[jax.ShapeDtypeStruct((nct, nh, C, C), F32),
                   jax.ShapeDtypeStruct((nct, nh, C, C), BF16),
                   jax.ShapeDtypeStruct((T, VD), BF16),
                   jax.ShapeDtypeStruct((T, VD), BF16),
                   jax.ShapeDtypeStruct((T, VD), BF16),
                   jax.ShapeDtypeStruct((T, VD), BF16),
                   jax.ShapeDtypeStruct((nct, 1, VD), F32)],
        compiler_params=_cparams(("parallel", "parallel")),
        name="gdn_a",
    )(q, k, v, gcol, grow, bcol, _chunk_tri(C, Ga))
    xm, attn, vb, kcp, qd, kd, egl = outs
    tinv = tri_inv(xm.reshape(nct * nh, C, C)).reshape(nct, nh, C, C)
    ncb = seq // C // G
    wide = pl.BlockSpec((C * G, VD), lambda b, n: (b * ncb + n, 0))
    mat = pl.BlockSpec((G, nh, C, C), lambda b, n: (b * ncb + n, 0, 0, 0))
    return pl.pallas_call(
        functools.partial(_gdn_c_kernel, G=G),
        grid=(batch, ncb),
        in_specs=[wide, wide, wide, wide, mat, mat,
                  pl.BlockSpec((G, 1, VD), lambda b, n: (b * ncb + n, 0, 0))],
        out_specs=wide,
        out_shape=jax.ShapeDtypeStruct((T, VD), F32),
        scratch_shapes=[pltpu.VMEM((nh, GDN_DK, GDN_DV), F32)],
        compiler_params=_cparams(("arbitrary", "arbitrary")),
        name="gdn_c",
    )(vb, kcp, qd, kd, tinv, attn, egl)


def _rope_pair(x, ct, st, low):
    swapped = jnp.where(low, pltpu.roll(x, LANES - RET_DK // 2, 1), pltpu.roll(x, RET_DK // 2, 1))
    return x * ct + swapped * st


def _ret_kernel(q_ref, k_ref, v_ref, ct_ref, st_ref, o_ref, s_ref, *, G):
    C = CHUNK

    @pl.when(pl.program_id(1) == 0)
    def _():
        s_ref[...] = jnp.zeros_like(s_ref)

    row, col = _chunk_masks()
    incl = col <= row
    dif = jnp.maximum(row - col, 0).astype(F32)
    lane = lax.broadcasted_iota(jnp.int32, (C, LANES), 1)
    head0 = lane < RET_DK
    low = lane % RET_DK < RET_DK // 2
    ridx = lax.broadcasted_iota(jnp.int32, (C, LANES), 0).astype(F32)

    def body(g, carry):
        sl = pl.ds(pl.multiple_of(g * C, C), C)
        ct, st = ct_ref[sl, :], st_ref[sl, :]
        lgs = [math.log(1.0 - 2.0 ** (-5.0 - h)) for h in range(RET_HEADS)]
        vss = [slice(h * RET_DV, (h + 1) * RET_DV) for h in range(RET_HEADS)]
        sc, cross, upd, vb = [], [], [], []
        for p in range(RET_HEADS // 2):
            ps = slice(p * LANES, (p + 1) * LANES)
            qp = _rope_pair(q_ref[sl, ps], ct, st, low) * RET_DK ** -0.5
            kp = _rope_pair(k_ref[sl, ps], ct, st, low)
            kb = kp.astype(BF16)
            for hh in range(2):
                h = 2 * p + hh
                msk = head0 if hh == 0 else jnp.logical_not(head0)
                q = jnp.where(msk, qp, 0.0)
                vb.append(v_ref[sl, vss[h]].astype(BF16))
                sc.append(_dot_nt(q.astype(BF16), kb))
                qdec = (q * jnp.exp((ridx + 1.0) * lgs[h])).astype(BF16)
                cross.append(_dot(qdec, s_ref[h].astype(BF16)))
                kdec = (jnp.where(msk, kp, 0.0) * jnp.exp((C - 1.0 - ridx) * lgs[h])).astype(BF16)
                upd.append(_dot_tn(kdec, vb[h]))
        for h in range(RET_HEADS):
            dmat = jnp.where(incl, jnp.exp(dif * lgs[h]), 0.0)
            o_ref[sl, vss[h]] = _dot((sc[h] * dmat).astype(BF16), vb[h]) + cross[h]
            s_ref[h] = s_ref[h] * math.exp(C * lgs[h]) + upd[h]
        return carry

    lax.fori_loop(0, G, body, 0)


def retention_chunked(p, q_blk, k_blk, v_blk, ct, st, batch, G=4):
    T = p.shape[0]
    QD, VD = RET_HEADS * RET_DK, RET_HEADS * RET_DV
    C = CHUNK
    seq = T // batch
    G = _pick(seq // C, (G, 2, 1))
    ncb = seq // C // G
    tab = pl.BlockSpec((C * G, LANES), lambda b, n: (b * ncb + n, 0))
    wide = pl.BlockSpec((C * G, VD), lambda b, n: (b * ncb + n, 0))
    return pl.pallas_call(
        functools.partial(_ret_kernel, G=G),
        grid=(batch, ncb),
        in_specs=[pl.BlockSpec((C * G, QD), lambda b, n: (b * ncb + n, q_blk)),
                  pl.BlockSpec((C * G, QD), lambda b, n: (b * ncb + n, k_blk)),
                  pl.BlockSpec((C * G, VD), lambda b, n: (b * ncb + n, v_blk)),
                  tab, tab],
        out_specs=wide,
        out_shape=jax.ShapeDtypeStruct((T, VD), F32),
        scratch_shapes=[pltpu.VMEM((RET_HEADS, LANES, RET_DV), F32)],
        compiler_params=_cparams(("arbitrary", "arbitrary")),
        name="retention",
    )(p, p, p, ct, st)


def _rope_tables(positions, d):
    inv = ROPE_THETA ** (-jnp.arange(0, d, 2, dtype=F32) / d)
    ang = positions.astype(F32).reshape(-1, 1) * inv
    cos, sin = jnp.cos(ang), jnp.sin(ang)
    reps = LANES // d
    return jnp.concatenate([cos, cos] * reps, -1), jnp.concatenate([-sin, sin] * reps, -1)


L0_MLA_W = 768
L0_LORA_W = 384
L0_N = 3 * RWKV_DIM + L0_MLA_W + L0_LORA_W
CQ_PAD = 512


def _proj_shift_kernel(x_ref, xh_ref, w_ref, mu_ref, o_ref, *, tiles_per_seq):
    m = pl.program_id(0)
    w = w_ref[...]
    p = _dot(x_ref[...].astype(BF16), w)
    ph = _dot(xh_ref[...].astype(BF16), w)
    ph = jnp.where(m % tiles_per_seq == 0, 0.0, ph)
    rows = lax.broadcasted_iota(jnp.int32, p.shape, 0)
    prev = jnp.where(rows == 0, ph[SUBLANES - 1:SUBLANES], pltpu.roll(p, 1, 0))
    o_ref[...] = p + mu_ref[...] * (prev - p)


def proj_shift(x, w, mu, seq, tm=1024, tn=1408):
    M, D = x.shape
    N = w.shape[1]
    tm = _pick(seq, (tm, 256, 128, 64, 32, 16, 8))
    tn = _pick(N, (tn, 512, 384, 256, 128))
    hb = tm // SUBLANES
    return pl.pallas_call(
        functools.partial(_proj_shift_kernel, tiles_per_seq=seq // tm),
        grid=(M // tm, N // tn),
        in_specs=[pl.BlockSpec((tm, D), lambda i, j: (i, 0)),
                  pl.BlockSpec((SUBLANES, D), lambda i, j: (jnp.maximum(i * hb - 1, 0), 0)),
                  pl.BlockSpec((D, tn), lambda i, j: (0, j)),
                  pl.BlockSpec((1, tn), lambda i, j: (0, j))],
        out_specs=pl.BlockSpec((tm, tn), lambda i, j: (i, j)),
        out_shape=jax.ShapeDtypeStruct((M, N), F32),
        compiler_params=_cparams(("parallel", "parallel")),
        name="proj_shift",
    )(x, x, w, mu.reshape(1, N))


def _mla_pre_kernel(p_ref, ct_ref, st_ref, qn_ref, kn_ref, wq_ref, wkv_ref, q_ref, k_ref, v_ref):
    p = p_ref[...]
    cq = p[:, :CQ_PAD]
    ckv = p[:, CQ_PAD:CQ_PAD + MLA_KV_RANK]
    kp = p[:, CQ_PAD + MLA_KV_RANK:]
    cqn = cq * lax.rsqrt(jnp.sum(cq * cq, -1, keepdims=True) * (1.0 / MLA_Q_RANK) + RMS_EPS) * qn_ref[...]
    ckvn = ckv * lax.rsqrt(jnp.mean(ckv * ckv, -1, keepdims=True) + RMS_EPS) * kn_ref[...]
    qa = _dot(cqn.astype(BF16), wq_ref[...])
    kva = _dot(ckvn.astype(BF16), wkv_ref[...])
    ct, st = ct_ref[...], st_ref[...]
    kr = (kp * ct + pltpu.roll(kp, MLA_ROPE, 1) * st)[:, :MLA_ROPE].astype(BF16)
    ones = jnp.ones((p.shape[0], MLA_V), BF16)
    hw = 2 * LANES
    for h in range(MLA_HEADS):
        qh = qa[:, h * hw:(h + 1) * hw]
        pe = qh[:, MLA_NOPE:]
        q_ref[0, h, :, :MLA_NOPE] = qh[:, :MLA_NOPE].astype(BF16)
        q_ref[0, h, :, MLA_NOPE:] = (pe * ct + pltpu.roll(pe, MLA_ROPE, 1) * st)[:, :MLA_ROPE].astype(BF16)
        kvh = kva[:, h * hw:(h + 1) * hw]
        k_ref[0, h, :, :MLA_NOPE] = kvh[:, :MLA_NOPE].astype(BF16)
        k_ref[0, h, :, MLA_NOPE:] = kr
        v_ref[0, h, :, :MLA_V] = kvh[:, MLA_NOPE:].astype(BF16)
        v_ref[0, h, :, MLA_V:] = ones


def mla_pre(p, mla_blk, ct, st, qn, kn, wq, wkv, batch, tm=512):
    T = p.shape[0]
    seq = T // batch
    tm = _pick(seq, (tm, 256, 128))
    nt = seq // tm
    dqk = MLA_NOPE + MLA_ROPE
    head_spec = lambda d: pl.BlockSpec((1, MLA_HEADS, tm, d), lambda i: (i // nt, 0, i % nt, 0))
    full = lambda a: pl.BlockSpec(a.shape, lambda i: (0,) * a.ndim)
    return pl.pallas_call(
        _mla_pre_kernel,
        grid=(T // tm,),
        in_specs=[pl.BlockSpec((tm, L0_MLA_W), lambda i: (i, mla_blk)),
                  pl.BlockSpec((tm, LANES), lambda i: (i, 0)),
                  pl.BlockSpec((tm, LANES), lambda i: (i, 0)),
                  full(qn), full(kn), full(wq), full(wkv)],
        out_specs=[head_spec(dqk), head_spec(dqk), head_spec(2 * MLA_V)],
        out_shape=[jax.ShapeDtypeStruct((batch, MLA_HEADS, seq, dqk), BF16),
                   jax.ShapeDtypeStruct((batch, MLA_HEADS, seq, dqk), BF16),
                   jax.ShapeDtypeStruct((batch, MLA_HEADS, seq, 2 * MLA_V), BF16)],
        compiler_params=_cparams(("parallel",)),
        name="mla_pre",
    )(p, ct, st, qn, kn, wq, wkv)


def _softplus(x):
    return jnp.maximum(x, 0.0) + jnp.log(1.0 + jnp.exp(-jnp.abs(x)))


def _group_sum(x, ones_bd, terms):
    parts = _split3(x)[:terms]
    out = _dot(parts[0], ones_bd)
    for t in parts[1:]:
        out = out + _dot(t, ones_bd)
    return out


def _rwkv_pre_kernel(k_ref, lora_ref, wl_ref, g2_ref, w0_ref, a0_ref, kk_ref, ka_ref, ones_ref,
                     lw_ref, k2_ref, av_ref, bv_ref, g_ref):
    lora = lora_ref[...]
    lane = lax.broadcasted_iota(jnp.int32, (lora.shape[0], LANES), 1)
    wa_in = jnp.where(lane < DECAY_LORA, jnp.tanh(lora[:, :LANES]), lora[:, :LANES])
    wa = _dot(wa_in.astype(BF16), wl_ref[...])
    g_ref[...] = _dot(jax.nn.sigmoid(lora[:, LANES:]).astype(BF16), g2_ref[...])
    w_log = -_softplus(-(w0_ref[...] + wa[:, :RWKV_DIM])) - 0.5
    lw_ref[...] = -jnp.exp(w_log)
    a = jax.nn.sigmoid(a0_ref[...] + wa[:, RWKV_DIM:])
    k = k_ref[...]
    kk = k * kk_ref[...]
    kkn = kk * lax.rsqrt(_group_sum(kk * kk, ones_ref[...], 2) + 1e-6)
    k2_ref[...] = k * (1.0 + (a - 1.0) * ka_ref[...])
    av_ref[...] = -kkn
    bv_ref[...] = kkn * a


def rwkv_pre(p, k_blk, lora_blk, wl, g2, w0, a0, kk, ka, ones_bd, tm=512):
    T = p.shape[0]
    tm = _pick(T, (tm, 256, 128, 64))
    row = pl.BlockSpec((tm, RWKV_DIM), lambda i: (i, 0))
    full = lambda a: pl.BlockSpec(a.shape, lambda i: (0,) * a.ndim)
    vec = lambda a: a.reshape(1, RWKV_DIM)
    args = (wl, g2, vec(w0), vec(a0), vec(kk), vec(ka), ones_bd)
    return pl.pallas_call(
        _rwkv_pre_kernel,
        grid=(T // tm,),
        in_specs=[pl.BlockSpec((tm, RWKV_DIM), lambda i: (i, k_blk)),
                  pl.BlockSpec((tm, L0_LORA_W), lambda i: (i, lora_blk))] + [full(a) for a in args],
        out_specs=[row] * 5,
        out_shape=[jax.ShapeDtypeStruct((T, RWKV_DIM), F32)] * 5,
        compiler_params=_cparams(("parallel",)),
        name="rwkv_pre",
    )(p, p, *args)


def _rwkv_post_kernel(y_ref, r_ref, k2_ref, v_ref, g_ref, rk_ref, gw_ref, gb_ref, ones_ref, o_ref):
    ones_bd = ones_ref[...]
    y = y_ref[...]
    yc = y - _group_sum(y, ones_bd, 2) * (1.0 / RWKV_HEAD)
    var = _group_sum(yc * yc, ones_bd, 1) * (1.0 / RWKV_HEAD)
    yn = yc * lax.rsqrt(var + RWKV_GN_EPS) * gw_ref[...] + gb_ref[...]
    bonus = _group_sum(r_ref[...] * k2_ref[...] * rk_ref[...], ones_bd, 2) * v_ref[...]
    o_ref[...] = ((yn + bonus) * g_ref[...]).astype(o_ref.dtype)


def rwkv_post(y, p, r_blk, v_blk, k2, g, rk, gw, gb, ones_bd, tm=512):
    T = y.shape[0]
    tm = _pick(T, (tm, 256, 128, 64))
    row = pl.BlockSpec((tm, RWKV_DIM), lambda i: (i, 0))
    full = lambda a: pl.BlockSpec(a.shape, lambda i: (0,) * a.ndim)
    vec = lambda a: a.reshape(1, RWKV_DIM)
    args = (vec(rk), vec(gw), vec(gb), ones_bd)
    return pl.pallas_call(
        _rwkv_post_kernel,
        grid=(T // tm,),
        in_specs=[row, pl.BlockSpec((tm, RWKV_DIM), lambda i: (i, r_blk)), row,
                  pl.BlockSpec((tm, RWKV_DIM), lambda i: (i, v_blk)), row] + [full(a) for a in args],
        out_specs=row,
        out_shape=jax.ShapeDtypeStruct((T, RWKV_DIM), BF16),
        compiler_params=_cparams(("parallel",)),
        name="rwkv_post",
    )(y, p, k2, p, g, *args)


def _mm2_ln_kernel(a1_ref, a2_ref, w1_ref, w2_ref, x_ref, g_ref, b_ref, o_ref):
    y = ALPHA * x_ref[...] + _dot(a1_ref[...], w1_ref[...]) + _dot(a2_ref[...], w2_ref[...])
    mu = jnp.mean(y, axis=-1, keepdims=True)
    yc = y - mu
    var = jnp.mean(yc * yc, axis=-1, keepdims=True)
    o_ref[...] = yc * lax.rsqrt(var + LN_EPS) * g_ref[...] + b_ref[...]


def mm2_ln(a1, a2, w, x, g, b, tm=512):
    M, K1 = a1.shape
    K2 = a2.shape[1]
    N = w.shape[1]
    tm = _pick(M, (tm, 256, 128, 64, 32, 16, 8))
    w = w.astype(BF16)
    row = lambda k: pl.BlockSpec((tm, k), lambda i: (i, 0))
    return pl.pallas_call(
        _mm2_ln_kernel,
        grid=(M // tm,),
        in_specs=[row(K1), row(K2),
                  pl.BlockSpec((K1, N), lambda i: (0, 0)),
                  pl.BlockSpec((K2, N), lambda i: (0, 0)),
                  row(N),
                  pl.BlockSpec((1, N), lambda i: (0, 0)),
                  pl.BlockSpec((1, N), lambda i: (0, 0))],
        out_specs=row(N),
        out_shape=jax.ShapeDtypeStruct((M, N), F32),
        compiler_params=_cparams(("parallel",)),
        name="mm2_ln",
    )(a1, a2, w[:K1], w[K1:], x, g.reshape(1, N), b.reshape(1, N))


def _half_swap(n):
    return jnp.concatenate([jnp.arange(n // 2, n), jnp.arange(0, n // 2)])


def _mla_rwkv_mixer(xb, B, S, ct, st, w_in, q_norm, w_uq, kv_norm, w_ukv, rwkv_mu, rwkv_w0, rwkv_w2,
                    rwkv_a0, rwkv_a2, rwkv_g2, rwkv_k_k, rwkv_k_a, rwkv_r_k, rwkv_gn_w, rwkv_gn_b,
                    w_out, ln_g, ln_b):
    T = B * S
    D = xb.shape[1]
    o_kv = MLA_Q_RANK
    o_pe = o_kv + MLA_KV_RANK
    o_r = o_pe + MLA_ROPE
    o_k, o_v = o_r + RWKV_DIM, o_r + 2 * RWKV_DIM
    o_xw = o_r + 3 * RWKV_DIM
    o_xg = o_xw + DECAY_LORA + AAA_LORA
    zc = lambda n: jnp.zeros((D, n), F32)
    w_pe = w_in[:, o_pe:o_r]
    w0p = jnp.concatenate([w_in[:, o_r:o_xw], w_in[:, :o_kv], zc(CQ_PAD - MLA_Q_RANK), w_in[:, o_kv:o_pe],
                           w_pe, w_pe[:, _half_swap(MLA_ROPE)], w_in[:, o_xw:], zc(L0_N - L0_MLA_W - 3 * RWKV_DIM
                                                                                   - (w_in.shape[1] - o_xw))],
                          axis=1).astype(BF16)
    mu = rwkv_mu
    mu_p = jnp.concatenate([mu[:3 * RWKV_DIM], jnp.zeros((L0_MLA_W,), F32), mu[3 * RWKV_DIM:],
                            jnp.zeros((L0_N - L0_MLA_W - mu.shape[0],), F32)])
    p = proj_shift(xb, w0p, mu_p, S)
    dqk = MLA_NOPE + MLA_ROPE
    qs = dqk ** -0.5 * math.log2(math.e)
    wq = w_uq.reshape(MLA_Q_RANK, MLA_HEADS, dqk) * qs
    wq = jnp.concatenate([wq, wq[..., MLA_NOPE:][..., _half_swap(MLA_ROPE)]], -1).reshape(MLA_Q_RANK, -1)
    wq = jnp.pad(wq, ((0, CQ_PAD - MLA_Q_RANK), (0, 0))).astype(BF16)
    qn = jnp.pad(q_norm, (0, CQ_PAD - MLA_Q_RANK)).reshape(1, CQ_PAD)
    q, k, v = mla_pre(p, 3 * RWKV_DIM // L0_MLA_W, ct, st, qn, kv_norm.reshape(1, -1), wq,
                      w_ukv.astype(BF16), B)
    o_mla = flash_attention(q, k, v)
    zl = jnp.zeros((DECAY_LORA, RWKV_DIM), F32)
    wl = jnp.concatenate([jnp.concatenate([rwkv_w2, zl], 1), jnp.concatenate([zl, rwkv_a2], 1)], 0).astype(BF16)
    g2 = jnp.pad(rwkv_g2, ((0, L0_LORA_W - LANES - GATE_LORA), (0, 0))).astype(BF16)
    hid = jnp.arange(RWKV_DIM) // RWKV_HEAD
    ones_bd = (hid[:, None] == hid[None, :]).astype(BF16)
    lw, k2, av, bv, g = rwkv_pre(p, 1, (3 * RWKV_DIM + L0_MLA_W) // L0_LORA_W, wl, g2, rwkv_w0, rwkv_a0,
                                 rwkv_k_k, rwkv_k_a, ones_bd)
    y = rwkv7_chunked(p, 0, 2 * RWKV_DIM // LANES, lw, k2, av, bv, B)
    o_rwkv = rwkv_post(y, p, 0, 2, k2, g, rwkv_r_k.reshape(-1), rwkv_gn_w, rwkv_gn_b, ones_bd)
    return mm2_ln(o_mla, o_rwkv, w_out, xb, ln_g, ln_b)


GDN_QKV_W = 2 * GDN_QK_HEADS * GDN_DK + GDN_V_HEADS * GDN_DV


def _gdn_pre_kernel(x_ref, xh_ref, ba_ref, cw_ref, al_ref, dt_ref, q_ref, k_ref, v_ref, bg_ref, *,
                    tiles_per_seq):
    m = pl.program_id(0)
    x = x_ref[...]
    xh = jnp.where(m % tiles_per_seq == 0, 0.0, xh_ref[...])
    rows = lax.broadcasted_iota(jnp.int32, x.shape, 0)
    cw = cw_ref[...]
    acc = x * cw[GDN_CONV - 1:GDN_CONV]
    for j in range(1, GDN_CONV):
        sh = pltpu.roll(x, j, 0)
        for i in range(j):
            sh = jnp.where(rows == i, xh[SUBLANES - j + i:SUBLANES - j + i + 1], sh)
        acc = acc + sh * cw[GDN_CONV - 1 - j:GDN_CONV - j]
    y = acc * jax.nn.sigmoid(acc)
    qk_w = GDN_QK_HEADS * GDN_DK
    for h in range(GDN_QK_HEADS):
        cs = slice(h * GDN_DK, (h + 1) * GDN_DK)
        yq = y[:, cs]
        q_ref[:, cs] = yq * (lax.rsqrt(jnp.sum(yq * yq, -1, keepdims=True) + 1e-6) * GDN_DK ** -0.5)
        yk = y[:, qk_w + h * GDN_DK:qk_w + (h + 1) * GDN_DK]
        k_ref[:, cs] = yk * lax.rsqrt(jnp.sum(yk * yk, -1, keepdims=True) + 1e-6)
    v_ref[...] = y[:, 2 * qk_w:]
    ba = ba_ref[...]
    lane = lax.broadcasted_iota(jnp.int32, ba.shape, 1)
    g = -jnp.exp(al_ref[...]) * _softplus(ba + dt_ref[...])
    bg_ref[...] = jnp.where(lane < GDN_V_HEADS, jax.nn.sigmoid(ba), g)


def gdn_pre(p, ba_blk, conv_w, a_log, dt_bias, seq, tm=256):
    T = p.shape[0]
    tm = _pick(seq, (tm, 128, 64))
    hb = tm // SUBLANES
    qk_w = GDN_QK_HEADS * GDN_DK
    v_w = GDN_V_HEADS * GDN_DV
    lane_pad = lambda a: jnp.pad(a, (GDN_V_HEADS, LANES - 2 * GDN_V_HEADS)).reshape(1, LANES)
    row = lambda n: pl.BlockSpec((tm, n), lambda i: (i, 0))
    full = lambda shp: pl.BlockSpec(shp, lambda i: (0, 0))
    return pl.pallas_call(
        functools.partial(_gdn_pre_kernel, tiles_per_seq=seq // tm),
        grid=(T // tm,),
        in_specs=[row(GDN_QKV_W),
                  pl.BlockSpec((SUBLANES, GDN_QKV_W), lambda i: (jnp.maximum(i * hb - 1, 0), 0)),
                  pl.BlockSpec((tm, LANES), lambda i: (i, ba_blk)),
                  full((GDN_CONV, GDN_QKV_W)), full((1, LANES)), full((1, LANES))],
        out_specs=[row(qk_w), row(qk_w), row(v_w), row(LANES)],
        out_shape=[jax.ShapeDtypeStruct((T, qk_w), F32), jax.ShapeDtypeStruct((T, qk_w), F32),
                   jax.ShapeDtypeStruct((T, v_w), F32), jax.ShapeDtypeStruct((T, LANES), F32)],
        compiler_params=_cparams(("parallel",)),
        name="gdn_pre",
    )(p, p, p, conv_w, lane_pad(a_log), lane_pad(dt_bias))


def _l1_post_kernel(og_ref, z_ref, or_ref, rg_ref, gn_ref, rw_ref, rb_ref, o_ref):
    v_w = GDN_V_HEADS * GDN_DV
    for h in range(GDN_V_HEADS):
        cs = slice(h * LANES, (h + 1) * LANES)
        o = og_ref[:, cs]
        z = z_ref[:, cs]
        on = o * lax.rsqrt(jnp.mean(o * o, -1, keepdims=True) + RMS_EPS) * gn_ref[...]
        o_ref[:, cs] = (on * (z * jax.nn.sigmoid(z))).astype(o_ref.dtype)
        r = or_ref[:, cs]
        rc = r - jnp.mean(r, -1, keepdims=True)
        rn = rc * lax.rsqrt(jnp.mean(rc * rc, -1, keepdims=True) + LN_EPS) * rw_ref[:, cs] + rb_ref[:, cs]
        g = rg_ref[:, cs]
        o_ref[:, v_w + h * LANES:v_w + (h + 1) * LANES] = (rn * (g * jax.nn.sigmoid(g))).astype(o_ref.dtype)


def l1_post(o_gdn, o_ret, p, z_blk, rg_blk, gdn_norm, ret_gn_w, ret_gn_b, tm=512):
    T, W = o_gdn.shape
    tm = _pick(T, (tm, 256, 128, 64))
    row = pl.BlockSpec((tm, W), lambda i: (i, 0))
    return pl.pallas_call(
        _l1_post_kernel,
        grid=(T // tm,),
        in_specs=[row, pl.BlockSpec((tm, W), lambda i: (i, z_blk)), row,
                  pl.BlockSpec((tm, W), lambda i: (i, rg_blk)),
                  pl.BlockSpec((1, LANES), lambda i: (0, 0)),
                  pl.BlockSpec((1, W), lambda i: (0, 0)),
                  pl.BlockSpec((1, W), lambda i: (0, 0))],
        out_specs=pl.BlockSpec((tm, 2 * W), lambda i: (i, 0)),
        out_shape=jax.ShapeDtypeStruct((T, 2 * W), BF16),
        compiler_params=_cparams(("parallel",)),
        name="l1_post",
    )(o_gdn, p, o_ret, p, gdn_norm.reshape(1, LANES), ret_gn_w.reshape(1, W), ret_gn_b.reshape(1, W))


def _gdn_retention_mixer(xb, B, S, ct, st, w_in, gdn_conv_w, gdn_A_log, gdn_dt_bias, gdn_norm,
                         ret_gn_w, ret_gn_b, w_out, ln_g, ln_b):
    D = xb.shape[1]
    v_w = GDN_V_HEADS * GDN_DV
    n_gdn = GDN_QKV_W + v_w
    w1p = jnp.concatenate([w_in[:, :n_gdn], w_in[:, n_gdn + 2 * GDN_V_HEADS:],
                           w_in[:, n_gdn:n_gdn + 2 * GDN_V_HEADS],
                           jnp.zeros((D, LANES - 2 * GDN_V_HEADS), F32)], axis=1)
    p = mm(xb, w1p, tm=1024, tn=896)
    q, k, v, bg = gdn_pre(p, (w1p.shape[1] - LANES) // LANES, gdn_conv_w, gdn_A_log, gdn_dt_bias, S)
    o_gdn = gdn_chunked(q, k, v, bg[:, GDN_V_HEADS:2 * GDN_V_HEADS], bg[:, :GDN_V_HEADS], B)
    qd = RET_HEADS * RET_DK
    o_ret = retention_chunked(p, n_gdn // qd, n_gdn // qd + 1, (n_gdn + 2 * qd) // v_w, ct, st, B)
    m1 = l1_post(o_gdn, o_ret, p, GDN_QKV_W // v_w, (n_gdn + 2 * qd + v_w) // v_w, gdn_norm, ret_gn_w, ret_gn_b)
    return mm_ln_full(m1, w_out, xb, ln_g, ln_b, tm=512)


def _conv_ffn_ln(x, S, w_gate, w_val, conv_w, conv_b, w_down, ln_g, ln_b):
    h = ffn1(x, w_gate, w_val, conv_w, conv_b, S)
    return mm_ln_full(h, w_down, x, ln_g, ln_b)


def kernel(x, positions, l0_w_in, l0_q_norm, l0_w_uq, l0_kv_norm, l0_w_ukv, l0_rwkv_mu, l0_rwkv_w0, l0_rwkv_w2, l0_rwkv_a0, l0_rwkv_a2, l0_rwkv_g2, l0_rwkv_k_k, l0_rwkv_k_a, l0_rwkv_r_k, l0_rwkv_gn_w, l0_rwkv_gn_b, l0_w_out, l0_ln1_g, l0_ln1_b, l0_ffn_w_gate, l0_ffn_w_val, l0_ffn_conv_w, l0_ffn_conv_b, l0_ffn_w_down, l0_ln2_g, l0_ln2_b, l1_w_in, l1_gdn_conv_w, l1_gdn_A_log, l1_gdn_dt_bias, l1_gdn_norm, l1_ret_gn_w, l1_ret_gn_b, l1_w_out, l1_ln1_g, l1_ln1_b, l1_ffn_w_gate, l1_ffn_w_val, l1_ffn_conv_w, l1_ffn_conv_b, l1_ffn_w_down, l1_ln2_g, l1_ln2_b):
    B, S, D = x.shape
    T = B * S
    xb = x.reshape(T, D)
    assert MLA_ROPE == RET_DK
    ct, st = _rope_tables(positions, MLA_ROPE)
    xb = _mla_rwkv_mixer(xb, B, S, ct, st, l0_w_in, l0_q_norm, l0_w_uq, l0_kv_norm, l0_w_ukv, l0_rwkv_mu,
                         l0_rwkv_w0, l0_rwkv_w2, l0_rwkv_a0, l0_rwkv_a2, l0_rwkv_g2, l0_rwkv_k_k, l0_rwkv_k_a,
                         l0_rwkv_r_k, l0_rwkv_gn_w, l0_rwkv_gn_b, l0_w_out, l0_ln1_g, l0_ln1_b)
    xb = _conv_ffn_ln(xb, S, l0_ffn_w_gate, l0_ffn_w_val, l0_ffn_conv_w, l0_ffn_conv_b, l0_ffn_w_down,
                      l0_ln2_g, l0_ln2_b)
    xb = _gdn_retention_mixer(xb, B, S, ct, st, l1_w_in, l1_gdn_conv_w, l1_gdn_A_log, l1_gdn_dt_bias,
                              l1_gdn_norm, l1_ret_gn_w, l1_ret_gn_b, l1_w_out, l1_ln1_g, l1_ln1_b)
    xb = _conv_ffn_ln(xb, S, l1_ffn_w_gate, l1_ffn_w_val, l1_ffn_conv_w, l1_ffn_conv_b, l1_ffn_w_down,
                      l1_ln2_g, l1_ln2_b)
    return xb.reshape(B, S, D)
```

```python
import functools
import math

import jax
import jax.numpy as jnp
from jax import lax
from jax.experimental import pallas as pl
from jax.experimental.pallas import tpu as pltpu

F32 = jnp.float32
BF16 = jnp.bfloat16

DEPTH = 2
ALPHA = (2 * DEPTH) ** 0.25
LN_EPS = 1e-5
RMS_EPS = 1e-6
ROPE_THETA = 10000.0

MLA_HEADS = 8
MLA_Q_RANK = 448
MLA_KV_RANK = 128
MLA_NOPE = 128
MLA_ROPE = 64
MLA_V = 128
RWKV_HEADS = 16
RWKV_HEAD = 64
RWKV_DIM = RWKV_HEADS * RWKV_HEAD
DECAY_LORA = 64
AAA_LORA = 64
GATE_LORA = 160
RWKV_GN_EPS = 64e-5
GDN_QK_HEADS = 4
GDN_V_HEADS = 8
GDN_DK = 128
GDN_DV = 128
GDN_CONV = 4
RET_HEADS = 8
RET_DK = 64
RET_DV = 128
FFN_CONV = 3

CHUNK = 64
LANES = 128
SUBLANES = 8
VMEM_LIMIT = 56 * 1024 * 1024


def _cparams(sem):
    return pltpu.CompilerParams(dimension_semantics=sem, vmem_limit_bytes=VMEM_LIMIT)


def _dot(a, b):
    return jnp.dot(a, b, preferred_element_type=F32)


def _dot_nt(a, b):
    return lax.dot_general(a, b, (((1,), (1,)), ((), ())), preferred_element_type=F32)


def _dot_tn(a, b):
    return lax.dot_general(a, b, (((0,), (0,)), ((), ())), preferred_element_type=F32)


def _split3(x):
    hi = x.astype(BF16)
    r1 = x - hi.astype(F32)
    mid = r1.astype(BF16)
    lo = (r1 - mid.astype(F32)).astype(BF16)
    return hi, mid, lo


def _dot_exact_lhs(m_bf16, x):
    hi, mid, lo = _split3(x)
    return _dot(m_bf16, hi) + _dot(m_bf16, mid) + _dot(m_bf16, lo)


def _mm_kernel(a_ref, w_ref, o_ref, *scratch, nk):
    if nk == 1:
        o_ref[...] = _dot(a_ref[...].astype(BF16), w_ref[...]).astype(o_ref.dtype)
        return
    acc_ref, = scratch
    k = pl.program_id(2)

    @pl.when(k == 0)
    def _():
        acc_ref[...] = jnp.zeros_like(acc_ref)

    acc_ref[...] += _dot(a_ref[...].astype(BF16), w_ref[...])

    @pl.when(k == nk - 1)
    def _():
        o_ref[...] = acc_ref[...].astype(o_ref.dtype)


def _pick(n, pref):
    for t in pref:
        if n % t == 0:
            return t
    return n


def mm(a, w, out_dtype=F32, tm=512, tn=None, tk=None):
    M, K = a.shape
    N = w.shape[1]
    tm = _pick(M, (tm, 256, 128, 64, 32, 16, 8))
    tn = tn or _pick(N, (512, 384, 256, 128))
    tk = tk or K
    nk = K // tk
    w = w.astype(BF16)
    scratch = [] if nk == 1 else [pltpu.VMEM((tm, tn), F32)]
    return pl.pallas_call(
        functools.partial(_mm_kernel, nk=nk),
        grid=(M // tm, N // tn, nk),
        in_specs=[pl.BlockSpec((tm, tk), lambda i, j, k: (i, k)),
                  pl.BlockSpec((tk, tn), lambda i, j, k: (k, j))],
        out_specs=pl.BlockSpec((tm, tn), lambda i, j, k: (i, j)),
        out_shape=jax.ShapeDtypeStruct((M, N), out_dtype),
        scratch_shapes=scratch,
        compiler_params=_cparams(("parallel", "parallel", "arbitrary")),
        name="mm",
    )(a, w)


def _mm_ln_kernel(a_ref, w_ref, x_ref, g_ref, b_ref, o_ref, acc_ref, *, nk):
    k = pl.program_id(1)

    @pl.when(k == 0)
    def _():
        acc_ref[...] = jnp.zeros_like(acc_ref)

    acc_ref[...] += _dot(a_ref[...].astype(BF16), w_ref[...])

    @pl.when(k == nk - 1)
    def _():
        y = ALPHA * x_ref[...] + acc_ref[...]
        mu = jnp.mean(y, axis=-1, keepdims=True)
        yc = y - mu
        var = jnp.mean(yc * yc, axis=-1, keepdims=True)
        o_ref[...] = yc * lax.rsqrt(var + LN_EPS) * g_ref[...] + b_ref[...]


def _mm_ln_full_kernel(a_ref, w_ref, x_ref, g_ref, b_ref, o_ref):
    y = ALPHA * x_ref[...] + _dot(a_ref[...].astype(BF16), w_ref[...])
    mu = jnp.mean(y, axis=-1, keepdims=True)
    yc = y - mu
    var = jnp.mean(yc * yc, axis=-1, keepdims=True)
    o_ref[...] = yc * lax.rsqrt(var + LN_EPS) * g_ref[...] + b_ref[...]


def mm_ln_full(a, w, x, g, b, tm=256):
    M, K = a.shape
    N = w.shape[1]
    tm = _pick(M, (tm, 128, 64, 32, 16, 8))
    const = lambda shp: pl.BlockSpec(shp, lambda i: (0, 0), pipeline_mode=pl.Buffered(1))
    return pl.pallas_call(
        _mm_ln_full_kernel,
        grid=(M // tm,),
        in_specs=[pl.BlockSpec((tm, K), lambda i: (i, 0)),
                  const((K, N)),
                  pl.BlockSpec((tm, N), lambda i: (i, 0)),
                  const((1, N)), const((1, N))],
        out_specs=pl.BlockSpec((tm, N), lambda i: (i, 0)),
        out_shape=jax.ShapeDtypeStruct((M, N), F32),
        compiler_params=_cparams(("parallel",)),
        name="mm_ln_full",
    )(a, w.astype(BF16), x, g.reshape(1, N), b.reshape(1, N))


def mm_ln(a, w, x, g, b, tm=512, tk=None):
    M, K = a.shape
    N = w.shape[1]
    tm = _pick(M, (tm, 256, 128, 64, 32, 16, 8))
    tk = tk or _pick(K, (1024, 1408, 512, 256, 128))
    nk = K // tk
    return pl.pallas_call(
        functools.partial(_mm_ln_kernel, nk=nk),
        grid=(M // tm, nk),
        in_specs=[pl.BlockSpec((tm, tk), lambda i, k: (i, k)),
                  pl.BlockSpec((tk, N), lambda i, k: (k, 0)),
                  pl.BlockSpec((tm, N), lambda i, k: (i, 0)),
                  pl.BlockSpec((1, N), lambda i, k: (0, 0)),
                  pl.BlockSpec((1, N), lambda i, k: (0, 0))],
        out_specs=pl.BlockSpec((tm, N), lambda i, k: (i, 0)),
        out_shape=jax.ShapeDtypeStruct((M, N), F32),
        scratch_shapes=[pltpu.VMEM((tm, N), F32)],
        compiler_params=_cparams(("parallel", "arbitrary")),
        name="mm_ln",
    )(a, w.astype(BF16), x, g.reshape(1, N), b.reshape(1, N))


def _ffn1_kernel(x_ref, xh_ref, wg_ref, wv_ref, cw_ref, cb_ref, o_ref, *, tiles_per_seq):
    m = pl.program_id(0)
    x = x_ref[...].astype(BF16)
    g = _dot(x, wg_ref[...])
    gh = _dot(xh_ref[...].astype(BF16), wg_ref[...])
    gh = jnp.where(m % tiles_per_seq == 0, 0.0, gh)
    rows = lax.broadcasted_iota(jnp.int32, g.shape, 0)
    g1 = jnp.where(rows == 0, gh[SUBLANES - 1:SUBLANES], pltpu.roll(g, 1, 0))
    g2 = jnp.where(rows == 0, gh[SUBLANES - 2:SUBLANES - 1],
                   jnp.where(rows == 1, gh[SUBLANES - 1:SUBLANES], pltpu.roll(g, 2, 0)))
    cw = cw_ref[...]
    h = g * cw[2:3] + g1 * cw[1:2] + g2 * cw[0:1] + cb_ref[...]
    v = _dot(x, wv_ref[...])
    o_ref[...] = (h * jax.nn.sigmoid(h) * v).astype(o_ref.dtype)


def ffn1(x, w_gate, w_val, conv_w, conv_b, seq, tm=1024, tf=512):
    M, D = x.shape
    Fd = w_gate.shape[1]
    tm = _pick(seq, (tm, 256, 128, 64, 32, 16, 8))
    tf = _pick(Fd, (tf, 256, 128))
    hb = tm // SUBLANES
    return pl.pallas_call(
        functools.partial(_ffn1_kernel, tiles_per_seq=seq // tm),
        grid=(M // tm, Fd // tf),
        in_specs=[pl.BlockSpec((tm, D), lambda i, j: (i, 0)),
                  pl.BlockSpec((SUBLANES, D), lambda i, j: (jnp.maximum(i * hb - 1, 0), 0)),
                  pl.BlockSpec((D, tf), lambda i, j: (0, j)),
                  pl.BlockSpec((D, tf), lambda i, j: (0, j)),
                  pl.BlockSpec((FFN_CONV, tf), lambda i, j: (0, j)),
                  pl.BlockSpec((1, tf), lambda i, j: (0, j))],
        out_specs=pl.BlockSpec((tm, tf), lambda i, j: (i, j)),
        out_shape=jax.ShapeDtypeStruct((M, Fd), BF16),
        compiler_params=_cparams(("parallel", "parallel")),
        name="ffn1",
    )(x, x, w_gate.astype(BF16), w_val.astype(BF16), conv_w, conv_b.reshape(1, Fd))


def _flash_kernel(q_ref, k_ref, v_ref, o_ref, m_ref, acc_ref, s_ref, *, tq, tk, ts, dv):
    qi = pl.program_id(2)
    nsub = tq // ts
    ndiag = tq // tk
    nfull = qi * ndiag
    m_ref[...] = jnp.full_like(m_ref, -jnp.inf)
    acc_ref[...] = jnp.zeros_like(acc_ref)

    def above_diagonal(r, d):
        return (r + 1) * ts <= d * tk

    def scores(r, blk):
        k = k_ref[0, 0, pl.ds(pl.multiple_of(blk * tk, tk), tk), :]
        return _dot_nt(q_ref[0, 0, r * ts:(r + 1) * ts, :], k)

    def consume(blk, diag, nxt_diag):
        v = v_ref[0, 0, pl.ds(pl.multiple_of(blk * tk, tk), tk), :]
        for r in range(nsub):
            rs = slice(r * ts, (r + 1) * ts)
            live = diag is None or not above_diagonal(r, diag)
            if live:
                s = s_ref[rs, :]
            if nxt_diag is None or (nxt_diag >= 0 and not above_diagonal(r, nxt_diag)):
                s_ref[rs, :] = scores(r, blk + 1)
            if not live:
                continue
            if diag is not None and r * ts < (diag + 1) * tk - 1:
                rowi = lax.broadcasted_iota(jnp.int32, s.shape, 0) + r * ts
                coli = lax.broadcasted_iota(jnp.int32, s.shape, 1) + diag * tk
                s = jnp.where(coli <= rowi, s, -jnp.inf)
            m_old = m_ref[rs, :]
            m_new = jnp.maximum(m_old, jnp.max(s, axis=-1, keepdims=True))
            p = jnp.exp2(s - m_new).astype(BF16)
            acc_ref[rs, :] = jnp.exp2(m_old - m_new) * acc_ref[rs, :] + _dot(p, v)
            m_ref[rs, :] = m_new

    for r in range(nsub):
        s_ref[r * ts:(r + 1) * ts, :] = scores(r, 0)

    def body(j, carry):
        consume(2 * j, None, None)
        consume(2 * j + 1, None, None)
        return carry

    lax.fori_loop(0, nfull // 2, body, 0)

    @pl.when(nfull % 2 == 1)
    def _():
        consume(nfull - 1, None, None)

    for d in range(ndiag):
        consume(nfull + d, d, d + 1 if d + 1 < ndiag else -1)
    acc = acc_ref[...]
    o_ref[...] = (acc[:, :dv] / acc[:, dv:]).astype(o_ref.dtype)


def flash_attention(q, k, v, tq=1024, tk=1024, ts=512):
    B, H, S, dq = q.shape
    dv = v.shape[-1] // 2
    tq = _pick(S, (tq, 512, 256, 128))
    tk = _pick(tq, (tk, 256, 128))
    nq = S // tq
    return pl.pallas_call(
        functools.partial(_flash_kernel, tq=tq, tk=tk, ts=_pick(tq, (ts, 128)), dv=dv),
        grid=(B, H, nq),
        in_specs=[pl.BlockSpec((1, 1, tq, dq), lambda b, h, i: (b, h, i, 0)),
                  pl.BlockSpec((1, 1, S, dq), lambda b, h, i: (b, h, 0, 0)),
                  pl.BlockSpec((1, 1, S, 2 * dv), lambda b, h, i: (b, h, 0, 0))],
        out_specs=pl.BlockSpec((tq, dv), lambda b, h, i: (b * nq + i, h)),
        out_shape=jax.ShapeDtypeStruct((B * S, H * dv), BF16),
        scratch_shapes=[pltpu.VMEM((tq, 1), F32), pltpu.VMEM((tq, 2 * dv), F32), pltpu.VMEM((tq, tk), F32)],
        compiler_params=_cparams(("parallel", "parallel", "arbitrary")),
        name="flash",
    )(q, k, v)


def _tri_inv_kernel(a_ref, t_ref):
    C = a_ref.shape[1]
    kk = lax.broadcasted_iota(jnp.int32, (SUBLANES, LANES), 0)
    zero = jnp.zeros((SUBLANES, LANES), F32)
    for i in range(C):
        ng = i // SUBLANES + 1
        acc = [jnp.where(kk == i - g * SUBLANES, 1.0, 0.0).astype(F32) for g in range(ng)]
        for j in range(i):
            a_ij = a_ref[0, i, j:j + 1, :]
            for g in range(j // SUBLANES + 1):
                acc[g] = acc[g] + a_ij * t_ref[0, j, g * SUBLANES:(g + 1) * SUBLANES, :]
        for g in range(C // SUBLANES):
            t_ref[0, i, g * SUBLANES:(g + 1) * SUBLANES, :] = acc[g] if g < ng else zero


def tri_inv(a):
    n_in, C, _ = a.shape
    ng = -(-n_in // LANES)
    N = ng * LANES
    a = jnp.pad(a, ((0, N - n_in), (0, 0), (0, 0)))
    al = a.reshape(ng, LANES, C * C).transpose(0, 2, 1).reshape(ng, C, C, LANES)
    tl = pl.pallas_call(
        _tri_inv_kernel,
        grid=(ng,),
        in_specs=[pl.BlockSpec((1, C, C, LANES), lambda i: (i, 0, 0, 0))],
        out_specs=pl.BlockSpec((1, C, C, LANES), lambda i: (i, 0, 0, 0)),
        out_shape=jax.ShapeDtypeStruct((ng, C, C, LANES), F32),
        compiler_params=_cparams(("parallel",)),
        name="tri_inv",
    )(al)
    return tl.reshape(ng, C * C, LANES).transpose(0, 2, 1).reshape(N, C, C)[:n_in]


def _tri_inv_pairs_kernel(a_ref, t_ref, al_ref, tl_ref):
    C = al_ref.shape[1]
    for i in range(C):
        mt = a_ref[pl.ds(i, LANES, stride=C), :].T
        al_ref[0, i] = mt[:C]
        al_ref[1, i] = mt[C:]
    for hh in range(2):
        _tri_inv_kernel(al_ref.at[pl.ds(hh, 1)], tl_ref.at[pl.ds(hh, 1)])
    for i in range(C):
        mt = jnp.concatenate([tl_ref[0, i], tl_ref[1, i]], axis=0)
        t_ref[pl.ds(i, LANES, stride=C), :] = mt.T


def tri_inv_pairs(a):
    n, m, C, _ = a.shape
    tiles = n * m
    ng = -(-tiles // LANES)
    a2 = jnp.pad(a.reshape(tiles * C, 2 * C), ((0, (ng * LANES - tiles) * C), (0, 0)))
    t = pl.pallas_call(
        _tri_inv_pairs_kernel,
        grid=(ng,),
        in_specs=[pl.BlockSpec((LANES * C, 2 * C), lambda i: (i, 0))],
        out_specs=pl.BlockSpec((LANES * C, 2 * C), lambda i: (i, 0)),
        out_shape=jax.ShapeDtypeStruct((ng * LANES * C, 2 * C), F32),
        scratch_shapes=[pltpu.VMEM((2, C, C, LANES), F32), pltpu.VMEM((2, C, C, LANES), F32)],
        compiler_params=_cparams(("parallel",)),
        name="tri_inv_pairs",
    )(a2)
    return t[:tiles * C].reshape(n, m, C, 2 * C)


def _chunk_tri(C, G):
    idx = jnp.arange(C * G)
    return ((idx[None, :] <= idx[:, None]) & (idx[None, :] // C == idx[:, None] // C)).astype(BF16)


def _chunk_masks():
    C = CHUNK
    row = lax.broadcasted_iota(jnp.int32, (C, C), 0)
    col = lax.broadcasted_iota(jnp.int32, (C, C), 1)
    return row, col


def _rwkv_a_kernel(r_ref, lw_ref, k_ref, v_ref, av_ref, bv_ref, tri_ref,
                   aab_ref, mrb_ref, at_ref, akv_ref, rt_ref, bh_ref, yk_ref, kv_ref, egl_ref, *, G):
    C = CHUNK
    rowp = lax.broadcasted_iota(jnp.int32, (C, LANES), 0)
    lane = lax.broadcasted_iota(jnp.int32, (C, LANES), 1)
    strict = lane % C < rowp
    incl = lane % C <= rowp
    head0 = lane < RWKV_HEAD
    bd = (lax.broadcasted_iota(jnp.int32, (LANES, LANES), 0) // RWKV_HEAD ==
          lax.broadcasted_iota(jnp.int32, (LANES, LANES), 1) // RWKV_HEAD)

    def by_head(x):
        return jnp.concatenate([jnp.where(head0, x, 0.0), jnp.where(head0, 0.0, x)], axis=0).astype(BF16)

    lw_all = lw_ref[...]
    gc_all = _dot_exact_lhs(tri_ref[...], lw_all)
    sls = [slice(g * C, (g + 1) * C) for g in range(G)]
    at, rt, vb, egl, gl, xb, xk = [], [], [], [], [], [], []
    for sl in sls:
        gc = gc_all[sl]
        gl.append(gc[C - 1:C, :])
        ieg = jnp.exp(-gc)
        egl.append(jnp.exp(gl[-1] - gc))
        at.append(av_ref[sl, :] * jnp.exp(gc - lw_all[sl]))
        rt.append(r_ref[sl, :] * jnp.exp(gc))
        vb.append(by_head(v_ref[sl, :]))
        lhs = jnp.concatenate([at[-1], rt[-1]], axis=0).astype(BF16)
        xb.append(_dot_nt(lhs, by_head(bv_ref[sl, :] * ieg)))
        xk.append(_dot_nt(lhs, by_head(k_ref[sl, :] * ieg)))
    for g, sl in enumerate(sls):
        aab_ref[g, 0] = jnp.where(strict, xb[g][:C], 0.0)
        mrb_ref[g, 0] = jnp.where(incl, xb[g][C:], 0.0).astype(BF16)
        akv_ref[sl, :] = _dot(jnp.where(strict, xk[g][:C], 0.0).astype(BF16), vb[g]).astype(BF16)
        yk_ref[sl, :] = _dot(jnp.where(incl, xk[g][C:], 0.0).astype(BF16), vb[g])
    for g, sl in enumerate(sls):
        at_ref[sl, :] = at[g].astype(BF16)
        rt_ref[sl, :] = rt[g].astype(BF16)
        bh_ref[sl, :] = (bv_ref[sl, :] * egl[g]).astype(BF16)
        kv_ref[g, 0] = jnp.where(bd, _dot_tn(v_ref[sl, :].astype(BF16), (k_ref[sl, :] * egl[g]).astype(BF16)), 0.0)
        egl_ref[g] = jnp.exp(gl[g])


def _rwkv_c_kernel(at_ref, akv_ref, rt_ref, bh_ref, yk_ref, t_ref, mrb_ref, kv_ref, egl_ref,
                   y_ref, s_ref, *, G):
    C = CHUNK
    npair = RWKV_HEADS // 2

    @pl.when(pl.program_id(1) == 0)
    def _():
        s_ref[...] = jnp.zeros_like(s_ref)

    lane = lax.broadcasted_iota(jnp.int32, (C, LANES), 1)
    head0 = lane < RWKV_HEAD
    head0w = jnp.concatenate([head0, head0], axis=1)
    bd = (lax.broadcasted_iota(jnp.int32, (LANES, LANES), 0) // RWKV_HEAD ==
          lax.broadcasted_iota(jnp.int32, (LANES, LANES), 1) // RWKV_HEAD)
    zero = jnp.zeros((), BF16)

    def by_head(x, m):
        return jnp.concatenate([jnp.where(m, x, zero), jnp.where(m, zero, x)], axis=0)

    def body(g, carry):
        sl = pl.ds(pl.multiple_of(g * C, C), C)
        cols = [slice(p * LANES, (p + 1) * LANES) for p in range(npair)]
        w, s, sb, u, ub, y = [], [], [], [], [], []
        for p, cs in enumerate(cols):
            both = jnp.concatenate([at_ref[sl, cs], akv_ref[sl, cs]], axis=1)
            w.append(_dot(t_ref[g, p].astype(BF16), by_head(both, head0w)))
            s.append(s_ref[p])
            sb.append(s[p].astype(BF16))
        for p, cs in enumerate(cols):
            u.append(_dot_nt(w[p][:, :LANES].astype(BF16), sb[p]) + w[p][:, LANES:])
            ub.append(u[p].astype(BF16))
            y.append(_dot_nt(rt_ref[sl, cs], sb[p]) + yk_ref[sl, cs])
        for p, cs in enumerate(cols):
            y_ref[sl, cs] = y[p] + _dot(mrb_ref[g, p], by_head(ub[p], head0))
            s_ref[p] = (s[p] * egl_ref[g, :, cs] + jnp.where(bd, _dot_tn(ub[p], bh_ref[sl, cs]), 0.0)
                        + kv_ref[g, p])
        return carry

    lax.fori_loop(0, G, body, 0)


def rwkv7_chunked(p, r_blk, v_blk, lw, k, av, bv, batch, G=4, Ga=4):
    T, HD = lw.shape
    C = CHUNK
    seq = T // batch
    nct = T // C
    G = _pick(seq // C, (G, 2, 1))
    Ga = _pick(seq // C, (Ga, 4, 2, 1))
    npair = HD // LANES
    row_spec = pl.BlockSpec((C * Ga, LANES), lambda i, p: (i, p))
    pair_mat = pl.BlockSpec((Ga, 1, C, LANES), lambda i, p: (i, p, 0, 0))
    outs = pl.pallas_call(
        functools.partial(_rwkv_a_kernel, G=Ga),
        grid=(nct // Ga, npair),
        in_specs=[pl.BlockSpec((C * Ga, LANES), lambda i, p: (i, r_blk + p)), row_spec, row_spec,
                  pl.BlockSpec((C * Ga, LANES), lambda i, p: (i, v_blk + p)), row_spec, row_spec,
                  pl.BlockSpec((C * Ga, C * Ga), lambda i, p: (0, 0))],
        out_specs=[pair_mat, pair_mat,
                   row_spec, row_spec, row_spec, row_spec, row_spec,
                   pl.BlockSpec((Ga, 1, LANES, LANES), lambda i, p: (i, p, 0, 0)),
                   pl.BlockSpec((Ga, 1, LANES), lambda i, p: (i, 0, p))],
        out_shape=[jax.ShapeDtypeStruct((nct, npair, C, LANES), F32),
                   jax.ShapeDtypeStruct((nct, npair, C, LANES), BF16),
                   jax.ShapeDtypeStruct((T, HD), BF16),
                   jax.ShapeDtypeStruct((T, HD), BF16),
                   jax.ShapeDtypeStruct((T, HD), BF16),
                   jax.ShapeDtypeStruct((T, HD), BF16),
                   jax.ShapeDtypeStruct((T, HD), F32),
                   jax.ShapeDtypeStruct((nct, npair, LANES, LANES), F32),
                   jax.ShapeDtypeStruct((nct, 1, HD), F32)],
        compiler_params=_cparams(("parallel", "parallel")),
        name="rwkv_a",
    )(p, lw, k, p, av, bv, _chunk_tri(C, Ga))
    aab, mrb, at, akv, rt, bh, yk, kv, egl = outs
    tinv = tri_inv_pairs(aab)
    ncb = seq // C // G
    wide = pl.BlockSpec((C * G, HD), lambda b, n: (b * ncb + n, 0))
    return pl.pallas_call(
        functools.partial(_rwkv_c_kernel, G=G),
        grid=(batch, ncb),
        in_specs=[wide, wide, wide, wide, wide,
                  pl.BlockSpec((G, npair, C, LANES), lambda b, n: (b * ncb + n, 0, 0, 0)),
                  pl.BlockSpec((G, npair, C, LANES), lambda b, n: (b * ncb + n, 0, 0, 0)),
                  pl.BlockSpec((G, npair, LANES, LANES), lambda b, n: (b * ncb + n, 0, 0, 0)),
                  pl.BlockSpec((G, 1, HD), lambda b, n: (b * ncb + n, 0, 0))],
        out_specs=wide,
        out_shape=jax.ShapeDtypeStruct((T, HD), F32),
        scratch_shapes=[pltpu.VMEM((npair, LANES, LANES), F32)],
        compiler_params=_cparams(("arbitrary", "arbitrary")),
        name="rwkv_c",
    )(at, akv, rt, bh, yk, tinv, mrb, kv, egl)


def _gdn_a_kernel(q_ref, k_ref, v_ref, gcol_ref, grow_ref, bcol_ref, tri_ref,
                  xm_ref, attn_ref, vb_ref, kcp_ref, qd_ref, kd_ref, egl_ref, *, G):
    C = CHUNK
    row, col = _chunk_masks()
    strict = col < row
    incl = col <= row
    triu = jnp.where(row <= col, 1.0, 0.0).astype(BF16)
    n = C * G
    gc_all = _dot_exact_lhs(tri_ref[...], jnp.broadcast_to(gcol_ref[0], (n, LANES)))
    hi, mid, lo = _split3(grow_ref[0, 0])
    gcr_all = _dot(hi, triu) + _dot(mid, triu) + _dot(lo, triu)
    beta_all = jnp.broadcast_to(bcol_ref[0], (n, LANES))
    sls = [slice(g * C, (g + 1) * C) for g in range(G)]
    kk, qk = [], []
    for sl in sls:
        k = k_ref[sl, :]
        kbf = k.astype(BF16)
        kk.append(_dot_nt((k * beta_all[sl]).astype(BF16), kbf))
        qk.append(_dot_nt(q_ref[sl, :].astype(BF16), kbf))
    for g, sl in enumerate(sls):
        gc = gc_all[sl]
        diff = gc[:, :C] - gcr_all[g:g + 1]
        decay = jnp.where(incl, jnp.exp(jnp.where(incl, diff, 0.0)), 0.0)
        xm_ref[g, 0] = jnp.where(strict, -kk[g] * decay, 0.0)
        attn_ref[g, 0] = jnp.where(incl, qk[g] * decay, 0.0).astype(BF16)
        eg = jnp.exp(gc)
        gl = gc[C - 1:C, :]
        k = k_ref[sl, :]
        beta = beta_all[sl]
        vb_ref[sl, :] = (v_ref[sl, :] * beta).astype(BF16)
        kcp_ref[sl, :] = (k * beta * eg).astype(BF16)
        qd_ref[sl, :] = (q_ref[sl, :] * eg).astype(BF16)
        kd_ref[sl, :] = (k * jnp.exp(gl - gc)).astype(BF16)
        egl_ref[g] = jnp.exp(gl)


def _gdn_c_kernel(vb_ref, kcp_ref, qd_ref, kd_ref, t_ref, attn_ref, egl_ref, o_ref, s_ref, *, G):
    C = CHUNK

    @pl.when(pl.program_id(1) == 0)
    def _():
        s_ref[...] = jnp.zeros_like(s_ref)

    def body(g, carry):
        sl = pl.ds(pl.multiple_of(g * C, C), C)
        cols = [slice(h * LANES, (h + 1) * LANES) for h in range(GDN_V_HEADS)]
        w, s, sb, vnb, o = [], [], [], [], []
        for h, cs in enumerate(cols):
            both = jnp.concatenate([vb_ref[sl, cs], kcp_ref[sl, cs]], axis=1)
            w.append(_dot(t_ref[g, h].astype(BF16), both))
            s.append(s_ref[h])
            sb.append(s[h].astype(BF16))
        for h, cs in enumerate(cols):
            vnb.append((w[h][:, :LANES] - _dot(w[h][:, LANES:].astype(BF16), sb[h])).astype(BF16))
            o.append(_dot(qd_ref[sl, cs], sb[h]))
        for h, cs in enumerate(cols):
            o_ref[sl, cs] = o[h] + _dot(attn_ref[g, h], vnb[h])
            s_ref[h] = s[h] * egl_ref[g, :, cs] + _dot_tn(kd_ref[sl, cs], vnb[h])
        return carry

    lax.fori_loop(0, G, body, 0)


def gdn_chunked(q, k, v, g, beta, batch, G=4, Ga=8):
    T, VD = v.shape
    C = CHUNK
    seq = T // batch
    nct = T // C
    G = _pick(seq // C, (G, 2, 1))
    Ga = _pick(seq // C, (Ga, 4, 2, 1))
    nh = GDN_V_HEADS
    rep = GDN_V_HEADS // GDN_QK_HEADS
    gcol = g.T.reshape(nh, T, 1)
    grow = jnp.pad(g.T.reshape(nh, nct // Ga, Ga, C), ((0, 0), (0, 0), (0, SUBLANES - Ga), (0, 0)))
    bcol = beta.T.reshape(nh, T, 1)
    qk_spec = pl.BlockSpec((C * Ga, LANES), lambda i, h: (i, h // rep))
    v_spec = pl.BlockSpec((C * Ga, LANES), lambda i, h: (i, h))
    col_spec = pl.BlockSpec((1, C * Ga, 1), lambda i, h: (h, i, 0))
    outs = pl.pallas_call(
        functools.partial(_gdn_a_kernel, G=Ga),
        grid=(nct // Ga, nh),
        in_specs=[qk_spec, qk_spec, v_spec, col_spec,
                  pl.BlockSpec((1, 1, SUBLANES, C), lambda i, h: (h, i, 0, 0)), col_spec,
                  pl.BlockSpec((C * Ga, C * Ga), lambda i, h: (0, 0))],
        out_specs=[pl.BlockSpec((Ga, 1, C, C), lambda i, h: (i, h, 0, 0)),
                   pl.BlockSpec((Ga, 1, C, C), lambda i, h: (i, h, 0, 0)),
                   v_spec, v_spec, v_spec, v_spec,
                   pl.BlockSpec((Ga, 1, LANES), lambda i, h: (i, 0, h))],
        out_shape=[jax.ShapeDtypeStruct((nct, nh, C, C), F32),
                   jax.ShapeDtypeStruct((nct, nh, C, C), BF16),
                   jax.ShapeDtypeStruct((T, VD), BF16),
                   jax.ShapeDtypeStruct((T, VD), BF16),
                   jax.ShapeDtypeStruct((T, VD), BF16),
                   jax.ShapeDtypeStruct((T, VD), BF16),
                   jax.ShapeDtypeStruct((nct, 1, VD), F32)],
        compiler_params=_cparams(("parallel", "parallel")),
        name="gdn_a",
    )(q, k, v, gcol, grow, bcol, _chunk_tri(C, Ga))
    xm, attn, vb, kcp, qd, kd, egl = outs
    tinv = tri_inv(xm.reshape(nct * nh, C, C)).reshape(nct, nh, C, C)
    ncb = seq // C // G
    wide = pl.BlockSpec((C * G, VD), lambda b, n: (b * ncb + n, 0))
    mat = pl.BlockSpec((G, nh, C, C), lambda b, n: (b * ncb + n, 0, 0, 0))
    return pl.pallas_call(
        functools.partial(_gdn_c_kernel, G=G),
        grid=(batch, ncb),
        in_specs=[wide, wide, wide, wide, mat, mat,
                  pl.BlockSpec((G, 1, VD), lambda b, n: (b * ncb + n, 0, 0))],
        out_specs=wide,
        out_shape=jax.ShapeDtypeStruct((T, VD), F32),
        scratch_shapes=[pltpu.VMEM((nh, GDN_DK, GDN_DV), F32)],
        compiler_params=_cparams(("arbitrary", "arbitrary")),
        name="gdn_c",
    )(vb, kcp, qd, kd, tinv, attn, egl)


def _rope_pair(x, ct, st, low):
    swapped = jnp.where(low, pltpu.roll(x, LANES - RET_DK // 2, 1), pltpu.roll(x, RET_DK // 2, 1))
    return x * ct + swapped * st


def _ret_kernel(q_ref, k_ref, v_ref, ct_ref, st_ref, o_ref, s_ref, *, G):
    C = CHUNK

    @pl.when(pl.program_id(1) == 0)
    def _():
        s_ref[...] = jnp.zeros_like(s_ref)

    row, col = _chunk_masks()
    incl = col <= row
    dif = jnp.maximum(row - col, 0).astype(F32)
    lane = lax.broadcasted_iota(jnp.int32, (C, LANES), 1)
    head0 = lane < RET_DK
    low = lane % RET_DK < RET_DK // 2
    ridx = lax.broadcasted_iota(jnp.int32, (C, LANES), 0).astype(F32)

    def body(g, carry):
        sl = pl.ds(pl.multiple_of(g * C, C), C)
        ct, st = ct_ref[sl, :], st_ref[sl, :]
        lgs = [math.log(1.0 - 2.0 ** (-5.0 - h)) for h in range(RET_HEADS)]
        vss = [slice(h * RET_DV, (h + 1) * RET_DV) for h in range(RET_HEADS)]
        sc, cross, upd, vb = [], [], [], []
        for p in range(RET_HEADS // 2):
            ps = slice(p * LANES, (p + 1) * LANES)
            qp = _rope_pair(q_ref[sl, ps], ct, st, low) * RET_DK ** -0.5
            kp = _rope_pair(k_ref[sl, ps], ct, st, low)
            kb = kp.astype(BF16)
            for hh in range(2):
                h = 2 * p + hh
                msk = head0 if hh == 0 else jnp.logical_not(head0)
                q = jnp.where(msk, qp, 0.0)
                vb.append(v_ref[sl, vss[h]].astype(BF16))
                sc.append(_dot_nt(q.astype(BF16), kb))
                qdec = (q * jnp.exp((ridx + 1.0) * lgs[h])).astype(BF16)
                cross.append(_dot(qdec, s_ref[h].astype(BF16)))
                kdec = (jnp.where(msk, kp, 0.0) * jnp.exp((C - 1.0 - ridx) * lgs[h])).astype(BF16)
                upd.append(_dot_tn(kdec, vb[h]))
        for h in range(RET_HEADS):
            dmat = jnp.where(incl, jnp.exp(dif * lgs[h]), 0.0)
            o_ref[sl, vss[h]] = _dot((sc[h] * dmat).astype(BF16), vb[h]) + cross[h]
            s_ref[h] = s_ref[h] * math.exp(C * lgs[h]) + upd[h]
        return carry

    lax.fori_loop(0, G, body, 0)


def retention_chunked(p, q_blk, k_blk, v_blk, ct, st, batch, G=4):
    T = p.shape[0]
    QD, VD = RET_HEADS * RET_DK, RET_HEADS * RET_DV
    C = CHUNK
    seq = T // batch
    G = _pick(seq // C, (G, 2, 1))
    ncb = seq // C // G
    tab = pl.BlockSpec((C * G, LANES), lambda b, n: (b * ncb + n, 0))
    wide = pl.BlockSpec((C * G, VD), lambda b, n: (b * ncb + n, 0))
    return pl.pallas_call(
        functools.partial(_ret_kernel, G=G),
        grid=(batch, ncb),
        in_specs=[pl.BlockSpec((C * G, QD), lambda b, n: (b * ncb + n, q_blk)),
                  pl.BlockSpec((C * G, QD), lambda b, n: (b * ncb + n, k_blk)),
                  pl.BlockSpec((C * G, VD), lambda b, n: (b * ncb + n, v_blk)),
                  tab, tab],
        out_specs=wide,
        out_shape=jax.ShapeDtypeStruct((T, VD), F32),
        scratch_shapes=[pltpu.VMEM((RET_HEADS, LANES, RET_DV), F32)],
        compiler_params=_cparams(("arbitrary", "arbitrary")),
        name="retention",
    )(p, p, p, ct, st)


def _rope_tables(positions, d):
    inv = ROPE_THETA ** (-jnp.arange(0, d, 2, dtype=F32) / d)
    ang = positions.astype(F32).reshape(-1, 1) * inv
    cos, sin = jnp.cos(ang), jnp.sin(ang)
    reps = LANES // d
    return jnp.concatenate([cos, cos] * reps, -1), jnp.concatenate([-sin, sin] * reps, -1)


L0_MLA_W = 768
L0_LORA_W = 384
L0_N = 3 * RWKV_DIM + L0_MLA_W + L0_LORA_W
CQ_PAD = 512


def _proj_shift_kernel(x_ref, xh_ref, w_ref, mu_ref, o_ref, *, tiles_per_seq):
    m = pl.program_id(0)
    w = w_ref[...]
    p = _dot(x_ref[...].astype(BF16), w)
    ph = _dot(xh_ref[...].astype(BF16), w)
    ph = jnp.where(m % tiles_per_seq == 0, 0.0, ph)
    rows = lax.broadcasted_iota(jnp.int32, p.shape, 0)
    prev = jnp.where(rows == 0, ph[SUBLANES - 1:SUBLANES], pltpu.roll(p, 1, 0))
    o_ref[...] = p + mu_ref[...] * (prev - p)


def proj_shift(x, w, mu, seq, tm=1024, tn=1408):
    M, D = x.shape
    N = w.shape[1]
    tm = _pick(seq, (tm, 256, 128, 64, 32, 16, 8))
    tn = _pick(N, (tn, 512, 384, 256, 128))
    hb = tm // SUBLANES
    return pl.pallas_call(
        functools.partial(_proj_shift_kernel, tiles_per_seq=seq // tm),
        grid=(M // tm, N // tn),
        in_specs=[pl.BlockSpec((tm, D), lambda i, j: (i, 0)),
                  pl.BlockSpec((SUBLANES, D), lambda i, j: (jnp.maximum(i * hb - 1, 0), 0)),
                  pl.BlockSpec((D, tn), lambda i, j: (0, j)),
                  pl.BlockSpec((1, tn), lambda i, j: (0, j))],
        out_specs=pl.BlockSpec((tm, tn), lambda i, j: (i, j)),
        out_shape=jax.ShapeDtypeStruct((M, N), F32),
        compiler_params=_cparams(("parallel", "parallel")),
        name="proj_shift",
    )(x, x, w, mu.reshape(1, N))


def _mla_pre_kernel(p_ref, ct_ref, st_ref, qn_ref, kn_ref, wq_ref, wkv_ref, q_ref, k_ref, v_ref):
    p = p_ref[...]
    cq = p[:, :CQ_PAD]
    ckv = p[:, CQ_PAD:CQ_PAD + MLA_KV_RANK]
    kp = p[:, CQ_PAD + MLA_KV_RANK:]
    cqn = cq * lax.rsqrt(jnp.sum(cq * cq, -1, keepdims=True) * (1.0 / MLA_Q_RANK) + RMS_EPS) * qn_ref[...]
    ckvn = ckv * lax.rsqrt(jnp.mean(ckv * ckv, -1, keepdims=True) + RMS_EPS) * kn_ref[...]
    qa = _dot(cqn.astype(BF16), wq_ref[...])
    kva = _dot(ckvn.astype(BF16), wkv_ref[...])
    ct, st = ct_ref[...], st_ref[...]
    kr = (kp * ct + pltpu.roll(kp, MLA_ROPE, 1) * st)[:, :MLA_ROPE].astype(BF16)
    ones = jnp.ones((p.shape[0], MLA_V), BF16)
    hw = 2 * LANES
    for h in range(MLA_HEADS):
        qh = qa[:, h * hw:(h + 1) * hw]
        pe = qh[:, MLA_NOPE:]
        q_ref[0, h, :, :MLA_NOPE] = qh[:, :MLA_NOPE].astype(BF16)
        q_ref[0, h, :, MLA_NOPE:] = (pe * ct + pltpu.roll(pe, MLA_ROPE, 1) * st)[:, :MLA_ROPE].astype(BF16)
        kvh = kva[:, h * hw:(h + 1) * hw]
        k_ref[0, h, :, :MLA_NOPE] = kvh[:, :MLA_NOPE].astype(BF16)
        k_ref[0, h, :, MLA_NOPE:] = kr
        v_ref[0, h, :, :MLA_V] = kvh[:, MLA_NOPE:].astype(BF16)
        v_ref[0, h, :, MLA_V:] = ones


def mla_pre(p, mla_blk, ct, st, qn, kn, wq, wkv, batch, tm=512):
    T = p.shape[0]
    seq = T // batch
    tm = _pick(seq, (tm, 256, 128))
    nt = seq // tm
    dqk = MLA_NOPE + MLA_ROPE
    head_spec = lambda d: pl.BlockSpec((1, MLA_HEADS, tm, d), lambda i: (i // nt, 0, i % nt, 0))
    full = lambda a: pl.BlockSpec(a.shape, lambda i: (0,) * a.ndim)
    return pl.pallas_call(
        _mla_pre_kernel,
        grid=(T // tm,),
        in_specs=[pl.BlockSpec((tm, L0_MLA_W), lambda i: (i, mla_blk)),
                  pl.BlockSpec((tm, LANES), lambda i: (i, 0)),
                  pl.BlockSpec((tm, LANES), lambda i: (i, 0)),
                  full(qn), full(kn), full(wq), full(wkv)],
        out_specs=[head_spec(dqk), head_spec(dqk), head_spec(2 * MLA_V)],
        out_shape=[jax.ShapeDtypeStruct((batch, MLA_HEADS, seq, dqk), BF16),
                   jax.ShapeDtypeStruct((batch, MLA_HEADS, seq, dqk), BF16),
                   jax.ShapeDtypeStruct((batch, MLA_HEADS, seq, 2 * MLA_V), BF16)],
        compiler_params=_cparams(("parallel",)),
        name="mla_pre",
    )(p, ct, st, qn, kn, wq, wkv)


def _softplus(x):
    return jnp.maximum(x, 0.0) + jnp.log(1.0 + jnp.exp(-jnp.abs(x)))


def _group_sum(x, ones_bd, terms):
    parts = _split3(x)[:terms]
    out = _dot(parts[0], ones_bd)
    for t in parts[1:]:
        out = out + _dot(t, ones_bd)
    return out


def _rwkv_pre_kernel(k_ref, lora_ref, wl_ref, g2_ref, w0_ref, a0_ref, kk_ref, ka_ref, ones_ref,
                     lw_ref, k2_ref, av_ref, bv_ref, g_ref):
    lora = lora_ref[...]
    lane = lax.broadcasted_iota(jnp.int32, (lora.shape[0], LANES), 1)
    wa_in = jnp.where(lane < DECAY_LORA, jnp.tanh(lora[:, :LANES]), lora[:, :LANES])
    wa = _dot(wa_in.astype(BF16), wl_ref[...])
    g_ref[...] = _dot(jax.nn.sigmoid(lora[:, LANES:]).astype(BF16), g2_ref[...])
    w_log = -_softplus(-(w0_ref[...] + wa[:, :RWKV_DIM])) - 0.5
    lw_ref[...] = -jnp.exp(w_log)
    a = jax.nn.sigmoid(a0_ref[...] + wa[:, RWKV_DIM:])
    k = k_ref[...]
    kk = k * kk_ref[...]
    kkn = kk * lax.rsqrt(_group_sum(kk * kk, ones_ref[...], 2) + 1e-6)
    k2_ref[...] = k * (1.0 + (a - 1.0) * ka_ref[...])
    av_ref[...] = -kkn
    bv_ref[...] = kkn * a


def rwkv_pre(p, k_blk, lora_blk, wl, g2, w0, a0, kk, ka, ones_bd, tm=512):
    T = p.shape[0]
    tm = _pick(T, (tm, 256, 128, 64))
    row = pl.BlockSpec((tm, RWKV_DIM), lambda i: (i, 0))
    full = lambda a: pl.BlockSpec(a.shape, lambda i: (0,) * a.ndim)
    vec = lambda a: a.reshape(1, RWKV_DIM)
    args = (wl, g2, vec(w0), vec(a0), vec(kk), vec(ka), ones_bd)
    return pl.pallas_call(
        _rwkv_pre_kernel,
        grid=(T // tm,),
        in_specs=[pl.BlockSpec((tm, RWKV_DIM), lambda i: (i, k_blk)),
                  pl.BlockSpec((tm, L0_LORA_W), lambda i: (i, lora_blk))] + [full(a) for a in args],
        out_specs=[row] * 5,
        out_shape=[jax.ShapeDtypeStruct((T, RWKV_DIM), F32)] * 5,
        compiler_params=_cparams(("parallel",)),
        name="rwkv_pre",
    )(p, p, *args)


def _rwkv_post_kernel(y_ref, r_ref, k2_ref, v_ref, g_ref, rk_ref, gw_ref, gb_ref, ones_ref, o_ref):
    ones_bd = ones_ref[...]
    y = y_ref[...]
    yc = y - _group_sum(y, ones_bd, 2) * (1.0 / RWKV_HEAD)
    var = _group_sum(yc * yc, ones_bd, 1) * (1.0 / RWKV_HEAD)
    yn = yc * lax.rsqrt(var + RWKV_GN_EPS) * gw_ref[...] + gb_ref[...]
    bonus = _group_sum(r_ref[...] * k2_ref[...] * rk_ref[...], ones_bd, 2) * v_ref[...]
    o_ref[...] = ((yn + bonus) * g_ref[...]).astype(o_ref.dtype)


def rwkv_post(y, p, r_blk, v_blk, k2, g, rk, gw, gb, ones_bd, tm=512):
    T = y.shape[0]
    tm = _pick(T, (tm, 256, 128, 64))
    row = pl.BlockSpec((tm, RWKV_DIM), lambda i: (i, 0))
    full = lambda a: pl.BlockSpec(a.shape, lambda i: (0,) * a.ndim)
    vec = lambda a: a.reshape(1, RWKV_DIM)
    args = (vec(rk), vec(gw), vec(gb), ones_bd)
    return pl.pallas_call(
        _rwkv_post_kernel,
        grid=(T // tm,),
        in_specs=[row, pl.BlockSpec((tm, RWKV_DIM), lambda i: (i, r_blk)), row,
                  pl.BlockSpec((tm, RWKV_DIM), lambda i: (i, v_blk)), row] + [full(a) for a in args],
        out_specs=row,
        out_shape=jax.ShapeDtypeStruct((T, RWKV_DIM), BF16),
        compiler_params=_cparams(("parallel",)),
        name="rwkv_post",
    )(y, p, k2, p, g, *args)


def _mm2_ln_kernel(a1_ref, a2_ref, w1_ref, w2_ref, x_ref, g_ref, b_ref, o_ref):
    y = ALPHA * x_ref[...] + _dot(a1_ref[...], w1_ref[...]) + _dot(a2_ref[...], w2_ref[...])
    mu = jnp.mean(y, axis=-1, keepdims=True)
    yc = y - mu
    var = jnp.mean(yc * yc, axis=-1, keepdims=True)
    o_ref[...] = yc * lax.rsqrt(var + LN_EPS) * g_ref[...] + b_ref[...]


def mm2_ln(a1, a2, w, x, g, b, tm=512):
    M, K1 = a1.shape
    K2 = a2.shape[1]
    N = w.shape[1]
    tm = _pick(M, (tm, 256, 128, 64, 32, 16, 8))
    w = w.astype(BF16)
    row = lambda k: pl.BlockSpec((tm, k), lambda i: (i, 0))
    return pl.pallas_call(
        _mm2_ln_kernel,
        grid=(M // tm,),
        in_specs=[row(K1), row(K2),
                  pl.BlockSpec((K1, N), lambda i: (0, 0)),
                  pl.BlockSpec((K2, N), lambda i: (0, 0)),
                  row(N),
                  pl.BlockSpec((1, N), lambda i: (0, 0)),
                  pl.BlockSpec((1, N), lambda i: (0, 0))],
        out_specs=row(N),
        out_shape=jax.ShapeDtypeStruct((M, N), F32),
        compiler_params=_cparams(("parallel",)),
        name="mm2_ln",
    )(a1, a2, w[:K1], w[K1:], x, g.reshape(1, N), b.reshape(1, N))


def _half_swap(n):
    return jnp.concatenate([jnp.arange(n // 2, n), jnp.arange(0, n // 2)])


def _mla_rwkv_mixer(xb, B, S, ct, st, w_in, q_norm, w_uq, kv_norm, w_ukv, rwkv_mu, rwkv_w0, rwkv_w2,
                    rwkv_a0, rwkv_a2, rwkv_g2, rwkv_k_k, rwkv_k_a, rwkv_r_k, rwkv_gn_w, rwkv_gn_b,
                    w_out, ln_g, ln_b):
    T = B * S
    D = xb.shape[1]
    o_kv = MLA_Q_RANK
    o_pe = o_kv + MLA_KV_RANK
    o_r = o_pe + MLA_ROPE
    o_k, o_v = o_r + RWKV_DIM, o_r + 2 * RWKV_DIM
    o_xw = o_r + 3 * RWKV_DIM
    o_xg = o_xw + DECAY_LORA + AAA_LORA
    zc = lambda n: jnp.zeros((D, n), F32)
    w_pe = w_in[:, o_pe:o_r]
    w0p = jnp.concatenate([w_in[:, o_r:o_xw], w_in[:, :o_kv], zc(CQ_PAD - MLA_Q_RANK), w_in[:, o_kv:o_pe],
                           w_pe, w_pe[:, _half_swap(MLA_ROPE)], w_in[:, o_xw:], zc(L0_N - L0_MLA_W - 3 * RWKV_DIM
                                                                                   - (w_in.shape[1] - o_xw))],
                          axis=1).astype(BF16)
    mu = rwkv_mu
    mu_p = jnp.concatenate([mu[:3 * RWKV_DIM], jnp.zeros((L0_MLA_W,), F32), mu[3 * RWKV_DIM:],
                            jnp.zeros((L0_N - L0_MLA_W - mu.shape[0],), F32)])
    p = proj_shift(xb, w0p, mu_p, S)
    dqk = MLA_NOPE + MLA_ROPE
    qs = dqk ** -0.5 * math.log2(math.e)
    wq = w_uq.reshape(MLA_Q_RANK, MLA_HEADS, dqk) * qs
    wq = jnp.concatenate([wq, wq[..., MLA_NOPE:][..., _half_swap(MLA_ROPE)]], -1).reshape(MLA_Q_RANK, -1)
    wq = jnp.pad(wq, ((0, CQ_PAD - MLA_Q_RANK), (0, 0))).astype(BF16)
    qn = jnp.pad(q_norm, (0, CQ_PAD - MLA_Q_RANK)).reshape(1, CQ_PAD)
    q, k, v = mla_pre(p, 3 * RWKV_DIM // L0_MLA_W, ct, st, qn, kv_norm.reshape(1, -1), wq,
                      w_ukv.astype(BF16), B)
    o_mla = flash_attention(q, k, v)
    zl = jnp.zeros((DECAY_LORA, RWKV_DIM), F32)
    wl = jnp.concatenate([jnp.concatenate([rwkv_w2, zl], 1), jnp.concatenate([zl, rwkv_a2], 1)], 0).astype(BF16)
    g2 = jnp.pad(rwkv_g2, ((0, L0_LORA_W - LANES - GATE_LORA), (0, 0))).astype(BF16)
    hid = jnp.arange(RWKV_DIM) // RWKV_HEAD
    ones_bd = (hid[:, None] == hid[None, :]).astype(BF16)
    lw, k2, av, bv, g = rwkv_pre(p, 1, (3 * RWKV_DIM + L0_MLA_W) // L0_LORA_W, wl, g2, rwkv_w0, rwkv_a0,
                                 rwkv_k_k, rwkv_k_a, ones_bd)
    y = rwkv7_chunked(p, 0, 2 * RWKV_DIM // LANES, lw, k2, av, bv, B)
    o_rwkv = rwkv_post(y, p, 0, 2, k2, g, rwkv_r_k.reshape(-1), rwkv_gn_w, rwkv_gn_b, ones_bd)
    return mm2_ln(o_mla, o_rwkv, w_out, xb, ln_g, ln_b)


GDN_QKV_W = 2 * GDN_QK_HEADS * GDN_DK + GDN_V_HEADS * GDN_DV


def _gdn_pre_kernel(x_ref, xh_ref, ba_ref, cw_ref, al_ref, dt_ref, q_ref, k_ref, v_ref, bg_ref, *,
                    tiles_per_seq):
    m = pl.program_id(0)
    x = x_ref[...]
    xh = jnp.where(m % tiles_per_seq == 0, 0.0, xh_ref[...])
    rows = lax.broadcasted_iota(jnp.int32, x.shape, 0)
    cw = cw_ref[...]
    acc = x * cw[GDN_CONV - 1:GDN_CONV]
    for j in range(1, GDN_CONV):
        sh = pltpu.roll(x, j, 0)
        for i in range(j):
            sh = jnp.where(rows == i, xh[SUBLANES - j + i:SUBLANES - j + i + 1], sh)
        acc = acc + sh * cw[GDN_CONV - 1 - j:GDN_CONV - j]
    y = acc * jax.nn.sigmoid(acc)
    qk_w = GDN_QK_HEADS * GDN_DK
    for h in range(GDN_QK_HEADS):
        cs = slice(h * GDN_DK, (h + 1) * GDN_DK)
        yq = y[:, cs]
        q_ref[:, cs] = yq * (lax.rsqrt(jnp.sum(yq * yq, -1, keepdims=True) + 1e-6) * GDN_DK ** -0.5)
        yk = y[:, qk_w + h * GDN_DK:qk_w + (h + 1) * GDN_DK]
        k_ref[:, cs] = yk * lax.rsqrt(jnp.sum(yk * yk, -1, keepdims=True) + 1e-6)
    v_ref[...] = y[:, 2 * qk_w:]
    ba = ba_ref[...]
    lane = lax.broadcasted_iota(jnp.int32, ba.shape, 1)
    g = -jnp.exp(al_ref[...]) * _softplus(ba + dt_ref[...])
    bg_ref[...] = jnp.where(lane < GDN_V_HEADS, jax.nn.sigmoid(ba), g)


def gdn_pre(p, ba_blk, conv_w, a_log, dt_bias, seq, tm=256):
    T = p.shape[0]
    tm = _pick(seq, (tm, 128, 64))
    hb = tm // SUBLANES
    qk_w = GDN_QK_HEADS * GDN_DK
    v_w = GDN_V_HEADS * GDN_DV
    lane_pad = lambda a: jnp.pad(a, (GDN_V_HEADS, LANES - 2 * GDN_V_HEADS)).reshape(1, LANES)
    row = lambda n: pl.BlockSpec((tm, n), lambda i: (i, 0))
    full = lambda shp: pl.BlockSpec(shp, lambda i: (0, 0))
    return pl.pallas_call(
        functools.partial(_gdn_pre_kernel, tiles_per_seq=seq // tm),
        grid=(T // tm,),
        in_specs=[row(GDN_QKV_W),
                  pl.BlockSpec((SUBLANES, GDN_QKV_W), lambda i: (jnp.maximum(i * hb - 1, 0), 0)),
                  pl.BlockSpec((tm, LANES), lambda i: (i, ba_blk)),
                  full((GDN_CONV, GDN_QKV_W)), full((1, LANES)), full((1, LANES))],
        out_specs=[row(qk_w), row(qk_w), row(v_w), row(LANES)],
        out_shape=[jax.ShapeDtypeStruct((T, qk_w), F32), jax.ShapeDtypeStruct((T, qk_w), F32),
                   jax.ShapeDtypeStruct((T, v_w), F32), jax.ShapeDtypeStruct((T, LANES), F32)],
        compiler_params=_cparams(("parallel",)),
        name="gdn_pre",
    )(p, p, p, conv_w, lane_pad(a_log), lane_pad(dt_bias))


def _l1_post_kernel(og_ref, z_ref, or_ref, rg_ref, gn_ref, rw_ref, rb_ref, o_ref):
    v_w = GDN_V_HEADS * GDN_DV
    for h in range(GDN_V_HEADS):
        cs = slice(h * LANES, (h + 1) * LANES)
        o = og_ref[:, cs]
        z = z_ref[:, cs]
        on = o * lax.rsqrt(jnp.mean(o * o, -1, keepdims=True) + RMS_EPS) * gn_ref[...]
        o_ref[:, cs] = (on * (z * jax.nn.sigmoid(z))).astype(o_ref.dtype)
        r = or_ref[:, cs]
        rc = r - jnp.mean(r, -1, keepdims=True)
        rn = rc * lax.rsqrt(jnp.mean(rc * rc, -1, keepdims=True) + LN_EPS) * rw_ref[:, cs] + rb_ref[:, cs]
        g = rg_ref[:, cs]
        o_ref[:, v_w + h * LANES:v_w + (h + 1) * LANES] = (rn * (g * jax.nn.sigmoid(g))).astype(o_ref.dtype)


def l1_post(o_gdn, o_ret, p, z_blk, rg_blk, gdn_norm, ret_gn_w, ret_gn_b, tm=512):
    T, W = o_gdn.shape
    tm = _pick(T, (tm, 256, 128, 64))
    row = pl.BlockSpec((tm, W), lambda i: (i, 0))
    return pl.pallas_call(
        _l1_post_kernel,
        grid=(T // tm,),
        in_specs=[row, pl.BlockSpec((tm, W), lambda i: (i, z_blk)), row,
                  pl.BlockSpec((tm, W), lambda i: (i, rg_blk)),
                  pl.BlockSpec((1, LANES), lambda i: (0, 0)),
                  pl.BlockSpec((1, W), lambda i: (0, 0)),
                  pl.BlockSpec((1, W), lambda i: (0, 0))],
        out_specs=pl.BlockSpec((tm, 2 * W), lambda i: (i, 0)),
        out_shape=jax.ShapeDtypeStruct((T, 2 * W), BF16),
        compiler_params=_cparams(("parallel",)),
        name="l1_post",
    )(o_gdn, p, o_ret, p, gdn_norm.reshape(1, LANES), ret_gn_w.reshape(1, W), ret_gn_b.reshape(1, W))


def _gdn_retention_mixer(xb, B, S, ct, st, w_in, gdn_conv_w, gdn_A_log, gdn_dt_bias, gdn_norm,
                         ret_gn_w, ret_gn_b, w_out, ln_g, ln_b):
    D = xb.shape[1]
    v_w = GDN_V_HEADS * GDN_DV
    n_gdn = GDN_QKV_W + v_w
    w1p = jnp.concatenate([w_in[:, :n_gdn], w_in[:, n_gdn + 2 * GDN_V_HEADS:],
                           w_in[:, n_gdn:n_gdn + 2 * GDN_V_HEADS],
                           jnp.zeros((D, LANES - 2 * GDN_V_HEADS), F32)], axis=1)
    p = mm(xb, w1p, tm=1024, tn=896)
    q, k, v, bg = gdn_pre(p, (w1p.shape[1] - LANES) // LANES, gdn_conv_w, gdn_A_log, gdn_dt_bias, S)
    o_gdn = gdn_chunked(q, k, v, bg[:, GDN_V_HEADS:2 * GDN_V_HEADS], bg[:, :GDN_V_HEADS], B)
    qd = RET_HEADS * RET_DK
    o_ret = retention_chunked(p, n_gdn // qd, n_gdn // qd + 1, (n_gdn + 2 * qd) // v_w, ct, st, B)
    m1 = l1_post(o_gdn, o_ret, p, GDN_QKV_W // v_w, (n_gdn + 2 * qd + v_w) // v_w, gdn_norm, ret_gn_w, ret_gn_b)
    return mm_ln_full(m1, w_out, xb, ln_g, ln_b, tm=512)


def _conv_ffn_ln(x, S, w_gate, w_val, conv_w, conv_b, w_down, ln_g, ln_b):
    h = ffn1(x, w_gate, w_val, conv_w, conv_b, S)
    return mm_ln_full(h, w_down, x, ln_g, ln_b)


def kernel(x, positions, l0_w_in, l0_q_norm, l0_w_uq, l0_kv_norm, l0_w_ukv, l0_rwkv_mu, l0_rwkv_w0, l0_rwkv_w2, l0_rwkv_a0, l0_rwkv_a2, l0_rwkv_g2, l0_rwkv_k_k, l0_rwkv_k_a, l0_rwkv_r_k, l0_rwkv_gn_w, l0_rwkv_gn_b, l0_w_out, l0_ln1_g, l0_ln1_b, l0_ffn_w_gate, l0_ffn_w_val, l0_ffn_conv_w, l0_ffn_conv_b, l0_ffn_w_down, l0_ln2_g, l0_ln2_b, l1_w_in, l1_gdn_conv_w, l1_gdn_A_log, l1_gdn_dt_bias, l1_gdn_norm, l1_ret_gn_w, l1_ret_gn_b, l1_w_out, l1_ln1_g, l1_ln1_b, l1_ffn_w_gate, l1_ffn_w_val, l1_ffn_conv_w, l1_ffn_conv_b, l1_ffn_w_down, l1_ln2_g, l1_ln2_b):
    B, S, D = x.shape
    T = B * S
    xb = x.reshape(T, D)
    assert MLA_ROPE == RET_DK
    ct, st = _rope_tables(positions, MLA_ROPE)
    xb = _mla_rwkv_mixer(xb, B, S, ct, st, l0_w_in, l0_q_norm, l0_w_uq, l0_kv_norm, l0_w_ukv, l0_rwkv_mu,
                         l0_rwkv_w0, l0_rwkv_w2, l0_rwkv_a0, l0_rwkv_a2, l0_rwkv_g2, l0_rwkv_k_k, l0_rwkv_k_a,
                         l0_rwkv_r_k, l0_rwkv_gn_w, l0_rwkv_gn_b, l0_w_out, l0_ln1_g, l0_ln1_b)
    xb = _conv_ffn_ln(xb, S, l0_ffn_w_gate, l0_ffn_w_val, l0_ffn_conv_w, l0_ffn_conv_b, l0_ffn_w_down,
                      l0_ln2_g, l0_ln2_b)
    xb = _gdn_retention_mixer(xb, B, S, ct, st, l1_w_in, l1_gdn_conv_w, l1_gdn_A_log, l1_gdn_dt_bias,
                              l1_gdn_norm, l1_ret_gn_w, l1_ret_gn_b, l1_w_out, l1_ln1_g, l1_ln1_b)
    xb = _conv_ffn_ln(xb, S, l1_ffn_w_gate, l1_ffn_w_val, l1_ffn_conv_w, l1_ffn_conv_b, l1_ffn_w_down,
                      l1_ln2_g, l1_ln2_b)
    return xb.reshape(B, S, D)
```

```python
import functools
import math

import jax
import jax.numpy as jnp
from jax import lax
from jax.experimental import pallas as pl
from jax.experimental.pallas import tpu as pltpu

F32 = jnp.float32
BF16 = jnp.bfloat16

DEPTH = 2
ALPHA = (2 * DEPTH) ** 0.25
LN_EPS = 1e-5
RMS_EPS = 1e-6
ROPE_THETA = 10000.0

MLA_HEADS = 8
MLA_Q_RANK = 448
MLA_KV_RANK = 128
MLA_NOPE = 128
MLA_ROPE = 64
MLA_V = 128
RWKV_HEADS = 16
RWKV_HEAD = 64
RWKV_DIM = RWKV_HEADS * RWKV_HEAD
DECAY_LORA = 64
AAA_LORA = 64
GATE_LORA = 160
RWKV_GN_EPS = 64e-5
GDN_QK_HEADS = 4
GDN_V_HEADS = 8
GDN_DK = 128
GDN_DV = 128
GDN_CONV = 4
RET_HEADS = 8
RET_DK = 64
RET_DV = 128
FFN_CONV = 3

CHUNK = 64
LANES = 128
SUBLANES = 8
VMEM_LIMIT = 56 * 1024 * 1024


def _cparams(sem):
    return pltpu.CompilerParams(dimension_semantics=sem, vmem_limit_bytes=VMEM_LIMIT)


def _dot(a, b):
    return jnp.dot(a, b, preferred_element_type=F32)


def _dot_nt(a, b):
    return lax.dot_general(a, b, (((1,), (1,)), ((), ())), preferred_element_type=F32)


def _dot_tn(a, b):
    return lax.dot_general(a, b, (((0,), (0,)), ((), ())), preferred_element_type=F32)


def _split3(x):
    hi = x.astype(BF16)
    r1 = x - hi.astype(F32)
    mid = r1.astype(BF16)
    lo = (r1 - mid.astype(F32)).astype(BF16)
    return hi, mid, lo


def _dot_exact_lhs(m_bf16, x):
    hi, mid, lo = _split3(x)
    return _dot(m_bf16, hi) + _dot(m_bf16, mid) + _dot(m_bf16, lo)


def _mm_kernel(a_ref, w_ref, o_ref, *scratch, nk):
    if nk == 1:
        o_ref[...] = _dot(a_ref[...].astype(BF16), w_ref[...]).astype(o_ref.dtype)
        return
    acc_ref, = scratch
    k = pl.program_id(2)

    @pl.when(k == 0)
    def _():
        acc_ref[...] = jnp.zeros_like(acc_ref)

    acc_ref[...] += _dot(a_ref[...].astype(BF16), w_ref[...])

    @pl.when(k == nk - 1)
    def _():
        o_ref[...] = acc_ref[...].astype(o_ref.dtype)


def _pick(n, pref):
    for t in pref:
        if n % t == 0:
            return t
    return n


def mm(a, w, out_dtype=F32, tm=512, tn=None, tk=None):
    M, K = a.shape
    N = w.shape[1]
    tm = _pick(M, (tm, 256, 128, 64, 32, 16, 8))
    tn = tn or _pick(N, (512, 384, 256, 128))
    tk = tk or K
    nk = K // tk
    w = w.astype(BF16)
    scratch = [] if nk == 1 else [pltpu.VMEM((tm, tn), F32)]
    return pl.pallas_call(
        functools.partial(_mm_kernel, nk=nk),
        grid=(M // tm, N // tn, nk),
        in_specs=[pl.BlockSpec((tm, tk), lambda i, j, k: (i, k)),
                  pl.BlockSpec((tk, tn), lambda i, j, k: (k, j))],
        out_specs=pl.BlockSpec((tm, tn), lambda i, j, k: (i, j)),
        out_shape=jax.ShapeDtypeStruct((M, N), out_dtype),
        scratch_shapes=scratch,
        compiler_params=_cparams(("parallel", "parallel", "arbitrary")),
        name="mm",
    )(a, w)


def _mm_ln_kernel(a_ref, w_ref, x_ref, g_ref, b_ref, o_ref, acc_ref, *, nk):
    k = pl.program_id(1)

    @pl.when(k == 0)
    def _():
        acc_ref[...] = jnp.zeros_like(acc_ref)

    acc_ref[...] += _dot(a_ref[...].astype(BF16), w_ref[...])

    @pl.when(k == nk - 1)
    def _():
        y = ALPHA * x_ref[...] + acc_ref[...]
        mu = jnp.mean(y, axis=-1, keepdims=True)
        yc = y - mu
        var = jnp.mean(yc * yc, axis=-1, keepdims=True)
        o_ref[...] = yc * lax.rsqrt(var + LN_EPS) * g_ref[...] + b_ref[...]


def _mm_ln_full_kernel(a_ref, w_ref, x_ref, g_ref, b_ref, o_ref):
    y = ALPHA * x_ref[...] + _dot(a_ref[...].astype(BF16), w_ref[...])
    mu = jnp.mean(y, axis=-1, keepdims=True)
    yc = y - mu
    var = jnp.mean(yc * yc, axis=-1, keepdims=True)
    o_ref[...] = yc * lax.rsqrt(var + LN_EPS) * g_ref[...] + b_ref[...]


def mm_ln_full(a, w, x, g, b, tm=256):
    M, K = a.shape
    N = w.shape[1]
    tm = _pick(M, (tm, 128, 64, 32, 16, 8))
    const = lambda shp: pl.BlockSpec(shp, lambda i: (0, 0), pipeline_mode=pl.Buffered(1))
    return pl.pallas_call(
        _mm_ln_full_kernel,
        grid=(M // tm,),
        in_specs=[pl.BlockSpec((tm, K), lambda i: (i, 0)),
                  const((K, N)),
                  pl.BlockSpec((tm, N), lambda i: (i, 0)),
                  const((1, N)), const((1, N))],
        out_specs=pl.BlockSpec((tm, N), lambda i: (i, 0)),
        out_shape=jax.ShapeDtypeStruct((M, N), F32),
        compiler_params=_cparams(("parallel",)),
        name="mm_ln_full",
    )(a, w.astype(BF16), x, g.reshape(1, N), b.reshape(1, N))


def mm_ln(a, w, x, g, b, tm=512, tk=None):
    M, K = a.shape
    N = w.shape[1]
    tm = _pick(M, (tm, 256, 128, 64, 32, 16, 8))
    tk = tk or _pick(K, (1024, 1408, 512, 256, 128))
    nk = K // tk
    return pl.pallas_call(
        functools.partial(_mm_ln_kernel, nk=nk),
        grid=(M // tm, nk),
        in_specs=[pl.BlockSpec((tm, tk), lambda i, k: (i, k)),
                  pl.BlockSpec((tk, N), lambda i, k: (k, 0)),
                  pl.BlockSpec((tm, N), lambda i, k: (i, 0)),
                  pl.BlockSpec((1, N), lambda i, k: (0, 0)),
                  pl.BlockSpec((1, N), lambda i, k: (0, 0))],
        out_specs=pl.BlockSpec((tm, N), lambda i, k: (i, 0)),
        out_shape=jax.ShapeDtypeStruct((M, N), F32),
        scratch_shapes=[pltpu.VMEM((tm, N), F32)],
        compiler_params=_cparams(("parallel", "arbitrary")),
        name="mm_ln",
    )(a, w.astype(BF16), x, g.reshape(1, N), b.reshape(1, N))


def _ffn1_kernel(x_ref, xh_ref, wg_ref, wv_ref, cw_ref, cb_ref, o_ref, *, tiles_per_seq):
    m = pl.program_id(0)
    x = x_ref[...].astype(BF16)
    g = _dot(x, wg_ref[...])
    gh = _dot(xh_ref[...].astype(BF16), wg_ref[...])
    gh = jnp.where(m % tiles_per_seq == 0, 0.0, gh)
    rows = lax.broadcasted_iota(jnp.int32, g.shape, 0)
    g1 = jnp.where(rows == 0, gh[SUBLANES - 1:SUBLANES], pltpu.roll(g, 1, 0))
    g2 = jnp.where(rows == 0, gh[SUBLANES - 2:SUBLANES - 1],
                   jnp.where(rows == 1, gh[SUBLANES - 1:SUBLANES], pltpu.roll(g, 2, 0)))
    cw = cw_ref[...]
    h = g * cw[2:3] + g1 * cw[1:2] + g2 * cw[0:1] + cb_ref[...]
    v = _dot(x, wv_ref[...])
    o_ref[...] = (h * jax.nn.sigmoid(h) * v).astype(o_ref.dtype)


def ffn1(x, w_gate, w_val, conv_w, conv_b, seq, tm=1024, tf=512):
    M, D = x.shape
    Fd = w_gate.shape[1]
    tm = _pick(seq, (tm, 256, 128, 64, 32, 16, 8))
    tf = _pick(Fd, (tf, 256, 128))
    hb = tm // SUBLANES
    return pl.pallas_call(
        functools.partial(_ffn1_kernel, tiles_per_seq=seq // tm),
        grid=(M // tm, Fd // tf),
        in_specs=[pl.BlockSpec((tm, D), lambda i, j: (i, 0)),
                  pl.BlockSpec((SUBLANES, D), lambda i, j: (jnp.maximum(i * hb - 1, 0), 0)),
                  pl.BlockSpec((D, tf), lambda i, j: (0, j)),
                  pl.BlockSpec((D, tf), lambda i, j: (0, j)),
                  pl.BlockSpec((FFN_CONV, tf), lambda i, j: (0, j)),
                  pl.BlockSpec((1, tf), lambda i, j: (0, j))],
        out_specs=pl.BlockSpec((tm, tf), lambda i, j: (i, j)),
        out_shape=jax.ShapeDtypeStruct((M, Fd), BF16),
        compiler_params=_cparams(("parallel", "parallel")),
        name="ffn1",
    )(x, x, w_gate.astype(BF16), w_val.astype(BF16), conv_w, conv_b.reshape(1, Fd))


def _flash_kernel(q_ref, k_ref, v_ref, o_ref, m_ref, acc_ref, s_ref, *, tq, tk, ts, dv):
    qi = pl.program_id(2)
    nsub = tq // ts
    ndiag = tq // tk
    nfull = qi * ndiag
    m_ref[...] = jnp.full_like(m_ref, -jnp.inf)
    acc_ref[...] = jnp.zeros_like(acc_ref)

    def above_diagonal(r, d):
        return (r + 1) * ts <= d * tk

    def scores(r, blk):
        k = k_ref[0, 0, pl.ds(pl.multiple_of(blk * tk, tk), tk), :]
        return _dot_nt(q_ref[0, 0, r * ts:(r + 1) * ts, :], k)

    def consume(blk, diag, nxt_diag):
        v = v_ref[0, 0, pl.ds(pl.multiple_of(blk * tk, tk), tk), :]
        for r in range(nsub):
            rs = slice(r * ts, (r + 1) * ts)
            live = diag is None or not above_diagonal(r, diag)
            if live:
                s = s_ref[rs, :]
            if nxt_diag is None or (nxt_diag >= 0 and not above_diagonal(r, nxt_diag)):
                s_ref[rs, :] = scores(r, blk + 1)
            if not live:
                continue
            if diag is not None and r * ts < (diag + 1) * tk - 1:
                rowi = lax.broadcasted_iota(jnp.int32, s.shape, 0) + r * ts
                coli = lax.broadcasted_iota(jnp.int32, s.shape, 1) + diag * tk
                s = jnp.where(coli <= rowi, s, -jnp.inf)
            m_old = m_ref[rs, :]
            m_new = jnp.maximum(m_old, jnp.max(s, axis=-1, keepdims=True))
            p = jnp.exp2(s - m_new).astype(BF16)
            acc_ref[rs, :] = jnp.exp2(m_old - m_new) * acc_ref[rs, :] + _dot(p, v)
            m_ref[rs, :] = m_new

    for r in range(nsub):
        s_ref[r * ts:(r + 1) * ts, :] = scores(r, 0)

    def body(j, carry):
        consume(2 * j, None, None)
        consume(2 * j + 1, None, None)
        return carry

    lax.fori_loop(0, nfull // 2, body, 0)

    @pl.when(nfull % 2 == 1)
    def _():
        consume(nfull - 1, None, None)

    for d in range(ndiag):
        consume(nfull + d, d, d + 1 if d + 1 < ndiag else -1)
    acc = acc_ref[...]
    o_ref[...] = (acc[:, :dv] / acc[:, dv:]).astype(o_ref.dtype)


def flash_attention(q, k, v, tq=1024, tk=1024, ts=512):
    B, H, S, dq = q.shape
    dv = v.shape[-1] // 2
    tq = _pick(S, (tq, 512, 256, 128))
    tk = _pick(tq, (tk, 256, 128))
    nq = S // tq
    return pl.pallas_call(
        functools.partial(_flash_kernel, tq=tq, tk=tk, ts=_pick(tq, (ts, 128)), dv=dv),
        grid=(B, H, nq),
        in_specs=[pl.BlockSpec((1, 1, tq, dq), lambda b, h, i: (b, h, i, 0)),
                  pl.BlockSpec((1, 1, S, dq), lambda b, h, i: (b, h, 0, 0)),
                  pl.BlockSpec((1, 1, S, 2 * dv), lambda b, h, i: (b, h, 0, 0))],
        out_specs=pl.BlockSpec((tq, dv), lambda b, h, i: (b * nq + i, h)),
        out_shape=jax.ShapeDtypeStruct((B * S, H * dv), BF16),
        scratch_shapes=[pltpu.VMEM((tq, 1), F32), pltpu.VMEM((tq, 2 * dv), F32), pltpu.VMEM((tq, tk), F32)],
        compiler_params=_cparams(("parallel", "parallel", "arbitrary")),
        name="flash",
    )(q, k, v)


def _tri_inv_kernel(a_ref, t_ref):
    C = a_ref.shape[1]
    kk = lax.broadcasted_iota(jnp.int32, (SUBLANES, LANES), 0)
    zero = jnp.zeros((SUBLANES, LANES), F32)
    for i in range(C):
        ng = i // SUBLANES + 1
        acc = [jnp.where(kk == i - g * SUBLANES, 1.0, 0.0).astype(F32) for g in range(ng)]
        for j in range(i):
            a_ij = a_ref[0, i, j:j + 1, :]
            for g in range(j // SUBLANES + 1):
                acc[g] = acc[g] + a_ij * t_ref[0, j, g * SUBLANES:(g + 1) * SUBLANES, :]
        for g in range(C // SUBLANES):
            t_ref[0, i, g * SUBLANES:(g + 1) * SUBLANES, :] = acc[g] if g < ng else zero


def tri_inv(a):
    n_in, C, _ = a.shape
    ng = -(-n_in // LANES)
    N = ng * LANES
    a = jnp.pad(a, ((0, N - n_in), (0, 0), (0, 0)))
    al = a.reshape(ng, LANES, C * C).transpose(0, 2, 1).reshape(ng, C, C, LANES)
    tl = pl.pallas_call(
        _tri_inv_kernel,
        grid=(ng,),
        in_specs=[pl.BlockSpec((1, C, C, LANES), lambda i: (i, 0, 0, 0))],
        out_specs=pl.BlockSpec((1, C, C, LANES), lambda i: (i, 0, 0, 0)),
        out_shape=jax.ShapeDtypeStruct((ng, C, C, LANES), F32),
        compiler_params=_cparams(("parallel",)),
        name="tri_inv",
    )(al)
    return tl.reshape(ng, C * C, LANES).transpose(0, 2, 1).reshape(N, C, C)[:n_in]


def _tri_inv_pairs_kernel(a_ref, t_ref, al_ref, tl_ref):
    C = al_ref.shape[1]
    for i in range(C):
        mt = a_ref[pl.ds(i, LANES, stride=C), :].T
        al_ref[0, i] = mt[:C]
        al_ref[1, i] = mt[C:]
    for hh in range(2):
        _tri_inv_kernel(al_ref.at[pl.ds(hh, 1)], tl_ref.at[pl.ds(hh, 1)])
    for i in range(C):
        mt = jnp.concatenate([tl_ref[0, i], tl_ref[1, i]], axis=0)
        t_ref[pl.ds(i, LANES, stride=C), :] = mt.T


def tri_inv_pairs(a):
    n, m, C, _ = a.shape
    tiles = n * m
    ng = -(-tiles // LANES)
    a2 = jnp.pad(a.reshape(tiles * C, 2 * C), ((0, (ng * LANES - tiles) * C), (0, 0)))
    t = pl.pallas_call(
        _tri_inv_pairs_kernel,
        grid=(ng,),
        in_specs=[pl.BlockSpec((LANES * C, 2 * C), lambda i: (i, 0))],
        out_specs=pl.BlockSpec((LANES * C, 2 * C), lambda i: (i, 0)),
        out_shape=jax.ShapeDtypeStruct((ng * LANES * C, 2 * C), F32),
        scratch_shapes=[pltpu.VMEM((2, C, C, LANES), F32), pltpu.VMEM((2, C, C, LANES), F32)],
        compiler_params=_cparams(("parallel",)),
        name="tri_inv_pairs",
    )(a2)
    return t[:tiles * C].reshape(n, m, C, 2 * C)


def _chunk_tri(C, G):
    idx = jnp.arange(C * G)
    return ((idx[None, :] <= idx[:, None]) & (idx[None, :] // C == idx[:, None] // C)).astype(BF16)


def _chunk_masks():
    C = CHUNK
    row = lax.broadcasted_iota(jnp.int32, (C, C), 0)
    col = lax.broadcasted_iota(jnp.int32, (C, C), 1)
    return row, col


def _rwkv_a_kernel(r_ref, lw_ref, k_ref, v_ref, av_ref, bv_ref, tri_ref,
                   aab_ref, mrb_ref, at_ref, akv_ref, rt_ref, bh_ref, yk_ref, kv_ref, egl_ref, *, G):
    C = CHUNK
    rowp = lax.broadcasted_iota(jnp.int32, (C, LANES), 0)
    lane = lax.broadcasted_iota(jnp.int32, (C, LANES), 1)
    strict = lane % C < rowp
    incl = lane % C <= rowp
    head0 = lane < RWKV_HEAD
    bd = (lax.broadcasted_iota(jnp.int32, (LANES, LANES), 0) // RWKV_HEAD ==
          lax.broadcasted_iota(jnp.int32, (LANES, LANES), 1) // RWKV_HEAD)

    def by_head(x):
        return jnp.concatenate([jnp.where(head0, x, 0.0), jnp.where(head0, 0.0, x)], axis=0).astype(BF16)

    lw_all = lw_ref[...]
    gc_all = _dot_exact_lhs(tri_ref[...], lw_all)
    sls = [slice(g * C, (g + 1) * C) for g in range(G)]
    at, rt, vb, egl, gl, xb, xk = [], [], [], [], [], [], []
    for sl in sls:
        gc = gc_all[sl]
        gl.append(gc[C - 1:C, :])
        ieg = jnp.exp(-gc)
        egl.append(jnp.exp(gl[-1] - gc))
        at.append(av_ref[sl, :] * jnp.exp(gc - lw_all[sl]))
        rt.append(r_ref[sl, :] * jnp.exp(gc))
        vb.append(by_head(v_ref[sl, :]))
        lhs = jnp.concatenate([at[-1], rt[-1]], axis=0).astype(BF16)
        xb.append(_dot_nt(lhs, by_head(bv_ref[sl, :] * ieg)))
        xk.append(_dot_nt(lhs, by_head(k_ref[sl, :] * ieg)))
    for g, sl in enumerate(sls):
        aab_ref[g, 0] = jnp.where(strict, xb[g][:C], 0.0)
        mrb_ref[g, 0] = jnp.where(incl, xb[g][C:], 0.0).astype(BF16)
        akv_ref[sl, :] = _dot(jnp.where(strict, xk[g][:C], 0.0).astype(BF16), vb[g]).astype(BF16)
        yk_ref[sl, :] = _dot(jnp.where(incl, xk[g][C:], 0.0).astype(BF16), vb[g])
    for g, sl in enumerate(sls):
        at_ref[sl, :] = at[g].astype(BF16)
        rt_ref[sl, :] = rt[g].astype(BF16)
        bh_ref[sl, :] = (bv_ref[sl, :] * egl[g]).astype(BF16)
        kv_ref[g, 0] = jnp.where(bd, _dot_tn(v_ref[sl, :].astype(BF16), (k_ref[sl, :] * egl[g]).astype(BF16)), 0.0)
        egl_ref[g] = jnp.exp(gl[g])


def _rwkv_c_kernel(at_ref, akv_ref, rt_ref, bh_ref, yk_ref, t_ref, mrb_ref, kv_ref, egl_ref,
                   y_ref, s_ref, *, G):
    C = CHUNK
    npair = RWKV_HEADS // 2

    @pl.when(pl.program_id(1) == 0)
    def _():
        s_ref[...] = jnp.zeros_like(s_ref)

    lane = lax.broadcasted_iota(jnp.int32, (C, LANES), 1)
    head0 = lane < RWKV_HEAD
    head0w = jnp.concatenate([head0, head0], axis=1)
    bd = (lax.broadcasted_iota(jnp.int32, (LANES, LANES), 0) // RWKV_HEAD ==
          lax.broadcasted_iota(jnp.int32, (LANES, LANES), 1) // RWKV_HEAD)
    zero = jnp.zeros((), BF16)

    def by_head(x, m):
        return jnp.concatenate([jnp.where(m, x, zero), jnp.where(m, zero, x)], axis=0)

    def body(g, carry):
        sl = pl.ds(pl.multiple_of(g * C, C), C)
        cols = [slice(p * LANES, (p + 1) * LANES) for p in range(npair)]
        w, s, sb, u, ub, y = [], [], [], [], [], []
        for p, cs in enumerate(cols):
            both = jnp.concatenate([at_ref[sl, cs], akv_ref[sl, cs]], axis=1)
            w.append(_dot(t_ref[g, p].astype(BF16), by_head(both, head0w)))
            s.append(s_ref[p])
            sb.append(s[p].astype(BF16))
        for p, cs in enumerate(cols):
            u.append(_dot_nt(w[p][:, :LANES].astype(BF16), sb[p]) + w[p][:, LANES:])
            ub.append(u[p].astype(BF16))
            y.append(_dot_nt(rt_ref[sl, cs], sb[p]) + yk_ref[sl, cs])
        for p, cs in enumerate(cols):
            y_ref[sl, cs] = y[p] + _dot(mrb_ref[g, p], by_head(ub[p], head0))
            s_ref[p] = (s[p] * egl_ref[g, :, cs] + jnp.where(bd, _dot_tn(ub[p], bh_ref[sl, cs]), 0.0)
                        + kv_ref[g, p])
        return carry

    lax.fori_loop(0, G, body, 0)


def rwkv7_chunked(p, r_blk, v_blk, lw, k, av, bv, batch, G=4, Ga=4):
    T, HD = lw.shape
    C = CHUNK
    seq = T // batch
    nct = T // C
    G = _pick(seq // C, (G, 2, 1))
    Ga = _pick(seq // C, (Ga, 4, 2, 1))
    npair = HD // LANES
    row_spec = pl.BlockSpec((C * Ga, LANES), lambda i, p: (i, p))
    pair_mat = pl.BlockSpec((Ga, 1, C, LANES), lambda i, p: (i, p, 0, 0))
    outs = pl.pallas_call(
        functools.partial(_rwkv_a_kernel, G=Ga),
        grid=(nct // Ga, npair),
        in_specs=[pl.BlockSpec((C * Ga, LANES), lambda i, p: (i, r_blk + p)), row_spec, row_spec,
                  pl.BlockSpec((C * Ga, LANES), lambda i, p: (i, v_blk + p)), row_spec, row_spec,
                  pl.BlockSpec((C * Ga, C * Ga), lambda i, p: (0, 0))],
        out_specs=[pair_mat, pair_mat,
                   row_spec, row_spec, row_spec, row_spec, row_spec,
                   pl.BlockSpec((Ga, 1, LANES, LANES), lambda i, p: (i, p, 0, 0)),
                   pl.BlockSpec((Ga, 1, LANES), lambda i, p: (i, 0, p))],
        out_shape=[jax.ShapeDtypeStruct((nct, npair, C, LANES), F32),
                   jax.ShapeDtypeStruct((nct, npair, C, LANES), BF16),
                   jax.ShapeDtypeStruct((T, HD), BF16),
                   jax.ShapeDtypeStruct((T, HD), BF16),
                   jax.ShapeDtypeStruct((T, HD), BF16),
                   jax.ShapeDtypeStruct((T, HD), BF16),
                   jax.ShapeDtypeStruct((T, HD), F32),
                   jax.ShapeDtypeStruct((nct, npair, LANES, LANES), F32),
                   jax.ShapeDtypeStruct((nct, 1, HD), F32)],
        compiler_params=_cparams(("parallel", "parallel")),
        name="rwkv_a",
    )(p, lw, k, p, av, bv, _chunk_tri(C, Ga))
    aab, mrb, at, akv, rt, bh, yk, kv, egl = outs
    tinv = tri_inv_pairs(aab)
    ncb = seq // C // G
    wide = pl.BlockSpec((C * G, HD), lambda b, n: (b * ncb + n, 0))
    return pl.pallas_call(
        functools.partial(_rwkv_c_kernel, G=G),
        grid=(batch, ncb),
        in_specs=[wide, wide, wide, wide, wide,
                  pl.BlockSpec((G, npair, C, LANES), lambda b, n: (b * ncb + n, 0, 0, 0)),
                  pl.BlockSpec((G, npair, C, LANES), lambda b, n: (b * ncb + n, 0, 0, 0)),
                  pl.BlockSpec((G, npair, LANES, LANES), lambda b, n: (b * ncb + n, 0, 0, 0)),
                  pl.BlockSpec((G, 1, HD), lambda b, n: (b * ncb + n, 0, 0))],
        out_specs=wide,
        out_shape=jax.ShapeDtypeStruct((T, HD), F32),
        scratch_shapes=[pltpu.VMEM((npair, LANES, LANES), F32)],
        compiler_params=_cparams(("arbitrary", "arbitrary")),
        name="rwkv_c",
    )(at, akv, rt, bh, yk, tinv, mrb, kv, egl)


def _gdn_a_kernel(q_ref, k_ref, v_ref, ga_ref, gb_ref, grow_ref, ba_ref, bb_ref, tri_ref, triu_ref,
                  xm_ref, attn_ref, vb_ref, kcp_ref, qd_ref, kd_ref, egl_ref, *, G):
    C = CHUNK
    n = C * G
    rowp = lax.broadcasted_iota(jnp.int32, (C, LANES), 0)
    lane = lax.broadcasted_iota(jnp.int32, (C, LANES), 1)
    strict = lane % C < rowp
    incl = lane % C <= rowp
    head_a = lane < C
    wide = lambda a_ref, b_ref: jnp.concatenate([jnp.broadcast_to(a_ref[0], (n, LANES)),
                                                 jnp.broadcast_to(b_ref[0], (n, LANES))], axis=1)
    gcw = _dot_exact_lhs(tri_ref[...], wide(ga_ref, gb_ref))
    hi, mid, lo = _split3(grow_ref[0, 0])
    triu = triu_ref[...]
    gcr_all = _dot(hi, triu) + _dot(mid, triu) + _dot(lo, triu)
    betaw = wide(ba_ref, bb_ref)
    sls = [slice(g * C, (g + 1) * C) for g in range(G)]
    kk, qk = [], []
    for sl in sls:
        kbf = k_ref[sl, :].astype(BF16)
        k2 = jnp.concatenate([kbf, kbf], axis=0)
        kk.append(_dot_nt(kbf, k2))
        qk.append(_dot_nt(q_ref[sl, :].astype(BF16), k2))
    for g, sl in enumerate(sls):
        gc = gcw[sl]
        bw = betaw[sl]
        diff = jnp.where(head_a, gc[:, :LANES], gc[:, LANES:]) - gcr_all[g:g + 1]
        decay = jnp.where(incl, jnp.exp(jnp.where(incl, diff, 0.0)), 0.0)
        bp = jnp.where(head_a, bw[:, :LANES], bw[:, LANES:])
        xm_ref[g, 0] = jnp.where(strict, -(kk[g] * bp) * decay, 0.0)
        attn_ref[g, 0] = jnp.where(incl, qk[g] * decay, 0.0).astype(BF16)
        eg = jnp.exp(gc)
        gl = gc[C - 1:C, :]
        k = k_ref[sl, :]
        q = q_ref[sl, :]
        k2w = jnp.concatenate([k, k], axis=1)
        vb_ref[sl, :] = (v_ref[sl, :] * bw).astype(BF16)
        kcp_ref[sl, :] = (k2w * bw * eg).astype(BF16)
        qd_ref[sl, :] = (jnp.concatenate([q, q], axis=1) * eg).astype(BF16)
        kd_ref[sl, :] = (k2w * jnp.exp(gl - gc)).astype(BF16)
        egl_ref[g] = jnp.exp(gl)


def _gdn_c_kernel(vb_ref, kcp_ref, qd_ref, kd_ref, t_ref, attn_ref, egl_ref, o_ref, s_ref, *, G):
    C = CHUNK
    npair = GDN_V_HEADS // 2

    @pl.when(pl.program_id(1) == 0)
    def _():
        s_ref[...] = jnp.zeros_like(s_ref)

    def blockdiag(x0, x1):
        z = jnp.zeros_like(x0)
        return jnp.concatenate([jnp.concatenate([x0, z], axis=1), jnp.concatenate([z, x1], axis=1)], axis=0)

    def body(g, carry):
        sl = pl.ds(pl.multiple_of(g * C, C), C)
        cols = [slice(h * LANES, (h + 1) * LANES) for h in range(GDN_V_HEADS)]
        both = [jnp.concatenate([vb_ref[sl, cs], kcp_ref[sl, cs]], axis=1) for cs in cols]
        w, s, sb, vnb, o = [], [], [], [], []
        for j in range(npair):
            wj = _dot(t_ref[g, j].astype(BF16), blockdiag(both[2 * j], both[2 * j + 1]))
            w += [wj[:, :2 * LANES], wj[:, 2 * LANES:]]
        for h in range(GDN_V_HEADS):
            s.append(s_ref[h])
            sb.append(s[h].astype(BF16))
        for h, cs in enumerate(cols):
            vnb.append((w[h][:, :LANES] - _dot(w[h][:, LANES:].astype(BF16), sb[h])).astype(BF16))
            o.append(_dot(qd_ref[sl, cs], sb[h]))
        for j in range(npair):
            att = _dot(attn_ref[g, j], blockdiag(vnb[2 * j], vnb[2 * j + 1]))
            for hh in range(2):
                h = 2 * j + hh
                o_ref[sl, cols[h]] = o[h] + att[:, hh * LANES:(hh + 1) * LANES]
                s_ref[h] = s[h] * egl_ref[g, :, cols[h]] + _dot_tn(kd_ref[sl, cols[h]], vnb[h])
        return carry

    lax.fori_loop(0, G, body, 0)


def gdn_chunked(q, k, v, g, beta, batch, G=4, Ga=8):
    T, VD = v.shape
    C = CHUNK
    seq = T // batch
    nct = T // C
    G = _pick(seq // C, (G, 2, 1))
    Ga = _pick(seq // C, (Ga, 4, 2, 1))
    nh = GDN_V_HEADS
    nq = GDN_QK_HEADS
    assert nh == 2 * nq and 2 * C == LANES
    col = lambda t, hh: t[:, hh::2].T.reshape(nq, T, 1)
    grow = g.T.reshape(nq, 2, nct // Ga, Ga, C).transpose(0, 2, 3, 1, 4).reshape(nq, nct // Ga, Ga, LANES)
    grow = jnp.pad(grow, ((0, 0), (0, 0), (0, SUBLANES - Ga), (0, 0)))
    lidx = jnp.arange(LANES)
    triu2 = ((lidx[:, None] <= lidx[None, :]) & (lidx[:, None] // C == lidx[None, :] // C)).astype(BF16)
    qk_spec = pl.BlockSpec((C * Ga, LANES), lambda i, j: (i, j))
    v_spec = pl.BlockSpec((C * Ga, 2 * LANES), lambda i, j: (i, j))
    col_spec = pl.BlockSpec((1, C * Ga, 1), lambda i, j: (j, i, 0))
    pair_mat = pl.BlockSpec((Ga, 1, C, LANES), lambda i, j: (i, j, 0, 0))
    outs = pl.pallas_call(
        functools.partial(_gdn_a_kernel, G=Ga),
        grid=(nct // Ga, nq),
        in_specs=[qk_spec, qk_spec, v_spec, col_spec, col_spec,
                  pl.BlockSpec((1, 1, SUBLANES, LANES), lambda i, j: (j, i, 0, 0)), col_spec, col_spec,
                  pl.BlockSpec((C * Ga, C * Ga), lambda i, j: (0, 0)),
                  pl.BlockSpec((LANES, LANES), lambda i, j: (0, 0))],
        out_specs=[pair_mat, pair_mat,
                   v_spec, v_spec, v_spec, v_spec,
                   pl.BlockSpec((Ga, 1, 2 * LANES), lambda i, j: (i, 0, j))],
        out_shape=[jax.ShapeDtypeStruct((nct, nq, C, LANES), F32),
                   jax.ShapeDtypeStruct((nct, nq, C, LANES), BF16),
                   jax.ShapeDtypeStruct((T, VD), BF16),
                   jax.ShapeDtypeStruct((T, VD), BF16),
                   jax.ShapeDtypeStruct((T, VD), BF16),
                   jax.ShapeDtypeStruct((T, VD), BF16),
                   jax.ShapeDtypeStruct((nct, 1, VD), F32)],
        compiler_params=_cparams(("parallel", "parallel")),
        name="gdn_a",
    )(q, k, v, col(g, 0), col(g, 1), grow, col(beta, 0), col(beta, 1), _chunk_tri(C, Ga), triu2)
    xm, attn, vb, kcp, qd, kd, egl = outs
    tinv = tri_inv_pairs(xm)
    ncb = seq // C // G
    wide = pl.BlockSpec((C * G, VD), lambda b, n: (b * ncb + n, 0))
    mat = pl.BlockSpec((G, nq, C, LANES), lambda b, n: (b * ncb + n, 0, 0, 0))
    return pl.pallas_call(
        functools.partial(_gdn_c_kernel, G=G),
        grid=(batch, ncb),
        in_specs=[wide, wide, wide, wide, mat, mat,
                  pl.BlockSpec((G, 1, VD), lambda b, n: (b * ncb + n, 0, 0))],
        out_specs=wide,
        out_shape=jax.ShapeDtypeStruct((T, VD), F32),
        scratch_shapes=[pltpu.VMEM((nh, GDN_DK, GDN_DV), F32)],
        compiler_params=_cparams(("arbitrary", "arbitrary")),
        name="gdn_c",
    )(vb, kcp, qd, kd, tinv, attn, egl)


def _rope_pair(x, ct, st, low):
    swapped = jnp.where(low, pltpu.roll(x, LANES - RET_DK // 2, 1), pltpu.roll(x, RET_DK // 2, 1))
    return x * ct + swapped * st


def _ret_kernel(q_ref, k_ref, v_ref, ct_ref, st_ref, o_ref, s_ref, *, G):
    C = CHUNK

    @pl.when(pl.program_id(1) == 0)
    def _():
        s_ref[...] = jnp.zeros_like(s_ref)

    row, col = _chunk_masks()
    incl = col <= row
    dif = jnp.maximum(row - col, 0).astype(F32)
    lane = lax.broadcasted_iota(jnp.int32, (C, LANES), 1)
    head0 = lane < RET_DK
    low = lane % RET_DK < RET_DK // 2
    ridx = lax.broadcasted_iota(jnp.int32, (C, LANES), 0).astype(F32)

    def body(g, carry):
        sl = pl.ds(pl.multiple_of(g * C, C), C)
        ct, st = ct_ref[sl, :], st_ref[sl, :]
        lgs = [math.log(1.0 - 2.0 ** (-5.0 - h)) for h in range(RET_HEADS)]
        vss = [slice(h * RET_DV, (h + 1) * RET_DV) for h in range(RET_HEADS)]
        sc, cross, upd, vb = [], [], [], []
        for p in range(RET_HEADS // 2):
            ps = slice(p * LANES, (p + 1) * LANES)
            qp = _rope_pair(q_ref[sl, ps], ct, st, low) * RET_DK ** -0.5
            kp = _rope_pair(k_ref[sl, ps], ct, st, low)
            kb = kp.astype(BF16)
            for hh in range(2):
                h = 2 * p + hh
                msk = head0 if hh == 0 else jnp.logical_not(head0)
                q = jnp.where(msk, qp, 0.0)
                vb.append(v_ref[sl, vss[h]].astype(BF16))
                sc.append(_dot_nt(q.astype(BF16), kb))
                qdec = (q * jnp.exp((ridx + 1.0) * lgs[h])).astype(BF16)
                cross.append(_dot(qdec, s_ref[h].astype(BF16)))
                kdec = (jnp.where(msk, kp, 0.0) * jnp.exp((C - 1.0 - ridx) * lgs[h])).astype(BF16)
                upd.append(_dot_tn(kdec, vb[h]))
        for h in range(RET_HEADS):
            dmat = jnp.where(incl, jnp.exp(dif * lgs[h]), 0.0)
            o_ref[sl, vss[h]] = _dot((sc[h] * dmat).astype(BF16), vb[h]) + cross[h]
            s_ref[h] = s_ref[h] * math.exp(C * lgs[h]) + upd[h]
        return carry

    lax.fori_loop(0, G, body, 0)


def retention_chunked(p, q_blk, k_blk, v_blk, ct, st, batch, G=4):
    T = p.shape[0]
    QD, VD = RET_HEADS * RET_DK, RET_HEADS * RET_DV
    C = CHUNK
    seq = T // batch
    G = _pick(seq // C, (G, 2, 1))
    ncb = seq // C // G
    tab = pl.BlockSpec((C * G, LANES), lambda b, n: (b * ncb + n, 0))
    wide = pl.BlockSpec((C * G, VD), lambda b, n: (b * ncb + n, 0))
    return pl.pallas_call(
        functools.partial(_ret_kernel, G=G),
        grid=(batch, ncb),
        in_specs=[pl.BlockSpec((C * G, QD), lambda b, n: (b * ncb + n, q_blk)),
                  pl.BlockSpec((C * G, QD), lambda b, n: (b * ncb + n, k_blk)),
                  pl.BlockSpec((C * G, VD), lambda b, n: (b * ncb + n, v_blk)),
                  tab, tab],
        out_specs=wide,
        out_shape=jax.ShapeDtypeStruct((T, VD), F32),
        scratch_shapes=[pltpu.VMEM((RET_HEADS, LANES, RET_DV), F32)],
        compiler_params=_cparams(("arbitrary", "arbitrary")),
        name="retention",
    )(p, p, p, ct, st)


def _rope_tables(positions, d):
    inv = ROPE_THETA ** (-jnp.arange(0, d, 2, dtype=F32) / d)
    ang = positions.astype(F32).reshape(-1, 1) * inv
    cos, sin = jnp.cos(ang), jnp.sin(ang)
    reps = LANES // d
    return jnp.concatenate([cos, cos] * reps, -1), jnp.concatenate([-sin, sin] * reps, -1)


L0_MLA_W = 768
L0_LORA_W = 384
L0_N = 3 * RWKV_DIM + L0_MLA_W + L0_LORA_W
CQ_PAD = 512


def _proj_shift_kernel(x_ref, xh_ref, w_ref, mu_ref, o_ref, *, tiles_per_seq):
    m = pl.program_id(0)
    w = w_ref[...]
    p = _dot(x_ref[...].astype(BF16), w)
    ph = _dot(xh_ref[...].astype(BF16), w)
    ph = jnp.where(m % tiles_per_seq == 0, 0.0, ph)
    rows = lax.broadcasted_iota(jnp.int32, p.shape, 0)
    prev = jnp.where(rows == 0, ph[SUBLANES - 1:SUBLANES], pltpu.roll(p, 1, 0))
    o_ref[...] = p + mu_ref[...] * (prev - p)


def proj_shift(x, w, mu, seq, tm=1024, tn=1408):
    M, D = x.shape
    N = w.shape[1]
    tm = _pick(seq, (tm, 256, 128, 64, 32, 16, 8))
    tn = _pick(N, (tn, 512, 384, 256, 128))
    hb = tm // SUBLANES
    return pl.pallas_call(
        functools.partial(_proj_shift_kernel, tiles_per_seq=seq // tm),
        grid=(M // tm, N // tn),
        in_specs=[pl.BlockSpec((tm, D), lambda i, j: (i, 0)),
                  pl.BlockSpec((SUBLANES, D), lambda i, j: (jnp.maximum(i * hb - 1, 0), 0)),
                  pl.BlockSpec((D, tn), lambda i, j: (0, j)),
                  pl.BlockSpec((1, tn), lambda i, j: (0, j))],
        out_specs=pl.BlockSpec((tm, tn), lambda i, j: (i, j)),
        out_shape=jax.ShapeDtypeStruct((M, N), F32),
        compiler_params=_cparams(("parallel", "parallel")),
        name="proj_shift",
    )(x, x, w, mu.reshape(1, N))


def _mla_pre_kernel(p_ref, ct_ref, st_ref, qn_ref, kn_ref, wq_ref, wkv_ref, q_ref, k_ref, v_ref):
    p = p_ref[...]
    cq = p[:, :CQ_PAD]
    ckv = p[:, CQ_PAD:CQ_PAD + MLA_KV_RANK]
    kp = p[:, CQ_PAD + MLA_KV_RANK:]
    cqn = cq * lax.rsqrt(jnp.sum(cq * cq, -1, keepdims=True) * (1.0 / MLA_Q_RANK) + RMS_EPS) * qn_ref[...]
    ckvn = ckv * lax.rsqrt(jnp.mean(ckv * ckv, -1, keepdims=True) + RMS_EPS) * kn_ref[...]
    qa = _dot(cqn.astype(BF16), wq_ref[...])
    kva = _dot(ckvn.astype(BF16), wkv_ref[...])
    ct, st = ct_ref[...], st_ref[...]
    kr = (kp * ct + pltpu.roll(kp, MLA_ROPE, 1) * st)[:, :MLA_ROPE].astype(BF16)
    ones = jnp.ones((p.shape[0], MLA_V), BF16)
    hw = 2 * LANES
    for h in range(MLA_HEADS):
        qh = qa[:, h * hw:(h + 1) * hw]
        pe = qh[:, MLA_NOPE:]
        q_ref[0, h, :, :MLA_NOPE] = qh[:, :MLA_NOPE].astype(BF16)
        q_ref[0, h, :, MLA_NOPE:] = (pe * ct + pltpu.roll(pe, MLA_ROPE, 1) * st)[:, :MLA_ROPE].astype(BF16)
        kvh = kva[:, h * hw:(h + 1) * hw]
        k_ref[0, h, :, :MLA_NOPE] = kvh[:, :MLA_NOPE].astype(BF16)
        k_ref[0, h, :, MLA_NOPE:] = kr
        v_ref[0, h, :, :MLA_V] = kvh[:, MLA_NOPE:].astype(BF16)
        v_ref[0, h, :, MLA_V:] = ones


def mla_pre(p, mla_blk, ct, st, qn, kn, wq, wkv, batch, tm=512):
    T = p.shape[0]
    seq = T // batch
    tm = _pick(seq, (tm, 256, 128))
    nt = seq // tm
    dqk = MLA_NOPE + MLA_ROPE
    head_spec = lambda d: pl.BlockSpec((1, MLA_HEADS, tm, d), lambda i: (i // nt, 0, i % nt, 0))
    full = lambda a: pl.BlockSpec(a.shape, lambda i: (0,) * a.ndim)
    return pl.pallas_call(
        _mla_pre_kernel,
        grid=(T // tm,),
        in_specs=[pl.BlockSpec((tm, L0_MLA_W), lambda i: (i, mla_blk)),
                  pl.BlockSpec((tm, LANES), lambda i: (i, 0)),
                  pl.BlockSpec((tm, LANES), lambda i: (i, 0)),
                  full(qn), full(kn), full(wq), full(wkv)],
        out_specs=[head_spec(dqk), head_spec(dqk), head_spec(2 * MLA_V)],
        out_shape=[jax.ShapeDtypeStruct((batch, MLA_HEADS, seq, dqk), BF16),
                   jax.ShapeDtypeStruct((batch, MLA_HEADS, seq, dqk), BF16),
                   jax.ShapeDtypeStruct((batch, MLA_HEADS, seq, 2 * MLA_V), BF16)],
        compiler_params=_cparams(("parallel",)),
        name="mla_pre",
    )(p, ct, st, qn, kn, wq, wkv)


def _softplus(x):
    return jnp.maximum(x, 0.0) + jnp.log(1.0 + jnp.exp(-jnp.abs(x)))


def _group_sum(x, ones_bd, terms):
    parts = _split3(x)[:terms]
    out = _dot(parts[0], ones_bd)
    for t in parts[1:]:
        out = out + _dot(t, ones_bd)
    return out


def _rwkv_pre_kernel(k_ref, lora_ref, wl_ref, g2_ref, w0_ref, a0_ref, kk_ref, ka_ref, ones_ref,
                     lw_ref, k2_ref, av_ref, bv_ref, g_ref):
    lora = lora_ref[...]
    lane = lax.broadcasted_iota(jnp.int32, (lora.shape[0], LANES), 1)
    wa_in = jnp.where(lane < DECAY_LORA, jnp.tanh(lora[:, :LANES]), lora[:, :LANES])
    wa = _dot(wa_in.astype(BF16), wl_ref[...])
    g_ref[...] = _dot(jax.nn.sigmoid(lora[:, LANES:]).astype(BF16), g2_ref[...])
    w_log = -_softplus(-(w0_ref[...] + wa[:, :RWKV_DIM])) - 0.5
    lw_ref[...] = -jnp.exp(w_log)
    a = jax.nn.sigmoid(a0_ref[...] + wa[:, RWKV_DIM:])
    k = k_ref[...]
    kk = k * kk_ref[...]
    kkn = kk * lax.rsqrt(_group_sum(kk * kk, ones_ref[...], 2) + 1e-6)
    k2_ref[...] = k * (1.0 + (a - 1.0) * ka_ref[...])
    av_ref[...] = -kkn
    bv_ref[...] = kkn * a


def rwkv_pre(p, k_blk, lora_blk, wl, g2, w0, a0, kk, ka, ones_bd, tm=512):
    T = p.shape[0]
    tm = _pick(T, (tm, 256, 128, 64))
    row = pl.BlockSpec((tm, RWKV_DIM), lambda i: (i, 0))
    full = lambda a: pl.BlockSpec(a.shape, lambda i: (0,) * a.ndim)
    vec = lambda a: a.reshape(1, RWKV_DIM)
    args = (wl, g2, vec(w0), vec(a0), vec(kk), vec(ka), ones_bd)
    return pl.pallas_call(
        _rwkv_pre_kernel,
        grid=(T // tm,),
        in_specs=[pl.BlockSpec((tm, RWKV_DIM), lambda i: (i, k_blk)),
                  pl.BlockSpec((tm, L0_LORA_W), lambda i: (i, lora_blk))] + [full(a) for a in args],
        out_specs=[row] * 5,
        out_shape=[jax.ShapeDtypeStruct((T, RWKV_DIM), F32)] * 5,
        compiler_params=_cparams(("parallel",)),
        name="rwkv_pre",
    )(p, p, *args)


def _rwkv_post_kernel(y_ref, r_ref, k2_ref, v_ref, g_ref, rk_ref, gw_ref, gb_ref, ones_ref, o_ref):
    ones_bd = ones_ref[...]
    y = y_ref[...]
    yc = y - _group_sum(y, ones_bd, 2) * (1.0 / RWKV_HEAD)
    var = _group_sum(yc * yc, ones_bd, 1) * (1.0 / RWKV_HEAD)
    yn = yc * lax.rsqrt(var + RWKV_GN_EPS) * gw_ref[...] + gb_ref[...]
    bonus = _group_sum(r_ref[...] * k2_ref[...] * rk_ref[...], ones_bd, 2) * v_ref[...]
    o_ref[...] = ((yn + bonus) * g_ref[...]).astype(o_ref.dtype)


def rwkv_post(y, p, r_blk, v_blk, k2, g, rk, gw, gb, ones_bd, tm=512):
    T = y.shape[0]
    tm = _pick(T, (tm, 256, 128, 64))
    row = pl.BlockSpec((tm, RWKV_DIM), lambda i: (i, 0))
    full = lambda a: pl.BlockSpec(a.shape, lambda i: (0,) * a.ndim)
    vec = lambda a: a.reshape(1, RWKV_DIM)
    args = (vec(rk), vec(gw), vec(gb), ones_bd)
    return pl.pallas_call(
        _rwkv_post_kernel,
        grid=(T // tm,),
        in_specs=[row, pl.BlockSpec((tm, RWKV_DIM), lambda i: (i, r_blk)), row,
                  pl.BlockSpec((tm, RWKV_DIM), lambda i: (i, v_blk)), row] + [full(a) for a in args],
        out_specs=row,
        out_shape=jax.ShapeDtypeStruct((T, RWKV_DIM), BF16),
        compiler_params=_cparams(("parallel",)),
        name="rwkv_post",
    )(y, p, k2, p, g, *args)


def _mm2_ln_kernel(a1_ref, a2_ref, w1_ref, w2_ref, x_ref, g_ref, b_ref, o_ref):
    y = ALPHA * x_ref[...] + _dot(a1_ref[...], w1_ref[...]) + _dot(a2_ref[...], w2_ref[...])
    mu = jnp.mean(y, axis=-1, keepdims=True)
    yc = y - mu
    var = jnp.mean(yc * yc, axis=-1, keepdims=True)
    o_ref[...] = yc * lax.rsqrt(var + LN_EPS) * g_ref[...] + b_ref[...]


def mm2_ln(a1, a2, w, x, g, b, tm=512):
    M, K1 = a1.shape
    K2 = a2.shape[1]
    N = w.shape[1]
    tm = _pick(M, (tm, 256, 128, 64, 32, 16, 8))
    w = w.astype(BF16)
    row = lambda k: pl.BlockSpec((tm, k), lambda i: (i, 0))
    return pl.pallas_call(
        _mm2_ln_kernel,
        grid=(M // tm,),
        in_specs=[row(K1), row(K2),
                  pl.BlockSpec((K1, N), lambda i: (0, 0)),
                  pl.BlockSpec((K2, N), lambda i: (0, 0)),
                  row(N),
                  pl.BlockSpec((1, N), lambda i: (0, 0)),
                  pl.BlockSpec((1, N), lambda i: (0, 0))],
        out_specs=row(N),
        out_shape=jax.ShapeDtypeStruct((M, N), F32),
        compiler_params=_cparams(("parallel",)),
        name="mm2_ln",
    )(a1, a2, w[:K1], w[K1:], x, g.reshape(1, N), b.reshape(1, N))


def _half_swap(n):
    return jnp.concatenate([jnp.arange(n // 2, n), jnp.arange(0, n // 2)])


def _mla_rwkv_mixer(xb, B, S, ct, st, w_in, q_norm, w_uq, kv_norm, w_ukv, rwkv_mu, rwkv_w0, rwkv_w2,
                    rwkv_a0, rwkv_a2, rwkv_g2, rwkv_k_k, rwkv_k_a, rwkv_r_k, rwkv_gn_w, rwkv_gn_b,
                    w_out, ln_g, ln_b):
    T = B * S
    D = xb.shape[1]
    o_kv = MLA_Q_RANK
    o_pe = o_kv + MLA_KV_RANK
    o_r = o_pe + MLA_ROPE
    o_k, o_v = o_r + RWKV_DIM, o_r + 2 * RWKV_DIM
    o_xw = o_r + 3 * RWKV_DIM
    o_xg = o_xw + DECAY_LORA + AAA_LORA
    zc = lambda n: jnp.zeros((D, n), F32)
    w_pe = w_in[:, o_pe:o_r]
    w0p = jnp.concatenate([w_in[:, o_r:o_xw], w_in[:, :o_kv], zc(CQ_PAD - MLA_Q_RANK), w_in[:, o_kv:o_pe],
                           w_pe, w_pe[:, _half_swap(MLA_ROPE)], w_in[:, o_xw:], zc(L0_N - L0_MLA_W - 3 * RWKV_DIM
                                                                                   - (w_in.shape[1] - o_xw))],
                          axis=1).astype(BF16)
    mu = rwkv_mu
    mu_p = jnp.concatenate([mu[:3 * RWKV_DIM], jnp.zeros((L0_MLA_W,), F32), mu[3 * RWKV_DIM:],
                            jnp.zeros((L0_N - L0_MLA_W - mu.shape[0],), F32)])
    p = proj_shift(xb, w0p, mu_p, S)
    dqk = MLA_NOPE + MLA_ROPE
    qs = dqk ** -0.5 * math.log2(math.e)
    wq = w_uq.reshape(MLA_Q_RANK, MLA_HEADS, dqk) * qs
    wq = jnp.concatenate([wq, wq[..., MLA_NOPE:][..., _half_swap(MLA_ROPE)]], -1).reshape(MLA_Q_RANK, -1)
    wq = jnp.pad(wq, ((0, CQ_PAD - MLA_Q_RANK), (0, 0))).astype(BF16)
    qn = jnp.pad(q_norm, (0, CQ_PAD - MLA_Q_RANK)).reshape(1, CQ_PAD)
    q, k, v = mla_pre(p, 3 * RWKV_DIM // L0_MLA_W, ct, st, qn, kv_norm.reshape(1, -1), wq,
                      w_ukv.astype(BF16), B)
    o_mla = flash_attention(q, k, v)
    zl = jnp.zeros((DECAY_LORA, RWKV_DIM), F32)
    wl = jnp.concatenate([jnp.concatenate([rwkv_w2, zl], 1), jnp.concatenate([zl, rwkv_a2], 1)], 0).astype(BF16)
    g2 = jnp.pad(rwkv_g2, ((0, L0_LORA_W - LANES - GATE_LORA), (0, 0))).astype(BF16)
    hid = jnp.arange(RWKV_DIM) // RWKV_HEAD
    ones_bd = (hid[:, None] == hid[None, :]).astype(BF16)
    lw, k2, av, bv, g = rwkv_pre(p, 1, (3 * RWKV_DIM + L0_MLA_W) // L0_LORA_W, wl, g2, rwkv_w0, rwkv_a0,
                                 rwkv_k_k, rwkv_k_a, ones_bd)
    y = rwkv7_chunked(p, 0, 2 * RWKV_DIM // LANES, lw, k2, av, bv, B)
    o_rwkv = rwkv_post(y, p, 0, 2, k2, g, rwkv_r_k.reshape(-1), rwkv_gn_w, rwkv_gn_b, ones_bd)
    return mm2_ln(o_mla, o_rwkv, w_out, xb, ln_g, ln_b)


GDN_QKV_W = 2 * GDN_QK_HEADS * GDN_DK + GDN_V_HEADS * GDN_DV


def _gdn_pre_kernel(x_ref, xh_ref, ba_ref, cw_ref, al_ref, dt_ref, q_ref, k_ref, v_ref, bg_ref, *,
                    tiles_per_seq):
    m = pl.program_id(0)
    x = x_ref[...]
    xh = jnp.where(m % tiles_per_seq == 0, 0.0, xh_ref[...])
    rows = lax.broadcasted_iota(jnp.int32, x.shape, 0)
    cw = cw_ref[...]
    acc = x * cw[GDN_CONV - 1:GDN_CONV]
    for j in range(1, GDN_CONV):
        sh = pltpu.roll(x, j, 0)
        for i in range(j):
            sh = jnp.where(rows == i, xh[SUBLANES - j + i:SUBLANES - j + i + 1], sh)
        acc = acc + sh * cw[GDN_CONV - 1 - j:GDN_CONV - j]
    y = acc * jax.nn.sigmoid(acc)
    qk_w = GDN_QK_HEADS * GDN_DK
    for h in range(GDN_QK_HEADS):
        cs = slice(h * GDN_DK, (h + 1) * GDN_DK)
        yq = y[:, cs]
        q_ref[:, cs] = yq * (lax.rsqrt(jnp.sum(yq * yq, -1, keepdims=True) + 1e-6) * GDN_DK ** -0.5)
        yk = y[:, qk_w + h * GDN_DK:qk_w + (h + 1) * GDN_DK]
        k_ref[:, cs] = yk * lax.rsqrt(jnp.sum(yk * yk, -1, keepdims=True) + 1e-6)
    v_ref[...] = y[:, 2 * qk_w:]
    ba = ba_ref[...]
    lane = lax.broadcasted_iota(jnp.int32, ba.shape, 1)
    g = -jnp.exp(al_ref[...]) * _softplus(ba + dt_ref[...])
    bg_ref[...] = jnp.where(lane < GDN_V_HEADS, jax.nn.sigmoid(ba), g)


def gdn_pre(p, ba_blk, conv_w, a_log, dt_bias, seq, tm=256):
    T = p.shape[0]
    tm = _pick(seq, (tm, 128, 64))
    hb = tm // SUBLANES
    qk_w = GDN_QK_HEADS * GDN_DK
    v_w = GDN_V_HEADS * GDN_DV
    lane_pad = lambda a: jnp.pad(a, (GDN_V_HEADS, LANES - 2 * GDN_V_HEADS)).reshape(1, LANES)
    row = lambda n: pl.BlockSpec((tm, n), lambda i: (i, 0))
    full = lambda shp: pl.BlockSpec(shp, lambda i: (0, 0))
    return pl.pallas_call(
        functools.partial(_gdn_pre_kernel, tiles_per_seq=seq // tm),
        grid=(T // tm,),
        in_specs=[row(GDN_QKV_W),
                  pl.BlockSpec((SUBLANES, GDN_QKV_W), lambda i: (jnp.maximum(i * hb - 1, 0), 0)),
                  pl.BlockSpec((tm, LANES), lambda i: (i, ba_blk)),
                  full((GDN_CONV, GDN_QKV_W)), full((1, LANES)), full((1, LANES))],
        out_specs=[row(qk_w), row(qk_w), row(v_w), row(LANES)],
        out_shape=[jax.ShapeDtypeStruct((T, qk_w), F32), jax.ShapeDtypeStruct((T, qk_w), F32),
                   jax.ShapeDtypeStruct((T, v_w), F32), jax.ShapeDtypeStruct((T, LANES), F32)],
        compiler_params=_cparams(("parallel",)),
        name="gdn_pre",
    )(p, p, p, conv_w, lane_pad(a_log), lane_pad(dt_bias))


def _l1_post_kernel(og_ref, z_ref, or_ref, rg_ref, gn_ref, rw_ref, rb_ref, o_ref):
    v_w = GDN_V_HEADS * GDN_DV
    for h in range(GDN_V_HEADS):
        cs = slice(h * LANES, (h + 1) * LANES)
        o = og_ref[:, cs]
        z = z_ref[:, cs]
        on = o * lax.rsqrt(jnp.mean(o * o, -1, keepdims=True) + RMS_EPS) * gn_ref[...]
        o_ref[:, cs] = (on * (z * jax.nn.sigmoid(z))).astype(o_ref.dtype)
        r = or_ref[:, cs]
        rc = r - jnp.mean(r, -1, keepdims=True)
        rn = rc * lax.rsqrt(jnp.mean(rc * rc, -1, keepdims=True) + LN_EPS) * rw_ref[:, cs] + rb_ref[:, cs]
        g = rg_ref[:, cs]
        o_ref[:, v_w + h * LANES:v_w + (h + 1) * LANES] = (rn * (g * jax.nn.sigmoid(g))).astype(o_ref.dtype)


def l1_post(o_gdn, o_ret, p, z_blk, rg_blk, gdn_norm, ret_gn_w, ret_gn_b, tm=512):
    T, W = o_gdn.shape
    tm = _pick(T, (tm, 256, 128, 64))
    row = pl.BlockSpec((tm, W), lambda i: (i, 0))
    return pl.pallas_call(
        _l1_post_kernel,
        grid=(T // tm,),
        in_specs=[row, pl.BlockSpec((tm, W), lambda i: (i, z_blk)), row,
                  pl.BlockSpec((tm, W), lambda i: (i, rg_blk)),
                  pl.BlockSpec((1, LANES), lambda i: (0, 0)),
                  pl.BlockSpec((1, W), lambda i: (0, 0)),
                  pl.BlockSpec((1, W), lambda i: (0, 0))],
        out_specs=pl.BlockSpec((tm, 2 * W), lambda i: (i, 0)),
        out_shape=jax.ShapeDtypeStruct((T, 2 * W), BF16),
        compiler_params=_cparams(("parallel",)),
        name="l1_post",
    )(o_gdn, p, o_ret, p, gdn_norm.reshape(1, LANES), ret_gn_w.reshape(1, W), ret_gn_b.reshape(1, W))


def _gdn_retention_mixer(xb, B, S, ct, st, w_in, gdn_conv_w, gdn_A_log, gdn_dt_bias, gdn_norm,
                         ret_gn_w, ret_gn_b, w_out, ln_g, ln_b):
    D = xb.shape[1]
    v_w = GDN_V_HEADS * GDN_DV
    n_gdn = GDN_QKV_W + v_w
    w1p = jnp.concatenate([w_in[:, :n_gdn], w_in[:, n_gdn + 2 * GDN_V_HEADS:],
                           w_in[:, n_gdn:n_gdn + 2 * GDN_V_HEADS],
                           jnp.zeros((D, LANES - 2 * GDN_V_HEADS), F32)], axis=1)
    p = mm(xb, w1p, tm=1024, tn=896)
    q, k, v, bg = gdn_pre(p, (w1p.shape[1] - LANES) // LANES, gdn_conv_w, gdn_A_log, gdn_dt_bias, S)
    o_gdn = gdn_chunked(q, k, v, bg[:, GDN_V_HEADS:2 * GDN_V_HEADS], bg[:, :GDN_V_HEADS], B)
    qd = RET_HEADS * RET_DK
    o_ret = retention_chunked(p, n_gdn // qd, n_gdn // qd + 1, (n_gdn + 2 * qd) // v_w, ct, st, B)
    m1 = l1_post(o_gdn, o_ret, p, GDN_QKV_W // v_w, (n_gdn + 2 * qd + v_w) // v_w, gdn_norm, ret_gn_w, ret_gn_b)
    return mm_ln_full(m1, w_out, xb, ln_g, ln_b, tm=512)


def _conv_ffn_ln(x, S, w_gate, w_val, conv_w, conv_b, w_down, ln_g, ln_b):
    h = ffn1(x, w_gate, w_val, conv_w, conv_b, S)
    return mm_ln_full(h, w_down, x, ln_g, ln_b)


def kernel(x, positions, l0_w_in, l0_q_norm, l0_w_uq, l0_kv_norm, l0_w_ukv, l0_rwkv_mu, l0_rwkv_w0, l0_rwkv_w2, l0_rwkv_a0, l0_rwkv_a2, l0_rwkv_g2, l0_rwkv_k_k, l0_rwkv_k_a, l0_rwkv_r_k, l0_rwkv_gn_w, l0_rwkv_gn_b, l0_w_out, l0_ln1_g, l0_ln1_b, l0_ffn_w_gate, l0_ffn_w_val, l0_ffn_conv_w, l0_ffn_conv_b, l0_ffn_w_down, l0_ln2_g, l0_ln2_b, l1_w_in, l1_gdn_conv_w, l1_gdn_A_log, l1_gdn_dt_bias, l1_gdn_norm, l1_ret_gn_w, l1_ret_gn_b, l1_w_out, l1_ln1_g, l1_ln1_b, l1_ffn_w_gate, l1_ffn_w_val, l1_ffn_conv_w, l1_ffn_conv_b, l1_ffn_w_down, l1_ln2_g, l1_ln2_b):
    B, S, D = x.shape
    T = B * S
    xb = x.reshape(T, D)
    assert MLA_ROPE == RET_DK
    ct, st = _rope_tables(positions, MLA_ROPE)
    xb = _mla_rwkv_mixer(xb, B, S, ct, st, l0_w_in, l0_q_norm, l0_w_uq, l0_kv_norm, l0_w_ukv, l0_rwkv_mu,
                         l0_rwkv_w0, l0_rwkv_w2, l0_rwkv_a0, l0_rwkv_a2, l0_rwkv_g2, l0_rwkv_k_k, l0_rwkv_k_a,
                         l0_rwkv_r_k, l0_rwkv_gn_w, l0_rwkv_gn_b, l0_w_out, l0_ln1_g, l0_ln1_b)
    xb = _conv_ffn_ln(xb, S, l0_ffn_w_gate, l0_ffn_w_val, l0_ffn_conv_w, l0_ffn_conv_b, l0_ffn_w_down,
                      l0_ln2_g, l0_ln2_b)
    xb = _gdn_retention_mixer(xb, B, S, ct, st, l1_w_in, l1_gdn_conv_w, l1_gdn_A_log, l1_gdn_dt_bias,
                              l1_gdn_norm, l1_ret_gn_w, l1_ret_gn_b, l1_w_out, l1_ln1_g, l1_ln1_b)
    xb = _conv_ffn_ln(xb, S, l1_ffn_w_gate, l1_ffn_w_val, l1_ffn_conv_w, l1_ffn_conv_b, l1_ffn_w_down,
                      l1_ln2_g, l1_ln2_b)
    return xb.reshape(B, S, D)
```

```python
import functools
import math

import jax
import jax.numpy as jnp
from jax import lax
from jax.experimental import pallas as pl
from jax.experimental.pallas import tpu as pltpu

F32 = jnp.float32
BF16 = jnp.bfloat16

DEPTH = 2
ALPHA = (2 * DEPTH) ** 0.25
LN_EPS = 1e-5
RMS_EPS = 1e-6
ROPE_THETA = 10000.0

MLA_HEADS = 8
MLA_Q_RANK = 448
MLA_KV_RANK = 128
MLA_NOPE = 128
MLA_ROPE = 64
MLA_V = 128
RWKV_HEADS = 16
RWKV_HEAD = 64
RWKV_DIM = RWKV_HEADS * RWKV_HEAD
DECAY_LORA = 64
AAA_LORA = 64
GATE_LORA = 160
RWKV_GN_EPS = 64e-5
GDN_QK_HEADS = 4
GDN_V_HEADS = 8
GDN_DK = 128
GDN_DV = 128
GDN_CONV = 4
RET_HEADS = 8
RET_DK = 64
RET_DV = 128
FFN_CONV = 3

CHUNK = 64
LANES = 128
SUBLANES = 8
VMEM_LIMIT = 56 * 1024 * 1024


def _cparams(sem):
    return pltpu.CompilerParams(dimension_semantics=sem, vmem_limit_bytes=VMEM_LIMIT)


def _dot(a, b):
    return jnp.dot(a, b, preferred_element_type=F32)


def _dot_nt(a, b):
    return lax.dot_general(a, b, (((1,), (1,)), ((), ())), preferred_element_type=F32)


def _dot_tn(a, b):
    return lax.dot_general(a, b, (((0,), (0,)), ((), ())), preferred_element_type=F32)


def _split3(x):
    hi = x.astype(BF16)
    r1 = x - hi.astype(F32)
    mid = r1.astype(BF16)
    lo = (r1 - mid.astype(F32)).astype(BF16)
    return hi, mid, lo


def _dot_exact_lhs(m_bf16, x):
    hi, mid, lo = _split3(x)
    return _dot(m_bf16, hi) + _dot(m_bf16, mid) + _dot(m_bf16, lo)


def _mm_kernel(a_ref, w_ref, o_ref, *scratch, nk):
    if nk == 1:
        o_ref[...] = _dot(a_ref[...].astype(BF16), w_ref[...]).astype(o_ref.dtype)
        return
    acc_ref, = scratch
    k = pl.program_id(2)

    @pl.when(k == 0)
    def _():
        acc_ref[...] = jnp.zeros_like(acc_ref)

    acc_ref[...] += _dot(a_ref[...].astype(BF16), w_ref[...])

    @pl.when(k == nk - 1)
    def _():
        o_ref[...] = acc_ref[...].astype(o_ref.dtype)


def _pick(n, pref):
    for t in pref:
        if n % t == 0:
            return t
    return n


def mm(a, w, out_dtype=F32, tm=512, tn=None, tk=None):
    M, K = a.shape
    N = w.shape[1]
    tm = _pick(M, (tm, 256, 128, 64, 32, 16, 8))
    tn = tn or _pick(N, (512, 384, 256, 128))
    tk = tk or K
    nk = K // tk
    w = w.astype(BF16)
    scratch = [] if nk == 1 else [pltpu.VMEM((tm, tn), F32)]
    return pl.pallas_call(
        functools.partial(_mm_kernel, nk=nk),
        grid=(M // tm, N // tn, nk),
        in_specs=[pl.BlockSpec((tm, tk), lambda i, j, k: (i, k)),
                  pl.BlockSpec((tk, tn), lambda i, j, k: (k, j))],
        out_specs=pl.BlockSpec((tm, tn), lambda i, j, k: (i, j)),
        out_shape=jax.ShapeDtypeStruct((M, N), out_dtype),
        scratch_shapes=scratch,
        compiler_params=_cparams(("parallel", "parallel", "arbitrary")),
        name="mm",
    )(a, w)


def _mm_ln_kernel(a_ref, w_ref, x_ref, g_ref, b_ref, o_ref, acc_ref, *, nk):
    k = pl.program_id(1)

    @pl.when(k == 0)
    def _():
        acc_ref[...] = jnp.zeros_like(acc_ref)

    acc_ref[...] += _dot(a_ref[...].astype(BF16), w_ref[...])

    @pl.when(k == nk - 1)
    def _():
        y = ALPHA * x_ref[...] + acc_ref[...]
        mu = jnp.mean(y, axis=-1, keepdims=True)
        yc = y - mu
        var = jnp.mean(yc * yc, axis=-1, keepdims=True)
        o_ref[...] = yc * lax.rsqrt(var + LN_EPS) * g_ref[...] + b_ref[...]


def _mm_ln_full_kernel(a_ref, w_ref, x_ref, g_ref, b_ref, o_ref):
    y = ALPHA * x_ref[...] + _dot(a_ref[...].astype(BF16), w_ref[...])
    mu = jnp.mean(y, axis=-1, keepdims=True)
    yc = y - mu
    var = jnp.mean(yc * yc, axis=-1, keepdims=True)
    o_ref[...] = yc * lax.rsqrt(var + LN_EPS) * g_ref[...] + b_ref[...]


def mm_ln_full(a, w, x, g, b, tm=256):
    M, K = a.shape
    N = w.shape[1]
    tm = _pick(M, (tm, 128, 64, 32, 16, 8))
    const = lambda shp: pl.BlockSpec(shp, lambda i: (0, 0), pipeline_mode=pl.Buffered(1))
    return pl.pallas_call(
        _mm_ln_full_kernel,
        grid=(M // tm,),
        in_specs=[pl.BlockSpec((tm, K), lambda i: (i, 0)),
                  const((K, N)),
                  pl.BlockSpec((tm, N), lambda i: (i, 0)),
                  const((1, N)), const((1, N))],
        out_specs=pl.BlockSpec((tm, N), lambda i: (i, 0)),
        out_shape=jax.ShapeDtypeStruct((M, N), F32),
        compiler_params=_cparams(("parallel",)),
        name="mm_ln_full",
    )(a, w.astype(BF16), x, g.reshape(1, N), b.reshape(1, N))


def mm_ln(a, w, x, g, b, tm=512, tk=None):
    M, K = a.shape
    N = w.shape[1]
    tm = _pick(M, (tm, 256, 128, 64, 32, 16, 8))
    tk = tk or _pick(K, (1024, 1408, 512, 256, 128))
    nk = K // tk
    return pl.pallas_call(
        functools.partial(_mm_ln_kernel, nk=nk),
        grid=(M // tm, nk),
        in_specs=[pl.BlockSpec((tm, tk), lambda i, k: (i, k)),
                  pl.BlockSpec((tk, N), lambda i, k: (k, 0)),
                  pl.BlockSpec((tm, N), lambda i, k: (i, 0)),
                  pl.BlockSpec((1, N), lambda i, k: (0, 0)),
                  pl.BlockSpec((1, N), lambda i, k: (0, 0))],
        out_specs=pl.BlockSpec((tm, N), lambda i, k: (i, 0)),
        out_shape=jax.ShapeDtypeStruct((M, N), F32),
        scratch_shapes=[pltpu.VMEM((tm, N), F32)],
        compiler_params=_cparams(("parallel", "arbitrary")),
        name="mm_ln",
    )(a, w.astype(BF16), x, g.reshape(1, N), b.reshape(1, N))


def _ffn1_kernel(x_ref, xh_ref, wg_ref, wv_ref, cw_ref, cb_ref, o_ref, *, tiles_per_seq):
    m = pl.program_id(0)
    x = x_ref[...].astype(BF16)
    g = _dot(x, wg_ref[...])
    gh = _dot(xh_ref[...].astype(BF16), wg_ref[...])
    gh = jnp.where(m % tiles_per_seq == 0, 0.0, gh)
    rows = lax.broadcasted_iota(jnp.int32, g.shape, 0)
    g1 = jnp.where(rows == 0, gh[SUBLANES - 1:SUBLANES], pltpu.roll(g, 1, 0))
    g2 = jnp.where(rows == 0, gh[SUBLANES - 2:SUBLANES - 1],
                   jnp.where(rows == 1, gh[SUBLANES - 1:SUBLANES], pltpu.roll(g, 2, 0)))
    cw = cw_ref[...]
    h = g * cw[2:3] + g1 * cw[1:2] + g2 * cw[0:1] + cb_ref[...]
    v = _dot(x, wv_ref[...])
    o_ref[...] = (h * jax.nn.sigmoid(h) * v).astype(o_ref.dtype)


def ffn1(x, w_gate, w_val, conv_w, conv_b, seq, tm=1024, tf=512):
    M, D = x.shape
    Fd = w_gate.shape[1]
    tm = _pick(seq, (tm, 256, 128, 64, 32, 16, 8))
    tf = _pick(Fd, (tf, 256, 128))
    hb = tm // SUBLANES
    return pl.pallas_call(
        functools.partial(_ffn1_kernel, tiles_per_seq=seq // tm),
        grid=(M // tm, Fd // tf),
        in_specs=[pl.BlockSpec((tm, D), lambda i, j: (i, 0)),
                  pl.BlockSpec((SUBLANES, D), lambda i, j: (jnp.maximum(i * hb - 1, 0), 0)),
                  pl.BlockSpec((D, tf), lambda i, j: (0, j)),
                  pl.BlockSpec((D, tf), lambda i, j: (0, j)),
                  pl.BlockSpec((FFN_CONV, tf), lambda i, j: (0, j)),
                  pl.BlockSpec((1, tf), lambda i, j: (0, j))],
        out_specs=pl.BlockSpec((tm, tf), lambda i, j: (i, j)),
        out_shape=jax.ShapeDtypeStruct((M, Fd), BF16),
        compiler_params=_cparams(("parallel", "parallel")),
        name="ffn1",
    )(x, x, w_gate.astype(BF16), w_val.astype(BF16), conv_w, conv_b.reshape(1, Fd))


def _flash_kernel(q_ref, k_ref, v_ref, o_ref, m_ref, acc_ref, s_ref, *, tq, tk, ts, dv):
    qi = pl.program_id(2)
    nsub = tq // ts
    ndiag = tq // tk
    nfull = qi * ndiag
    m_ref[...] = jnp.full_like(m_ref, -jnp.inf)
    acc_ref[...] = jnp.zeros_like(acc_ref)

    def above_diagonal(r, d):
        return (r + 1) * ts <= d * tk

    def scores(r, blk):
        k = k_ref[0, 0, pl.ds(pl.multiple_of(blk * tk, tk), tk), :]
        return _dot_nt(q_ref[0, 0, r * ts:(r + 1) * ts, :], k)

    def consume(blk, diag, nxt_diag):
        v = v_ref[0, 0, pl.ds(pl.multiple_of(blk * tk, tk), tk), :]
        for r in range(nsub):
            rs = slice(r * ts, (r + 1) * ts)
            live = diag is None or not above_diagonal(r, diag)
            if live:
                s = s_ref[rs, :]
            if nxt_diag is None or (nxt_diag >= 0 and not above_diagonal(r, nxt_diag)):
                s_ref[rs, :] = scores(r, blk + 1)
            if not live:
                continue
            if diag is not None and r * ts < (diag + 1) * tk - 1:
                rowi = lax.broadcasted_iota(jnp.int32, s.shape, 0) + r * ts
                coli = lax.broadcasted_iota(jnp.int32, s.shape, 1) + diag * tk
                s = jnp.where(coli <= rowi, s, -jnp.inf)
            m_old = m_ref[rs, :]
            m_new = jnp.maximum(m_old, jnp.max(s, axis=-1, keepdims=True))
            p = jnp.exp2(s - m_new).astype(BF16)
            acc_ref[rs, :] = jnp.exp2(m_old - m_new) * acc_ref[rs, :] + _dot(p, v)
            m_ref[rs, :] = m_new

    for r in range(nsub):
        s_ref[r * ts:(r + 1) * ts, :] = scores(r, 0)

    def body(j, carry):
        consume(2 * j, None, None)
        consume(2 * j + 1, None, None)
        return carry

    lax.fori_loop(0, nfull // 2, body, 0)

    @pl.when(nfull % 2 == 1)
    def _():
        consume(nfull - 1, None, None)

    for d in range(ndiag):
        consume(nfull + d, d, d + 1 if d + 1 < ndiag else -1)
    acc = acc_ref[...]
    o_ref[...] = (acc[:, :dv] / acc[:, dv:]).astype(o_ref.dtype)


def flash_attention(q, k, v, tq=1024, tk=1024, ts=512):
    B, H, S, dq = q.shape
    dv = v.shape[-1] // 2
    tq = _pick(S, (tq, 512, 256, 128))
    tk = _pick(tq, (tk, 256, 128))
    nq = S // tq
    return pl.pallas_call(
        functools.partial(_flash_kernel, tq=tq, tk=tk, ts=_pick(tq, (ts, 128)), dv=dv),
        grid=(B, H, nq),
        in_specs=[pl.BlockSpec((1, 1, tq, dq), lambda b, h, i: (b, h, i, 0)),
                  pl.BlockSpec((1, 1, S, dq), lambda b, h, i: (b, h, 0, 0)),
                  pl.BlockSpec((1, 1, S, 2 * dv), lambda b, h, i: (b, h, 0, 0))],
        out_specs=pl.BlockSpec((tq, dv), lambda b, h, i: (b * nq + i, h)),
        out_shape=jax.ShapeDtypeStruct((B * S, H * dv), BF16),
        scratch_shapes=[pltpu.VMEM((tq, 1), F32), pltpu.VMEM((tq, 2 * dv), F32), pltpu.VMEM((tq, tk), F32)],
        compiler_params=_cparams(("parallel", "parallel", "arbitrary")),
        name="flash",
    )(q, k, v)


def _tri_inv_kernel(a_ref, t_ref):
    C = a_ref.shape[1]
    kk = lax.broadcasted_iota(jnp.int32, (SUBLANES, LANES), 0)
    zero = jnp.zeros((SUBLANES, LANES), F32)
    for i in range(C):
        ng = i // SUBLANES + 1
        acc = [jnp.where(kk == i - g * SUBLANES, 1.0, 0.0).astype(F32) for g in range(ng)]
        for j in range(i):
            a_ij = a_ref[0, i, j:j + 1, :]
            for g in range(j // SUBLANES + 1):
                acc[g] = acc[g] + a_ij * t_ref[0, j, g * SUBLANES:(g + 1) * SUBLANES, :]
        for g in range(C // SUBLANES):
            t_ref[0, i, g * SUBLANES:(g + 1) * SUBLANES, :] = acc[g] if g < ng else zero


def tri_inv(a):
    n_in, C, _ = a.shape
    ng = -(-n_in // LANES)
    N = ng * LANES
    a = jnp.pad(a, ((0, N - n_in), (0, 0), (0, 0)))
    al = a.reshape(ng, LANES, C * C).transpose(0, 2, 1).reshape(ng, C, C, LANES)
    tl = pl.pallas_call(
        _tri_inv_kernel,
        grid=(ng,),
        in_specs=[pl.BlockSpec((1, C, C, LANES), lambda i: (i, 0, 0, 0))],
        out_specs=pl.BlockSpec((1, C, C, LANES), lambda i: (i, 0, 0, 0)),
        out_shape=jax.ShapeDtypeStruct((ng, C, C, LANES), F32),
        compiler_params=_cparams(("parallel",)),
        name="tri_inv",
    )(al)
    return tl.reshape(ng, C * C, LANES).transpose(0, 2, 1).reshape(N, C, C)[:n_in]


def _tri_inv_pairs_kernel(a_ref, t_ref, al_ref, tl_ref):
    C = al_ref.shape[1]
    for i in range(C):
        mt = a_ref[pl.ds(i, LANES, stride=C), :].T
        al_ref[0, i] = mt[:C]
        al_ref[1, i] = mt[C:]
    for hh in range(2):
        _tri_inv_kernel(al_ref.at[pl.ds(hh, 1)], tl_ref.at[pl.ds(hh, 1)])
    for i in range(C):
        mt = jnp.concatenate([tl_ref[0, i], tl_ref[1, i]], axis=0)
        t_ref[pl.ds(i, LANES, stride=C), :] = mt.T


def tri_inv_pairs(a):
    n, m, C, _ = a.shape
    tiles = n * m
    ng = -(-tiles // LANES)
    a2 = jnp.pad(a.reshape(tiles * C, 2 * C), ((0, (ng * LANES - tiles) * C), (0, 0)))
    t = pl.pallas_call(
        _tri_inv_pairs_kernel,
        grid=(ng,),
        in_specs=[pl.BlockSpec((LANES * C, 2 * C), lambda i: (i, 0))],
        out_specs=pl.BlockSpec((LANES * C, 2 * C), lambda i: (i, 0)),
        out_shape=jax.ShapeDtypeStruct((ng * LANES * C, 2 * C), F32),
        scratch_shapes=[pltpu.VMEM((2, C, C, LANES), F32), pltpu.VMEM((2, C, C, LANES), F32)],
        compiler_params=_cparams(("parallel",)),
        name="tri_inv_pairs",
    )(a2)
    return t[:tiles * C].reshape(n, m, C, 2 * C)


def _chunk_tri(C, G):
    idx = jnp.arange(C * G)
    return ((idx[None, :] <= idx[:, None]) & (idx[None, :] // C == idx[:, None] // C)).astype(BF16)


def _chunk_masks():
    C = CHUNK
    row = lax.broadcasted_iota(jnp.int32, (C, C), 0)
    col = lax.broadcasted_iota(jnp.int32, (C, C), 1)
    return row, col


def _rwkv_a_kernel(r_ref, lw_ref, k_ref, v_ref, av_ref, bv_ref, tri_ref,
                   aab_ref, mrb_ref, at_ref, akv_ref, rt_ref, bh_ref, yk_ref, kv_ref, egl_ref, *, G):
    C = CHUNK
    rowp = lax.broadcasted_iota(jnp.int32, (C, LANES), 0)
    lane = lax.broadcasted_iota(jnp.int32, (C, LANES), 1)
    strict = lane % C < rowp
    incl = lane % C <= rowp
    head0 = lane < RWKV_HEAD
    bd = (lax.broadcasted_iota(jnp.int32, (LANES, LANES), 0) // RWKV_HEAD ==
          lax.broadcasted_iota(jnp.int32, (LANES, LANES), 1) // RWKV_HEAD)

    def by_head(x):
        return jnp.concatenate([jnp.where(head0, x, 0.0), jnp.where(head0, 0.0, x)], axis=0).astype(BF16)

    lw_all = lw_ref[...]
    gc_all = _dot_exact_lhs(tri_ref[...], lw_all)
    sls = [slice(g * C, (g + 1) * C) for g in range(G)]
    at, rt, vb, egl, gl, xb, xk = [], [], [], [], [], [], []
    for sl in sls:
        gc = gc_all[sl]
        gl.append(gc[C - 1:C, :])
        ieg = jnp.exp(-gc)
        egl.append(jnp.exp(gl[-1] - gc))
        at.append(av_ref[sl, :] * jnp.exp(gc - lw_all[sl]))
        rt.append(r_ref[sl, :] * jnp.exp(gc))
        vb.append(by_head(v_ref[sl, :]))
        lhs = jnp.concatenate([at[-1], rt[-1]], axis=0).astype(BF16)
        xb.append(_dot_nt(lhs, by_head(bv_ref[sl, :] * ieg)))
        xk.append(_dot_nt(lhs, by_head(k_ref[sl, :] * ieg)))
    for g, sl in enumerate(sls):
        aab_ref[g, 0] = jnp.where(strict, xb[g][:C], 0.0)
        mrb_ref[g, 0] = jnp.where(incl, xb[g][C:], 0.0).astype(BF16)
        akv_ref[sl, :] = _dot(jnp.where(strict, xk[g][:C], 0.0).astype(BF16), vb[g]).astype(BF16)
        yk_ref[sl, :] = _dot(jnp.where(incl, xk[g][C:], 0.0).astype(BF16), vb[g])
    for g, sl in enumerate(sls):
        at_ref[sl, :] = at[g].astype(BF16)
        rt_ref[sl, :] = rt[g].astype(BF16)
        bh_ref[sl, :] = (bv_ref[sl, :] * egl[g]).astype(BF16)
        kv_ref[g, 0] = jnp.where(bd, _dot_tn(v_ref[sl, :].astype(BF16), (k_ref[sl, :] * egl[g]).astype(BF16)), 0.0)
        egl_ref[g] = jnp.exp(gl[g])


def _rwkv_c_kernel(at_ref, akv_ref, rt_ref, bh_ref, yk_ref, t_ref, mrb_ref, kv_ref, egl_ref,
                   y_ref, s_ref, *, G):
    C = CHUNK
    npair = RWKV_HEADS // 2

    @pl.when(pl.program_id(1) == 0)
    def _():
        s_ref[...] = jnp.zeros_like(s_ref)

    lane = lax.broadcasted_iota(jnp.int32, (C, LANES), 1)
    head0 = lane < RWKV_HEAD
    head0w = jnp.concatenate([head0, head0], axis=1)
    bd = (lax.broadcasted_iota(jnp.int32, (LANES, LANES), 0) // RWKV_HEAD ==
          lax.broadcasted_iota(jnp.int32, (LANES, LANES), 1) // RWKV_HEAD)
    zero = jnp.zeros((), BF16)

    def by_head(x, m):
        return jnp.concatenate([jnp.where(m, x, zero), jnp.where(m, zero, x)], axis=0)

    def body(g, carry):
        sl = pl.ds(pl.multiple_of(g * C, C), C)
        cols = [slice(p * LANES, (p + 1) * LANES) for p in range(npair)]
        w, s, sb, u, ub, y = [], [], [], [], [], []
        for p, cs in enumerate(cols):
            both = jnp.concatenate([at_ref[sl, cs], akv_ref[sl, cs]], axis=1)
            w.append(_dot(t_ref[g, p].astype(BF16), by_head(both, head0w)))
            s.append(s_ref[p])
            sb.append(s[p].astype(BF16))
        for p, cs in enumerate(cols):
            u.append(_dot_nt(w[p][:, :LANES].astype(BF16), sb[p]) + w[p][:, LANES:])
            ub.append(u[p].astype(BF16))
            y.append(_dot_nt(rt_ref[sl, cs], sb[p]) + yk_ref[sl, cs])
        for p, cs in enumerate(cols):
            y_ref[sl, cs] = y[p] + _dot(mrb_ref[g, p], by_head(ub[p], head0))
            s_ref[p] = (s[p] * egl_ref[g, :, cs] + jnp.where(bd, _dot_tn(ub[p], bh_ref[sl, cs]), 0.0)
                        + kv_ref[g, p])
        return carry

    lax.fori_loop(0, G, body, 0)


def rwkv7_chunked(p, r_blk, v_blk, lw, k, av, bv, batch, G=4, Ga=4):
    T, HD = lw.shape
    C = CHUNK
    seq = T // batch
    nct = T // C
    G = _pick(seq // C, (G, 2, 1))
    Ga = _pick(seq // C, (Ga, 4, 2, 1))
    npair = HD // LANES
    row_spec = pl.BlockSpec((C * Ga, LANES), lambda i, p: (i, p))
    pair_mat = pl.BlockSpec((Ga, 1, C, LANES), lambda i, p: (i, p, 0, 0))
    outs = pl.pallas_call(
        functools.partial(_rwkv_a_kernel, G=Ga),
        grid=(nct // Ga, npair),
        in_specs=[pl.BlockSpec((C * Ga, LANES), lambda i, p: (i, r_blk + p)), row_spec, row_spec,
                  pl.BlockSpec((C * Ga, LANES), lambda i, p: (i, v_blk + p)), row_spec, row_spec,
                  pl.BlockSpec((C * Ga, C * Ga), lambda i, p: (0, 0))],
        out_specs=[pair_mat, pair_mat,
                   row_spec, row_spec, row_spec, row_spec, row_spec,
                   pl.BlockSpec((Ga, 1, LANES, LANES), lambda i, p: (i, p, 0, 0)),
                   pl.BlockSpec((Ga, 1, LANES), lambda i, p: (i, 0, p))],
        out_shape=[jax.ShapeDtypeStruct((nct, npair, C, LANES), F32),
                   jax.ShapeDtypeStruct((nct, npair, C, LANES), BF16),
                   jax.ShapeDtypeStruct((T, HD), BF16),
                   jax.ShapeDtypeStruct((T, HD), BF16),
                   jax.ShapeDtypeStruct((T, HD), BF16),
                   jax.ShapeDtypeStruct((T, HD), BF16),
                   jax.ShapeDtypeStruct((T, HD), F32),
                   jax.ShapeDtypeStruct((nct, npair, LANES, LANES), F32),
                   jax.ShapeDtypeStruct((nct, 1, HD), F32)],
        compiler_params=_cparams(("parallel", "parallel")),
        name="rwkv_a",
    )(p, lw, k, p, av, bv, _chunk_tri(C, Ga))
    aab, mrb, at, akv, rt, bh, yk, kv, egl = outs
    tinv = tri_inv_pairs(aab)
    ncb = seq // C // G
    wide = pl.BlockSpec((C * G, HD), lambda b, n: (b * ncb + n, 0))
    return pl.pallas_call(
        functools.partial(_rwkv_c_kernel, G=G),
        grid=(batch, ncb),
        in_specs=[wide, wide, wide, wide, wide,
                  pl.BlockSpec((G, npair, C, LANES), lambda b, n: (b * ncb + n, 0, 0, 0)),
                  pl.BlockSpec((G, npair, C, LANES), lambda b, n: (b * ncb + n, 0, 0, 0)),
                  pl.BlockSpec((G, npair, LANES, LANES), lambda b, n: (b * ncb + n, 0, 0, 0)),
                  pl.BlockSpec((G, 1, HD), lambda b, n: (b * ncb + n, 0, 0))],
        out_specs=wide,
        out_shape=jax.ShapeDtypeStruct((T, HD), F32),
        scratch_shapes=[pltpu.VMEM((npair, LANES, LANES), F32)],
        compiler_params=_cparams(("arbitrary", "arbitrary")),
        name="rwkv_c",
    )(at, akv, rt, bh, yk, tinv, mrb, kv, egl)


def _gdn_a_kernel(q_ref, k_ref, v_ref, bg_ref, grow_ref, selg_ref, selb_ref, tri_ref, triu_ref,
                  xm_ref, attn_ref, vb_ref, kcp_ref, qd_ref, kd_ref, egl_ref, *, G):
    C = CHUNK
    n = C * G
    rowp = lax.broadcasted_iota(jnp.int32, (C, LANES), 0)
    lane = lax.broadcasted_iota(jnp.int32, (C, LANES), 1)
    strict = lane % C < rowp
    incl = lane % C <= rowp
    head_a = lane < C
    bg3 = _split3(bg_ref[...])
    wide = lambda sel: _dot(bg3[0], sel) + _dot(bg3[1], sel) + _dot(bg3[2], sel)
    gcw = _dot_exact_lhs(tri_ref[...], wide(selg_ref[0]))
    hi, mid, lo = _split3(grow_ref[0, 0])
    triu = triu_ref[...]
    gcr_all = _dot(hi, triu) + _dot(mid, triu) + _dot(lo, triu)
    betaw = wide(selb_ref[0])
    sls = [slice(g * C, (g + 1) * C) for g in range(G)]
    kk, qk = [], []
    for sl in sls:
        kbf = k_ref[sl, :].astype(BF16)
        k2 = jnp.concatenate([kbf, kbf], axis=0)
        kk.append(_dot_nt(kbf, k2))
        qk.append(_dot_nt(q_ref[sl, :].astype(BF16), k2))
    for g, sl in enumerate(sls):
        gc = gcw[sl]
        bw = betaw[sl]
        diff = jnp.where(head_a, gc[:, :LANES], gc[:, LANES:]) - gcr_all[g:g + 1]
        decay = jnp.where(incl, jnp.exp(jnp.where(incl, diff, 0.0)), 0.0)
        bp = jnp.where(head_a, bw[:, :LANES], bw[:, LANES:])
        xm_ref[g, 0] = jnp.where(strict, -(kk[g] * bp) * decay, 0.0)
        attn_ref[g, 0] = jnp.where(incl, qk[g] * decay, 0.0).astype(BF16)
        eg = jnp.exp(gc)
        gl = gc[C - 1:C, :]
        k = k_ref[sl, :]
        q = q_ref[sl, :]
        k2w = jnp.concatenate([k, k], axis=1)
        vb_ref[sl, :] = (v_ref[sl, :] * bw).astype(BF16)
        kcp_ref[sl, :] = (k2w * bw * eg).astype(BF16)
        qd_ref[sl, :] = (jnp.concatenate([q, q], axis=1) * eg).astype(BF16)
        kd_ref[sl, :] = (k2w * jnp.exp(gl - gc)).astype(BF16)
        egl_ref[g] = jnp.exp(gl)


def _gdn_c_kernel(vb_ref, kcp_ref, qd_ref, kd_ref, t_ref, attn_ref, egl_ref, o_ref, s_ref, *, G):
    C = CHUNK
    npair = GDN_V_HEADS // 2

    @pl.when(pl.program_id(1) == 0)
    def _():
        s_ref[...] = jnp.zeros_like(s_ref)

    def blockdiag(x0, x1):
        z = jnp.zeros_like(x0)
        return jnp.concatenate([jnp.concatenate([x0, z], axis=1), jnp.concatenate([z, x1], axis=1)], axis=0)

    def body(g, carry):
        sl = pl.ds(pl.multiple_of(g * C, C), C)
        cols = [slice(h * LANES, (h + 1) * LANES) for h in range(GDN_V_HEADS)]
        both = [jnp.concatenate([vb_ref[sl, cs], kcp_ref[sl, cs]], axis=1) for cs in cols]
        w, s, sb, vnb, o = [], [], [], [], []
        for j in range(npair):
            wj = _dot(t_ref[g, j].astype(BF16), blockdiag(both[2 * j], both[2 * j + 1]))
            w += [wj[:, :2 * LANES], wj[:, 2 * LANES:]]
        for h in range(GDN_V_HEADS):
            s.append(s_ref[h])
            sb.append(s[h].astype(BF16))
        for h, cs in enumerate(cols):
            vnb.append((w[h][:, :LANES] - _dot(w[h][:, LANES:].astype(BF16), sb[h])).astype(BF16))
            o.append(_dot(qd_ref[sl, cs], sb[h]))
        for j in range(npair):
            att = _dot(attn_ref[g, j], blockdiag(vnb[2 * j], vnb[2 * j + 1]))
            for hh in range(2):
                h = 2 * j + hh
                o_ref[sl, cols[h]] = o[h] + att[:, hh * LANES:(hh + 1) * LANES]
                s_ref[h] = s[h] * egl_ref[g, :, cols[h]] + _dot_tn(kd_ref[sl, cols[h]], vnb[h])
        return carry

    lax.fori_loop(0, G, body, 0)


def gdn_chunked(q, k, v, bg, batch, G=4, Ga=8):
    T, VD = v.shape
    C = CHUNK
    seq = T // batch
    nct = T // C
    G = _pick(seq // C, (G, 2, 1))
    Ga = _pick(seq // C, (Ga, 4, 2, 1))
    nh = GDN_V_HEADS
    nq = GDN_QK_HEADS
    assert nh == 2 * nq and 2 * C == LANES
    g = bg[:, nh:2 * nh]
    lrow = jnp.arange(LANES)[None, :, None]
    lcol = jnp.arange(2 * LANES)[None, None, :] // LANES
    jj = jnp.arange(nq)[:, None, None]
    selb = (lrow == 2 * jj + lcol).astype(BF16)
    selg = (lrow == nh + 2 * jj + lcol).astype(BF16)
    grow = g.T.reshape(nq, 2, nct // Ga, Ga, C).transpose(0, 2, 3, 1, 4).reshape(nq, nct // Ga, Ga, LANES)
    grow = jnp.pad(grow, ((0, 0), (0, 0), (0, SUBLANES - Ga), (0, 0)))
    lidx = jnp.arange(LANES)
    triu2 = ((lidx[:, None] <= lidx[None, :]) & (lidx[:, None] // C == lidx[None, :] // C)).astype(BF16)
    qk_spec = pl.BlockSpec((C * Ga, LANES), lambda i, j: (i, j))
    v_spec = pl.BlockSpec((C * Ga, 2 * LANES), lambda i, j: (i, j))
    sel_spec = pl.BlockSpec((1, LANES, 2 * LANES), lambda i, j: (j, 0, 0))
    pair_mat = pl.BlockSpec((Ga, 1, C, LANES), lambda i, j: (i, j, 0, 0))
    outs = pl.pallas_call(
        functools.partial(_gdn_a_kernel, G=Ga),
        grid=(nct // Ga, nq),
        in_specs=[qk_spec, qk_spec, v_spec, pl.BlockSpec((C * Ga, LANES), lambda i, j: (i, 0)),
                  pl.BlockSpec((1, 1, SUBLANES, LANES), lambda i, j: (j, i, 0, 0)), sel_spec, sel_spec,
                  pl.BlockSpec((C * Ga, C * Ga), lambda i, j: (0, 0)),
                  pl.BlockSpec((LANES, LANES), lambda i, j: (0, 0))],
        out_specs=[pair_mat, pair_mat,
                   v_spec, v_spec, v_spec, v_spec,
                   pl.BlockSpec((Ga, 1, 2 * LANES), lambda i, j: (i, 0, j))],
        out_shape=[jax.ShapeDtypeStruct((nct, nq, C, LANES), F32),
                   jax.ShapeDtypeStruct((nct, nq, C, LANES), BF16),
                   jax.ShapeDtypeStruct((T, VD), BF16),
                   jax.ShapeDtypeStruct((T, VD), BF16),
                   jax.ShapeDtypeStruct((T, VD), BF16),
                   jax.ShapeDtypeStruct((T, VD), BF16),
                   jax.ShapeDtypeStruct((nct, 1, VD), F32)],
        compiler_params=_cparams(("parallel", "parallel")),
        name="gdn_a",
    )(q, k, v, bg, grow, selg, selb, _chunk_tri(C, Ga), triu2)
    xm, attn, vb, kcp, qd, kd, egl = outs
    tinv = tri_inv_pairs(xm)
    ncb = seq // C // G
    wide = pl.BlockSpec((C * G, VD), lambda b, n: (b * ncb + n, 0))
    mat = pl.BlockSpec((G, nq, C, LANES), lambda b, n: (b * ncb + n, 0, 0, 0))
    return pl.pallas_call(
        functools.partial(_gdn_c_kernel, G=G),
        grid=(batch, ncb),
        in_specs=[wide, wide, wide, wide, mat, mat,
                  pl.BlockSpec((G, 1, VD), lambda b, n: (b * ncb + n, 0, 0))],
        out_specs=wide,
        out_shape=jax.ShapeDtypeStruct((T, VD), F32),
        scratch_shapes=[pltpu.VMEM((nh, GDN_DK, GDN_DV), F32)],
        compiler_params=_cparams(("arbitrary", "arbitrary")),
        name="gdn_c",
    )(vb, kcp, qd, kd, tinv, attn, egl)


def _rope_pair(x, ct, st, low):
    swapped = jnp.where(low, pltpu.roll(x, LANES - RET_DK // 2, 1), pltpu.roll(x, RET_DK // 2, 1))
    return x * ct + swapped * st


def _ret_kernel(q_ref, k_ref, v_ref, ct_ref, st_ref, o_ref, s_ref, *, G):
    C = CHUNK

    @pl.when(pl.program_id(1) == 0)
    def _():
        s_ref[...] = jnp.zeros_like(s_ref)

    row, col = _chunk_masks()
    incl = col <= row
    dif = jnp.maximum(row - col, 0).astype(F32)
    lane = lax.broadcasted_iota(jnp.int32, (C, LANES), 1)
    head0 = lane < RET_DK
    low = lane % RET_DK < RET_DK // 2
    ridx = lax.broadcasted_iota(jnp.int32, (C, LANES), 0).astype(F32)

    def body(g, carry):
        sl = pl.ds(pl.multiple_of(g * C, C), C)
        ct, st = ct_ref[sl, :], st_ref[sl, :]
        lgs = [math.log(1.0 - 2.0 ** (-5.0 - h)) for h in range(RET_HEADS)]
        vss = [slice(h * RET_DV, (h + 1) * RET_DV) for h in range(RET_HEADS)]
        sc, cross, upd, vb = [], [], [], []
        for p in range(RET_HEADS // 2):
            ps = slice(p * LANES, (p + 1) * LANES)
            qp = _rope_pair(q_ref[sl, ps], ct, st, low) * RET_DK ** -0.5
            kp = _rope_pair(k_ref[sl, ps], ct, st, low)
            kb = kp.astype(BF16)
            for hh in range(2):
                h = 2 * p + hh
                msk = head0 if hh == 0 else jnp.logical_not(head0)
                q = jnp.where(msk, qp, 0.0)
                vb.append(v_ref[sl, vss[h]].astype(BF16))
                sc.append(_dot_nt(q.astype(BF16), kb))
                qdec = (q * jnp.exp((ridx + 1.0) * lgs[h])).astype(BF16)
                cross.append(_dot(qdec, s_ref[h].astype(BF16)))
                kdec = (jnp.where(msk, kp, 0.0) * jnp.exp((C - 1.0 - ridx) * lgs[h])).astype(BF16)
                upd.append(_dot_tn(kdec, vb[h]))
        for h in range(RET_HEADS):
            dmat = jnp.where(incl, jnp.exp(dif * lgs[h]), 0.0)
            o_ref[sl, vss[h]] = _dot((sc[h] * dmat).astype(BF16), vb[h]) + cross[h]
            s_ref[h] = s_ref[h] * math.exp(C * lgs[h]) + upd[h]
        return carry

    lax.fori_loop(0, G, body, 0)


def retention_chunked(p, q_blk, k_blk, v_blk, ct, st, batch, G=4):
    T = p.shape[0]
    QD, VD = RET_HEADS * RET_DK, RET_HEADS * RET_DV
    C = CHUNK
    seq = T // batch
    G = _pick(seq // C, (G, 2, 1))
    ncb = seq // C // G
    tab = pl.BlockSpec((C * G, LANES), lambda b, n: (b * ncb + n, 0))
    wide = pl.BlockSpec((C * G, VD), lambda b, n: (b * ncb + n, 0))
    return pl.pallas_call(
        functools.partial(_ret_kernel, G=G),
        grid=(batch, ncb),
        in_specs=[pl.BlockSpec((C * G, QD), lambda b, n: (b * ncb + n, q_blk)),
                  pl.BlockSpec((C * G, QD), lambda b, n: (b * ncb + n, k_blk)),
                  pl.BlockSpec((C * G, VD), lambda b, n: (b * ncb + n, v_blk)),
                  tab, tab],
        out_specs=wide,
        out_shape=jax.ShapeDtypeStruct((T, VD), F32),
        scratch_shapes=[pltpu.VMEM((RET_HEADS, LANES, RET_DV), F32)],
        compiler_params=_cparams(("arbitrary", "arbitrary")),
        name="retention",
    )(p, p, p, ct, st)


def _rope_tables(positions, d):
    inv = ROPE_THETA ** (-jnp.arange(0, d, 2, dtype=F32) / d)
    ang = positions.astype(F32).reshape(-1, 1) * inv
    cos, sin = jnp.cos(ang), jnp.sin(ang)
    reps = LANES // d
    return jnp.concatenate([cos, cos] * reps, -1), jnp.concatenate([-sin, sin] * reps, -1)


L0_MLA_W = 768
L0_LORA_W = 384
L0_N = 3 * RWKV_DIM + L0_MLA_W + L0_LORA_W
CQ_PAD = 512


def _proj_shift_kernel(x_ref, xh_ref, w_ref, mu_ref, o_ref, *, tiles_per_seq):
    m = pl.program_id(0)
    w = w_ref[...]
    p = _dot(x_ref[...].astype(BF16), w)
    ph = _dot(xh_ref[...].astype(BF16), w)
    ph = jnp.where(m % tiles_per_seq == 0, 0.0, ph)
    rows = lax.broadcasted_iota(jnp.int32, p.shape, 0)
    prev = jnp.where(rows == 0, ph[SUBLANES - 1:SUBLANES], pltpu.roll(p, 1, 0))
    o_ref[...] = p + mu_ref[...] * (prev - p)


def proj_shift(x, w, mu, seq, tm=1024, tn=1408):
    M, D = x.shape
    N = w.shape[1]
    tm = _pick(seq, (tm, 256, 128, 64, 32, 16, 8))
    tn = _pick(N, (tn, 512, 384, 256, 128))
    hb = tm // SUBLANES
    return pl.pallas_call(
        functools.partial(_proj_shift_kernel, tiles_per_seq=seq // tm),
        grid=(M // tm, N // tn),
        in_specs=[pl.BlockSpec((tm, D), lambda i, j: (i, 0)),
                  pl.BlockSpec((SUBLANES, D), lambda i, j: (jnp.maximum(i * hb - 1, 0), 0)),
                  pl.BlockSpec((D, tn), lambda i, j: (0, j)),
                  pl.BlockSpec((1, tn), lambda i, j: (0, j))],
        out_specs=pl.BlockSpec((tm, tn), lambda i, j: (i, j)),
        out_shape=jax.ShapeDtypeStruct((M, N), F32),
        compiler_params=_cparams(("parallel", "parallel")),
        name="proj_shift",
    )(x, x, w, mu.reshape(1, N))


def _mla_pre_kernel(p_ref, ct_ref, st_ref, qn_ref, kn_ref, wq_ref, wkv_ref, q_ref, k_ref, v_ref):
    p = p_ref[...]
    cq = p[:, :CQ_PAD]
    ckv = p[:, CQ_PAD:CQ_PAD + MLA_KV_RANK]
    kp = p[:, CQ_PAD + MLA_KV_RANK:]
    cqn = cq * lax.rsqrt(jnp.sum(cq * cq, -1, keepdims=True) * (1.0 / MLA_Q_RANK) + RMS_EPS) * qn_ref[...]
    ckvn = ckv * lax.rsqrt(jnp.mean(ckv * ckv, -1, keepdims=True) + RMS_EPS) * kn_ref[...]
    qa = _dot(cqn.astype(BF16), wq_ref[...])
    kva = _dot(ckvn.astype(BF16), wkv_ref[...])
    ct, st = ct_ref[...], st_ref[...]
    kr = (kp * ct + pltpu.roll(kp, MLA_ROPE, 1) * st)[:, :MLA_ROPE].astype(BF16)
    ones = jnp.ones((p.shape[0], MLA_V), BF16)
    hw = 2 * LANES
    for h in range(MLA_HEADS):
        qh = qa[:, h * hw:(h + 1) * hw]
        pe = qh[:, MLA_NOPE:]
        q_ref[0, h, :, :MLA_NOPE] = qh[:, :MLA_NOPE].astype(BF16)
        q_ref[0, h, :, MLA_NOPE:] = (pe * ct + pltpu.roll(pe, MLA_ROPE, 1) * st)[:, :MLA_ROPE].astype(BF16)
        kvh = kva[:, h * hw:(h + 1) * hw]
        k_ref[0, h, :, :MLA_NOPE] = kvh[:, :MLA_NOPE].astype(BF16)
        k_ref[0, h, :, MLA_NOPE:] = kr
        v_ref[0, h, :, :MLA_V] = kvh[:, MLA_NOPE:].astype(BF16)
        v_ref[0, h, :, MLA_V:] = ones


def mla_pre(p, mla_blk, ct, st, qn, kn, wq, wkv, batch, tm=512):
    T = p.shape[0]
    seq = T // batch
    tm = _pick(seq, (tm, 256, 128))
    nt = seq // tm
    dqk = MLA_NOPE + MLA_ROPE
    head_spec = lambda d: pl.BlockSpec((1, MLA_HEADS, tm, d), lambda i: (i // nt, 0, i % nt, 0))
    full = lambda a: pl.BlockSpec(a.shape, lambda i: (0,) * a.ndim)
    return pl.pallas_call(
        _mla_pre_kernel,
        grid=(T // tm,),
        in_specs=[pl.BlockSpec((tm, L0_MLA_W), lambda i: (i, mla_blk)),
                  pl.BlockSpec((tm, LANES), lambda i: (i, 0)),
                  pl.BlockSpec((tm, LANES), lambda i: (i, 0)),
                  full(qn), full(kn), full(wq), full(wkv)],
        out_specs=[head_spec(dqk), head_spec(dqk), head_spec(2 * MLA_V)],
        out_shape=[jax.ShapeDtypeStruct((batch, MLA_HEADS, seq, dqk), BF16),
                   jax.ShapeDtypeStruct((batch, MLA_HEADS, seq, dqk), BF16),
                   jax.ShapeDtypeStruct((batch, MLA_HEADS, seq, 2 * MLA_V), BF16)],
        compiler_params=_cparams(("parallel",)),
        name="mla_pre",
    )(p, ct, st, qn, kn, wq, wkv)


def _softplus(x):
    return jnp.maximum(x, 0.0) + jnp.log(1.0 + jnp.exp(-jnp.abs(x)))


def _group_sum(x, ones_bd, terms):
    parts = _split3(x)[:terms]
    out = _dot(parts[0], ones_bd)
    for t in parts[1:]:
        out = out + _dot(t, ones_bd)
    return out


def _rwkv_pre_kernel(k_ref, lora_ref, wl_ref, g2_ref, w0_ref, a0_ref, kk_ref, ka_ref, ones_ref,
                     lw_ref, k2_ref, av_ref, bv_ref, g_ref):
    lora = lora_ref[...]
    lane = lax.broadcasted_iota(jnp.int32, (lora.shape[0], LANES), 1)
    wa_in = jnp.where(lane < DECAY_LORA, jnp.tanh(lora[:, :LANES]), lora[:, :LANES])
    wa = _dot(wa_in.astype(BF16), wl_ref[...])
    g_ref[...] = _dot(jax.nn.sigmoid(lora[:, LANES:]).astype(BF16), g2_ref[...])
    w_log = -_softplus(-(w0_ref[...] + wa[:, :RWKV_DIM])) - 0.5
    lw_ref[...] = -jnp.exp(w_log)
    a = jax.nn.sigmoid(a0_ref[...] + wa[:, RWKV_DIM:])
    k = k_ref[...]
    kk = k * kk_ref[...]
    kkn = kk * lax.rsqrt(_group_sum(kk * kk, ones_ref[...], 2) + 1e-6)
    k2_ref[...] = k * (1.0 + (a - 1.0) * ka_ref[...])
    av_ref[...] = -kkn
    bv_ref[...] = kkn * a


def rwkv_pre(p, k_blk, lora_blk, wl, g2, w0, a0, kk, ka, ones_bd, tm=512):
    T = p.shape[0]
    tm = _pick(T, (tm, 256, 128, 64))
    row = pl.BlockSpec((tm, RWKV_DIM), lambda i: (i, 0))
    full = lambda a: pl.BlockSpec(a.shape, lambda i: (0,) * a.ndim)
    vec = lambda a: a.reshape(1, RWKV_DIM)
    args = (wl, g2, vec(w0), vec(a0), vec(kk), vec(ka), ones_bd)
    return pl.pallas_call(
        _rwkv_pre_kernel,
        grid=(T // tm,),
        in_specs=[pl.BlockSpec((tm, RWKV_DIM), lambda i: (i, k_blk)),
                  pl.BlockSpec((tm, L0_LORA_W), lambda i: (i, lora_blk))] + [full(a) for a in args],
        out_specs=[row] * 5,
        out_shape=[jax.ShapeDtypeStruct((T, RWKV_DIM), F32)] * 5,
        compiler_params=_cparams(("parallel",)),
        name="rwkv_pre",
    )(p, p, *args)


def _rwkv_post_kernel(y_ref, r_ref, k2_ref, v_ref, g_ref, rk_ref, gw_ref, gb_ref, ones_ref, o_ref):
    ones_bd = ones_ref[...]
    y = y_ref[...]
    yc = y - _group_sum(y, ones_bd, 2) * (1.0 / RWKV_HEAD)
    var = _group_sum(yc * yc, ones_bd, 1) * (1.0 / RWKV_HEAD)
    yn = yc * lax.rsqrt(var + RWKV_GN_EPS) * gw_ref[...] + gb_ref[...]
    bonus = _group_sum(r_ref[...] * k2_ref[...] * rk_ref[...], ones_bd, 2) * v_ref[...]
    o_ref[...] = ((yn + bonus) * g_ref[...]).astype(o_ref.dtype)


def rwkv_post(y, p, r_blk, v_blk, k2, g, rk, gw, gb, ones_bd, tm=512):
    T = y.shape[0]
    tm = _pick(T, (tm, 256, 128, 64))
    row = pl.BlockSpec((tm, RWKV_DIM), lambda i: (i, 0))
    full = lambda a: pl.BlockSpec(a.shape, lambda i: (0,) * a.ndim)
    vec = lambda a: a.reshape(1, RWKV_DIM)
    args = (vec(rk), vec(gw), vec(gb), ones_bd)
    return pl.pallas_call(
        _rwkv_post_kernel,
        grid=(T // tm,),
        in_specs=[row, pl.BlockSpec((tm, RWKV_DIM), lambda i: (i, r_blk)), row,
                  pl.BlockSpec((tm, RWKV_DIM), lambda i: (i, v_blk)), row] + [full(a) for a in args],
        out_specs=row,
        out_shape=jax.ShapeDtypeStruct((T, RWKV_DIM), BF16),
        compiler_params=_cparams(("parallel",)),
        name="rwkv_post",
    )(y, p, k2, p, g, *args)


def _mm2_ln_kernel(a1_ref, a2_ref, w1_ref, w2_ref, x_ref, g_ref, b_ref, o_ref):
    y = ALPHA * x_ref[...] + _dot(a1_ref[...], w1_ref[...]) + _dot(a2_ref[...], w2_ref[...])
    mu = jnp.mean(y, axis=-1, keepdims=True)
    yc = y - mu
    var = jnp.mean(yc * yc, axis=-1, keepdims=True)
    o_ref[...] = yc * lax.rsqrt(var + LN_EPS) * g_ref[...] + b_ref[...]


def mm2_ln(a1, a2, w, x, g, b, tm=512):
    M, K1 = a1.shape
    K2 = a2.shape[1]
    N = w.shape[1]
    tm = _pick(M, (tm, 256, 128, 64, 32, 16, 8))
    w = w.astype(BF16)
    row = lambda k: pl.BlockSpec((tm, k), lambda i: (i, 0))
    return pl.pallas_call(
        _mm2_ln_kernel,
        grid=(M // tm,),
        in_specs=[row(K1), row(K2),
                  pl.BlockSpec((K1, N), lambda i: (0, 0)),
                  pl.BlockSpec((K2, N), lambda i: (0, 0)),
                  row(N),
                  pl.BlockSpec((1, N), lambda i: (0, 0)),
                  pl.BlockSpec((1, N), lambda i: (0, 0))],
        out_specs=row(N),
        out_shape=jax.ShapeDtypeStruct((M, N), F32),
        compiler_params=_cparams(("parallel",)),
        name="mm2_ln",
    )(a1, a2, w[:K1], w[K1:], x, g.reshape(1, N), b.reshape(1, N))


def _half_swap(n):
    return jnp.concatenate([jnp.arange(n // 2, n), jnp.arange(0, n // 2)])


def _mla_rwkv_mixer(xb, B, S, ct, st, w_in, q_norm, w_uq, kv_norm, w_ukv, rwkv_mu, rwkv_w0, rwkv_w2,
                    rwkv_a0, rwkv_a2, rwkv_g2, rwkv_k_k, rwkv_k_a, rwkv_r_k, rwkv_gn_w, rwkv_gn_b,
                    w_out, ln_g, ln_b):
    T = B * S
    D = xb.shape[1]
    o_kv = MLA_Q_RANK
    o_pe = o_kv + MLA_KV_RANK
    o_r = o_pe + MLA_ROPE
    o_k, o_v = o_r + RWKV_DIM, o_r + 2 * RWKV_DIM
    o_xw = o_r + 3 * RWKV_DIM
    o_xg = o_xw + DECAY_LORA + AAA_LORA
    zc = lambda n: jnp.zeros((D, n), F32)
    w_pe = w_in[:, o_pe:o_r]
    w0p = jnp.concatenate([w_in[:, o_r:o_xw], w_in[:, :o_kv], zc(CQ_PAD - MLA_Q_RANK), w_in[:, o_kv:o_pe],
                           w_pe, w_pe[:, _half_swap(MLA_ROPE)], w_in[:, o_xw:], zc(L0_N - L0_MLA_W - 3 * RWKV_DIM
                                                                                   - (w_in.shape[1] - o_xw))],
                          axis=1).astype(BF16)
    mu = rwkv_mu
    mu_p = jnp.concatenate([mu[:3 * RWKV_DIM], jnp.zeros((L0_MLA_W,), F32), mu[3 * RWKV_DIM:],
                            jnp.zeros((L0_N - L0_MLA_W - mu.shape[0],), F32)])
    p = proj_shift(xb, w0p, mu_p, S)
    dqk = MLA_NOPE + MLA_ROPE
    qs = dqk ** -0.5 * math.log2(math.e)
    wq = w_uq.reshape(MLA_Q_RANK, MLA_HEADS, dqk) * qs
    wq = jnp.concatenate([wq, wq[..., MLA_NOPE:][..., _half_swap(MLA_ROPE)]], -1).reshape(MLA_Q_RANK, -1)
    wq = jnp.pad(wq, ((0, CQ_PAD - MLA_Q_RANK), (0, 0))).astype(BF16)
    qn = jnp.pad(q_norm, (0, CQ_PAD - MLA_Q_RANK)).reshape(1, CQ_PAD)
    q, k, v = mla_pre(p, 3 * RWKV_DIM // L0_MLA_W, ct, st, qn, kv_norm.reshape(1, -1), wq,
                      w_ukv.astype(BF16), B)
    o_mla = flash_attention(q, k, v)
    zl = jnp.zeros((DECAY_LORA, RWKV_DIM), F32)
    wl = jnp.concatenate([jnp.concatenate([rwkv_w2, zl], 1), jnp.concatenate([zl, rwkv_a2], 1)], 0).astype(BF16)
    g2 = jnp.pad(rwkv_g2, ((0, L0_LORA_W - LANES - GATE_LORA), (0, 0))).astype(BF16)
    hid = jnp.arange(RWKV_DIM) // RWKV_HEAD
    ones_bd = (hid[:, None] == hid[None, :]).astype(BF16)
    lw, k2, av, bv, g = rwkv_pre(p, 1, (3 * RWKV_DIM + L0_MLA_W) // L0_LORA_W, wl, g2, rwkv_w0, rwkv_a0,
                                 rwkv_k_k, rwkv_k_a, ones_bd)
    y = rwkv7_chunked(p, 0, 2 * RWKV_DIM // LANES, lw, k2, av, bv, B)
    o_rwkv = rwkv_post(y, p, 0, 2, k2, g, rwkv_r_k.reshape(-1), rwkv_gn_w, rwkv_gn_b, ones_bd)
    return mm2_ln(o_mla, o_rwkv, w_out, xb, ln_g, ln_b)


GDN_QKV_W = 2 * GDN_QK_HEADS * GDN_DK + GDN_V_HEADS * GDN_DV


def _gdn_pre_kernel(x_ref, xh_ref, ba_ref, cw_ref, al_ref, dt_ref, q_ref, k_ref, v_ref, bg_ref, *,
                    tiles_per_seq):
    m = pl.program_id(0)
    x = x_ref[...]
    xh = jnp.where(m % tiles_per_seq == 0, 0.0, xh_ref[...])
    rows = lax.broadcasted_iota(jnp.int32, x.shape, 0)
    cw = cw_ref[...]
    acc = x * cw[GDN_CONV - 1:GDN_CONV]
    for j in range(1, GDN_CONV):
        sh = pltpu.roll(x, j, 0)
        for i in range(j):
            sh = jnp.where(rows == i, xh[SUBLANES - j + i:SUBLANES - j + i + 1], sh)
        acc = acc + sh * cw[GDN_CONV - 1 - j:GDN_CONV - j]
    y = acc * jax.nn.sigmoid(acc)
    qk_w = GDN_QK_HEADS * GDN_DK
    for h in range(GDN_QK_HEADS):
        cs = slice(h * GDN_DK, (h + 1) * GDN_DK)
        yq = y[:, cs]
        q_ref[:, cs] = yq * (lax.rsqrt(jnp.sum(yq * yq, -1, keepdims=True) + 1e-6) * GDN_DK ** -0.5)
        yk = y[:, qk_w + h * GDN_DK:qk_w + (h + 1) * GDN_DK]
        k_ref[:, cs] = yk * lax.rsqrt(jnp.sum(yk * yk, -1, keepdims=True) + 1e-6)
    v_ref[...] = y[:, 2 * qk_w:]
    ba = ba_ref[...]
    lane = lax.broadcasted_iota(jnp.int32, ba.shape, 1)
    g = -jnp.exp(al_ref[...]) * _softplus(ba + dt_ref[...])
    bg_ref[...] = jnp.where(lane < GDN_V_HEADS, jax.nn.sigmoid(ba), g)


def gdn_pre(p, ba_blk, conv_w, a_log, dt_bias, seq, tm=256):
    T = p.shape[0]
    tm = _pick(seq, (tm, 128, 64))
    hb = tm // SUBLANES
    qk_w = GDN_QK_HEADS * GDN_DK
    v_w = GDN_V_HEADS * GDN_DV
    lane_pad = lambda a: jnp.pad(a, (GDN_V_HEADS, LANES - 2 * GDN_V_HEADS)).reshape(1, LANES)
    row = lambda n: pl.BlockSpec((tm, n), lambda i: (i, 0))
    full = lambda shp: pl.BlockSpec(shp, lambda i: (0, 0))
    return pl.pallas_call(
        functools.partial(_gdn_pre_kernel, tiles_per_seq=seq // tm),
        grid=(T // tm,),
        in_specs=[row(GDN_QKV_W),
                  pl.BlockSpec((SUBLANES, GDN_QKV_W), lambda i: (jnp.maximum(i * hb - 1, 0), 0)),
                  pl.BlockSpec((tm, LANES), lambda i: (i, ba_blk)),
                  full((GDN_CONV, GDN_QKV_W)), full((1, LANES)), full((1, LANES))],
        out_specs=[row(qk_w), row(qk_w), row(v_w), row(LANES)],
        out_shape=[jax.ShapeDtypeStruct((T, qk_w), F32), jax.ShapeDtypeStruct((T, qk_w), F32),
                   jax.ShapeDtypeStruct((T, v_w), F32), jax.ShapeDtypeStruct((T, LANES), F32)],
        compiler_params=_cparams(("parallel",)),
        name="gdn_pre",
    )(p, p, p, conv_w, lane_pad(a_log), lane_pad(dt_bias))


def _l1_post_kernel(og_ref, z_ref, or_ref, rg_ref, gn_ref, rw_ref, rb_ref, o_ref):
    v_w = GDN_V_HEADS * GDN_DV
    for h in range(GDN_V_HEADS):
        cs = slice(h * LANES, (h + 1) * LANES)
        o = og_ref[:, cs]
        z = z_ref[:, cs]
        on = o * lax.rsqrt(jnp.mean(o * o, -1, keepdims=True) + RMS_EPS) * gn_ref[...]
        o_ref[:, cs] = (on * (z * jax.nn.sigmoid(z))).astype(o_ref.dtype)
        r = or_ref[:, cs]
        rc = r - jnp.mean(r, -1, keepdims=True)
        rn = rc * lax.rsqrt(jnp.mean(rc * rc, -1, keepdims=True) + LN_EPS) * rw_ref[:, cs] + rb_ref[:, cs]
        g = rg_ref[:, cs]
        o_ref[:, v_w + h * LANES:v_w + (h + 1) * LANES] = (rn * (g * jax.nn.sigmoid(g))).astype(o_ref.dtype)


def l1_post(o_gdn, o_ret, p, z_blk, rg_blk, gdn_norm, ret_gn_w, ret_gn_b, tm=512):
    T, W = o_gdn.shape
    tm = _pick(T, (tm, 256, 128, 64))
    row = pl.BlockSpec((tm, W), lambda i: (i, 0))
    return pl.pallas_call(
        _l1_post_kernel,
        grid=(T // tm,),
        in_specs=[row, pl.BlockSpec((tm, W), lambda i: (i, z_blk)), row,
                  pl.BlockSpec((tm, W), lambda i: (i, rg_blk)),
                  pl.BlockSpec((1, LANES), lambda i: (0, 0)),
                  pl.BlockSpec((1, W), lambda i: (0, 0)),
                  pl.BlockSpec((1, W), lambda i: (0, 0))],
        out_specs=pl.BlockSpec((tm, 2 * W), lambda i: (i, 0)),
        out_shape=jax.ShapeDtypeStruct((T, 2 * W), BF16),
        compiler_params=_cparams(("parallel",)),
        name="l1_post",
    )(o_gdn, p, o_ret, p, gdn_norm.reshape(1, LANES), ret_gn_w.reshape(1, W), ret_gn_b.reshape(1, W))


def _gdn_retention_mixer(xb, B, S, ct, st, w_in, gdn_conv_w, gdn_A_log, gdn_dt_bias, gdn_norm,
                         ret_gn_w, ret_gn_b, w_out, ln_g, ln_b):
    D = xb.shape[1]
    v_w = GDN_V_HEADS * GDN_DV
    n_gdn = GDN_QKV_W + v_w
    w1p = jnp.concatenate([w_in[:, :n_gdn], w_in[:, n_gdn + 2 * GDN_V_HEADS:],
                           w_in[:, n_gdn:n_gdn + 2 * GDN_V_HEADS],
                           jnp.zeros((D, LANES - 2 * GDN_V_HEADS), F32)], axis=1)
    p = mm(xb, w1p, tm=1024, tn=896)
    q, k, v, bg = gdn_pre(p, (w1p.shape[1] - LANES) // LANES, gdn_conv_w, gdn_A_log, gdn_dt_bias, S)
    o_gdn = gdn_chunked(q, k, v, bg, B)
    qd = RET_HEADS * RET_DK
    o_ret = retention_chunked(p, n_gdn // qd, n_gdn // qd + 1, (n_gdn + 2 * qd) // v_w, ct, st, B)
    m1 = l1_post(o_gdn, o_ret, p, GDN_QKV_W // v_w, (n_gdn + 2 * qd + v_w) // v_w, gdn_norm, ret_gn_w, ret_gn_b)
    return mm_ln_full(m1, w_out, xb, ln_g, ln_b, tm=512)


def _conv_ffn_ln(x, S, w_gate, w_val, conv_w, conv_b, w_down, ln_g, ln_b):
    h = ffn1(x, w_gate, w_val, conv_w, conv_b, S)
    return mm_ln_full(h, w_down, x, ln_g, ln_b)


def kernel(x, positions, l0_w_in, l0_q_norm, l0_w_uq, l0_kv_norm, l0_w_ukv, l0_rwkv_mu, l0_rwkv_w0, l0_rwkv_w2, l0_rwkv_a0, l0_rwkv_a2, l0_rwkv_g2, l0_rwkv_k_k, l0_rwkv_k_a, l0_rwkv_r_k, l0_rwkv_gn_w, l0_rwkv_gn_b, l0_w_out, l0_ln1_g, l0_ln1_b, l0_ffn_w_gate, l0_ffn_w_val, l0_ffn_conv_w, l0_ffn_conv_b, l0_ffn_w_down, l0_ln2_g, l0_ln2_b, l1_w_in, l1_gdn_conv_w, l1_gdn_A_log, l1_gdn_dt_bias, l1_gdn_norm, l1_ret_gn_w, l1_ret_gn_b, l1_w_out, l1_ln1_g, l1_ln1_b, l1_ffn_w_gate, l1_ffn_w_val, l1_ffn_conv_w, l1_ffn_conv_b, l1_ffn_w_down, l1_ln2_g, l1_ln2_b):
    B, S, D = x.shape
    T = B * S
    xb = x.reshape(T, D)
    assert MLA_ROPE == RET_DK
    ct, st = _rope_tables(positions, MLA_ROPE)
    xb = _mla_rwkv_mixer(xb, B, S, ct, st, l0_w_in, l0_q_norm, l0_w_uq, l0_kv_norm, l0_w_ukv, l0_rwkv_mu,
                         l0_rwkv_w0, l0_rwkv_w2, l0_rwkv_a0, l0_rwkv_a2, l0_rwkv_g2, l0_rwkv_k_k, l0_rwkv_k_a,
                         l0_rwkv_r_k, l0_rwkv_gn_w, l0_rwkv_gn_b, l0_w_out, l0_ln1_g, l0_ln1_b)
    xb = _conv_ffn_ln(xb, S, l0_ffn_w_gate, l0_ffn_w_val, l0_ffn_conv_w, l0_ffn_conv_b, l0_ffn_w_down,
                      l0_ln2_g, l0_ln2_b)
    xb = _gdn_retention_mixer(xb, B, S, ct, st, l1_w_in, l1_gdn_conv_w, l1_gdn_A_log, l1_gdn_dt_bias,
                              l1_gdn_norm, l1_ret_gn_w, l1_ret_gn_b, l1_w_out, l1_ln1_g, l1_ln1_b)
    xb = _conv_ffn_ln(xb, S, l1_ffn_w_gate, l1_ffn_w_val, l1_ffn_conv_w, l1_ffn_conv_b, l1_ffn_w_down,
                      l1_ln2_g, l1_ln2_b)
    return xb.reshape(B, S, D)
```

```python
import functools
import math

import jax
import jax.numpy as jnp
from jax import lax
from jax.experimental import pallas as pl
from jax.experimental.pallas import tpu as pltpu

F32 = jnp.float32
BF16 = jnp.bfloat16

DEPTH = 2
ALPHA = (2 * DEPTH) ** 0.25
LN_EPS = 1e-5
RMS_EPS = 1e-6
ROPE_THETA = 10000.0

MLA_HEADS = 8
MLA_Q_RANK = 448
MLA_KV_RANK = 128
MLA_NOPE = 128
MLA_ROPE = 64
MLA_V = 128
RWKV_HEADS = 16
RWKV_HEAD = 64
RWKV_DIM = RWKV_HEADS * RWKV_HEAD
DECAY_LORA = 64
AAA_LORA = 64
GATE_LORA = 160
RWKV_GN_EPS = 64e-5
GDN_QK_HEADS = 4
GDN_V_HEADS = 8
GDN_DK = 128
GDN_DV = 128
GDN_CONV = 4
RET_HEADS = 8
RET_DK = 64
RET_DV = 128
FFN_CONV = 3

CHUNK = 64
LANES = 128
SUBLANES = 8
VMEM_LIMIT = 56 * 1024 * 1024


def _cparams(sem):
    return pltpu.CompilerParams(dimension_semantics=sem, vmem_limit_bytes=VMEM_LIMIT)


def _dot(a, b):
    return jnp.dot(a, b, preferred_element_type=F32)


def _dot_nt(a, b):
    return lax.dot_general(a, b, (((1,), (1,)), ((), ())), preferred_element_type=F32)


def _dot_tn(a, b):
    return lax.dot_general(a, b, (((0,), (0,)), ((), ())), preferred_element_type=F32)


def _split3(x):
    hi = x.astype(BF16)
    r1 = x - hi.astype(F32)
    mid = r1.astype(BF16)
    lo = (r1 - mid.astype(F32)).astype(BF16)
    return hi, mid, lo


def _dot_exact_lhs(m_bf16, x):
    hi, mid, lo = _split3(x)
    return _dot(m_bf16, hi) + _dot(m_bf16, mid) + _dot(m_bf16, lo)


def _mm_kernel(a_ref, w_ref, o_ref, *scratch, nk):
    if nk == 1:
        o_ref[...] = _dot(a_ref[...].astype(BF16), w_ref[...]).astype(o_ref.dtype)
        return
    acc_ref, = scratch
    k = pl.program_id(2)

    @pl.when(k == 0)
    def _():
        acc_ref[...] = jnp.zeros_like(acc_ref)

    acc_ref[...] += _dot(a_ref[...].astype(BF16), w_ref[...])

    @pl.when(k == nk - 1)
    def _():
        o_ref[...] = acc_ref[...].astype(o_ref.dtype)


def _pick(n, pref):
    for t in pref:
        if n % t == 0:
            return t
    return n


def mm(a, w, out_dtype=F32, tm=512, tn=None, tk=None):
    M, K = a.shape
    N = w.shape[1]
    tm = _pick(M, (tm, 256, 128, 64, 32, 16, 8))
    tn = tn or _pick(N, (512, 384, 256, 128))
    tk = tk or K
    nk = K // tk
    w = w.astype(BF16)
    scratch = [] if nk == 1 else [pltpu.VMEM((tm, tn), F32)]
    return pl.pallas_call(
        functools.partial(_mm_kernel, nk=nk),
        grid=(M // tm, N // tn, nk),
        in_specs=[pl.BlockSpec((tm, tk), lambda i, j, k: (i, k)),
                  pl.BlockSpec((tk, tn), lambda i, j, k: (k, j))],
        out_specs=pl.BlockSpec((tm, tn), lambda i, j, k: (i, j)),
        out_shape=jax.ShapeDtypeStruct((M, N), out_dtype),
        scratch_shapes=scratch,
        compiler_params=_cparams(("parallel", "parallel", "arbitrary")),
        name="mm",
    )(a, w)


def _mm_ln_kernel(a_ref, w_ref, x_ref, g_ref, b_ref, o_ref, acc_ref, *, nk):
    k = pl.program_id(1)

    @pl.when(k == 0)
    def _():
        acc_ref[...] = jnp.zeros_like(acc_ref)

    acc_ref[...] += _dot(a_ref[...].astype(BF16), w_ref[...])

    @pl.when(k == nk - 1)
    def _():
        y = ALPHA * x_ref[...] + acc_ref[...]
        mu = jnp.mean(y, axis=-1, keepdims=True)
        yc = y - mu
        var = jnp.mean(yc * yc, axis=-1, keepdims=True)
        o_ref[...] = yc * lax.rsqrt(var + LN_EPS) * g_ref[...] + b_ref[...]


def _mm_ln_full_kernel(a_ref, w_ref, x_ref, g_ref, b_ref, o_ref):
    y = ALPHA * x_ref[...] + _dot(a_ref[...].astype(BF16), w_ref[...])
    mu = jnp.mean(y, axis=-1, keepdims=True)
    yc = y - mu
    var = jnp.mean(yc * yc, axis=-1, keepdims=True)
    o_ref[...] = yc * lax.rsqrt(var + LN_EPS) * g_ref[...] + b_ref[...]


def mm_ln_full(a, w, x, g, b, tm=256):
    M, K = a.shape
    N = w.shape[1]
    tm = _pick(M, (tm, 128, 64, 32, 16, 8))
    const = lambda shp: pl.BlockSpec(shp, lambda i: (0, 0), pipeline_mode=pl.Buffered(1))
    return pl.pallas_call(
        _mm_ln_full_kernel,
        grid=(M // tm,),
        in_specs=[pl.BlockSpec((tm, K), lambda i: (i, 0)),
                  const((K, N)),
                  pl.BlockSpec((tm, N), lambda i: (i, 0)),
                  const((1, N)), const((1, N))],
        out_specs=pl.BlockSpec((tm, N), lambda i: (i, 0)),
        out_shape=jax.ShapeDtypeStruct((M, N), F32),
        compiler_params=_cparams(("parallel",)),
        name="mm_ln_full",
    )(a, w.astype(BF16), x, g.reshape(1, N), b.reshape(1, N))


def mm_ln(a, w, x, g, b, tm=512, tk=None):
    M, K = a.shape
    N = w.shape[1]
    tm = _pick(M, (tm, 256, 128, 64, 32, 16, 8))
    tk = tk or _pick(K, (1024, 1408, 512, 256, 128))
    nk = K // tk
    return pl.pallas_call(
        functools.partial(_mm_ln_kernel, nk=nk),
        grid=(M // tm, nk),
        in_specs=[pl.BlockSpec((tm, tk), lambda i, k: (i, k)),
                  pl.BlockSpec((tk, N), lambda i, k: (k, 0)),
                  pl.BlockSpec((tm, N), lambda i, k: (i, 0)),
                  pl.BlockSpec((1, N), lambda i, k: (0, 0)),
                  pl.BlockSpec((1, N), lambda i, k: (0, 0))],
        out_specs=pl.BlockSpec((tm, N), lambda i, k: (i, 0)),
        out_shape=jax.ShapeDtypeStruct((M, N), F32),
        scratch_shapes=[pltpu.VMEM((tm, N), F32)],
        compiler_params=_cparams(("parallel", "arbitrary")),
        name="mm_ln",
    )(a, w.astype(BF16), x, g.reshape(1, N), b.reshape(1, N))


def _ffn1_kernel(x_ref, xh_ref, wg_ref, wv_ref, cw_ref, cb_ref, o_ref, *, tiles_per_seq):
    m = pl.program_id(0)
    x = x_ref[...].astype(BF16)
    g = _dot(x, wg_ref[...])
    gh = _dot(xh_ref[...].astype(BF16), wg_ref[...])
    gh = jnp.where(m % tiles_per_seq == 0, 0.0, gh)
    rows = lax.broadcasted_iota(jnp.int32, g.shape, 0)
    g1 = jnp.where(rows == 0, gh[SUBLANES - 1:SUBLANES], pltpu.roll(g, 1, 0))
    g2 = jnp.where(rows == 0, gh[SUBLANES - 2:SUBLANES - 1],
                   jnp.where(rows == 1, gh[SUBLANES - 1:SUBLANES], pltpu.roll(g, 2, 0)))
    cw = cw_ref[...]
    h = g * cw[2:3] + g1 * cw[1:2] + g2 * cw[0:1] + cb_ref[...]
    v = _dot(x, wv_ref[...])
    o_ref[...] = (h * jax.nn.sigmoid(h) * v).astype(o_ref.dtype)


def ffn1(x, w_gate, w_val, conv_w, conv_b, seq, tm=1024, tf=512):
    M, D = x.shape
    Fd = w_gate.shape[1]
    tm = _pick(seq, (tm, 256, 128, 64, 32, 16, 8))
    tf = _pick(Fd, (tf, 256, 128))
    hb = tm // SUBLANES
    return pl.pallas_call(
        functools.partial(_ffn1_kernel, tiles_per_seq=seq // tm),
        grid=(M // tm, Fd // tf),
        in_specs=[pl.BlockSpec((tm, D), lambda i, j: (i, 0)),
                  pl.BlockSpec((SUBLANES, D), lambda i, j: (jnp.maximum(i * hb - 1, 0), 0)),
                  pl.BlockSpec((D, tf), lambda i, j: (0, j)),
                  pl.BlockSpec((D, tf), lambda i, j: (0, j)),
                  pl.BlockSpec((FFN_CONV, tf), lambda i, j: (0, j)),
                  pl.BlockSpec((1, tf), lambda i, j: (0, j))],
        out_specs=pl.BlockSpec((tm, tf), lambda i, j: (i, j)),
        out_shape=jax.ShapeDtypeStruct((M, Fd), BF16),
        compiler_params=_cparams(("parallel", "parallel")),
        name="ffn1",
    )(x, x, w_gate.astype(BF16), w_val.astype(BF16), conv_w, conv_b.reshape(1, Fd))


def _flash_kernel(q_ref, k_ref, v_ref, o_ref, m_ref, acc_ref, s_ref, *, tq, tk, ts, dv):
    qi = pl.program_id(2)
    nsub = tq // ts
    ndiag = tq // tk
    nfull = qi * ndiag
    m_ref[...] = jnp.full_like(m_ref, -jnp.inf)
    acc_ref[...] = jnp.zeros_like(acc_ref)

    def above_diagonal(r, d):
        return (r + 1) * ts <= d * tk

    def scores(r, blk):
        k = k_ref[0, 0, pl.ds(pl.multiple_of(blk * tk, tk), tk), :]
        return _dot_nt(q_ref[0, 0, r * ts:(r + 1) * ts, :], k)

    def consume(blk, diag, nxt_diag):
        v = v_ref[0, 0, pl.ds(pl.multiple_of(blk * tk, tk), tk), :]
        for r in range(nsub):
            rs = slice(r * ts, (r + 1) * ts)
            live = diag is None or not above_diagonal(r, diag)
            if live:
                s = s_ref[rs, :]
            if nxt_diag is None or (nxt_diag >= 0 and not above_diagonal(r, nxt_diag)):
                s_ref[rs, :] = scores(r, blk + 1)
            if not live:
                continue
            if diag is not None and r * ts < (diag + 1) * tk - 1:
                rowi = lax.broadcasted_iota(jnp.int32, s.shape, 0) + r * ts
                coli = lax.broadcasted_iota(jnp.int32, s.shape, 1) + diag * tk
                s = jnp.where(coli <= rowi, s, -jnp.inf)
            m_old = m_ref[rs, :]
            m_new = jnp.maximum(m_old, jnp.max(s, axis=-1, keepdims=True))
            p = jnp.exp2(s - m_new).astype(BF16)
            acc_ref[rs, :] = jnp.exp2(m_old - m_new) * acc_ref[rs, :] + _dot(p, v)
            m_ref[rs, :] = m_new

    for r in range(nsub):
        s_ref[r * ts:(r + 1) * ts, :] = scores(r, 0)

    def body(j, carry):
        consume(2 * j, None, None)
        consume(2 * j + 1, None, None)
        return carry

    lax.fori_loop(0, nfull // 2, body, 0)

    @pl.when(nfull % 2 == 1)
    def _():
        consume(nfull - 1, None, None)

    for d in range(ndiag):
        consume(nfull + d, d, d + 1 if d + 1 < ndiag else -1)
    acc = acc_ref[...]
    o_ref[...] = (acc[:, :dv] / acc[:, dv:]).astype(o_ref.dtype)


def flash_attention(q, k, v, tq=1024, tk=1024, ts=512):
    B, H, S, dq = q.shape
    dv = v.shape[-1] // 2
    tq = _pick(S, (tq, 512, 256, 128))
    tk = _pick(tq, (tk, 256, 128))
    nq = S // tq
    return pl.pallas_call(
        functools.partial(_flash_kernel, tq=tq, tk=tk, ts=_pick(tq, (ts, 128)), dv=dv),
        grid=(B, H, nq),
        in_specs=[pl.BlockSpec((1, 1, tq, dq), lambda b, h, i: (b, h, i, 0)),
                  pl.BlockSpec((1, 1, S, dq), lambda b, h, i: (b, h, 0, 0)),
                  pl.BlockSpec((1, 1, S, 2 * dv), lambda b, h, i: (b, h, 0, 0))],
        out_specs=pl.BlockSpec((tq, dv), lambda b, h, i: (b * nq + i, h)),
        out_shape=jax.ShapeDtypeStruct((B * S, H * dv), BF16),
        scratch_shapes=[pltpu.VMEM((tq, 1), F32), pltpu.VMEM((tq, 2 * dv), F32), pltpu.VMEM((tq, tk), F32)],
        compiler_params=_cparams(("parallel", "parallel", "arbitrary")),
        name="flash",
    )(q, k, v)


def _tri_inv_kernel(a_ref, t_ref):
    C = a_ref.shape[1]
    kk = lax.broadcasted_iota(jnp.int32, (SUBLANES, LANES), 0)
    zero = jnp.zeros((SUBLANES, LANES), F32)
    for i in range(C):
        ng = i // SUBLANES + 1
        acc = [jnp.where(kk == i - g * SUBLANES, 1.0, 0.0).astype(F32) for g in range(ng)]
        for j in range(i):
            a_ij = a_ref[0, i, j:j + 1, :]
            for g in range(j // SUBLANES + 1):
                acc[g] = acc[g] + a_ij * t_ref[0, j, g * SUBLANES:(g + 1) * SUBLANES, :]
        for g in range(C // SUBLANES):
            t_ref[0, i, g * SUBLANES:(g + 1) * SUBLANES, :] = acc[g] if g < ng else zero


def tri_inv(a):
    n_in, C, _ = a.shape
    ng = -(-n_in // LANES)
    N = ng * LANES
    a = jnp.pad(a, ((0, N - n_in), (0, 0), (0, 0)))
    al = a.reshape(ng, LANES, C * C).transpose(0, 2, 1).reshape(ng, C, C, LANES)
    tl = pl.pallas_call(
        _tri_inv_kernel,
        grid=(ng,),
        in_specs=[pl.BlockSpec((1, C, C, LANES), lambda i: (i, 0, 0, 0))],
        out_specs=pl.BlockSpec((1, C, C, LANES), lambda i: (i, 0, 0, 0)),
        out_shape=jax.ShapeDtypeStruct((ng, C, C, LANES), F32),
        compiler_params=_cparams(("parallel",)),
        name="tri_inv",
    )(al)
    return tl.reshape(ng, C * C, LANES).transpose(0, 2, 1).reshape(N, C, C)[:n_in]


def _tri_inv_pairs_kernel(a_ref, t_ref, al_ref, tl_ref):
    C = al_ref.shape[1]
    for i in range(C):
        mt = a_ref[pl.ds(i, LANES, stride=C), :].T
        al_ref[0, i] = mt[:C]
        al_ref[1, i] = mt[C:]
    for hh in range(2):
        _tri_inv_kernel(al_ref.at[pl.ds(hh, 1)], tl_ref.at[pl.ds(hh, 1)])
    for i in range(C):
        mt = jnp.concatenate([tl_ref[0, i], tl_ref[1, i]], axis=0)
        t_ref[pl.ds(i, LANES, stride=C), :] = mt.T


def tri_inv_pairs(a):
    n, m, C, _ = a.shape
    tiles = n * m
    ng = -(-tiles // LANES)
    a2 = jnp.pad(a.reshape(tiles * C, 2 * C), ((0, (ng * LANES - tiles) * C), (0, 0)))
    t = pl.pallas_call(
        _tri_inv_pairs_kernel,
        grid=(ng,),
        in_specs=[pl.BlockSpec((LANES * C, 2 * C), lambda i: (i, 0))],
        out_specs=pl.BlockSpec((LANES * C, 2 * C), lambda i: (i, 0)),
        out_shape=jax.ShapeDtypeStruct((ng * LANES * C, 2 * C), F32),
        scratch_shapes=[pltpu.VMEM((2, C, C, LANES), F32), pltpu.VMEM((2, C, C, LANES), F32)],
        compiler_params=_cparams(("parallel",)),
        name="tri_inv_pairs",
    )(a2)
    return t[:tiles * C].reshape(n, m, C, 2 * C)


def _chunk_tri(C, G):
    idx = jnp.arange(C * G)
    return ((idx[None, :] <= idx[:, None]) & (idx[None, :] // C == idx[:, None] // C)).astype(BF16)


def _chunk_masks():
    C = CHUNK
    row = lax.broadcasted_iota(jnp.int32, (C, C), 0)
    col = lax.broadcasted_iota(jnp.int32, (C, C), 1)
    return row, col


def _rwkv_a_kernel(r_ref, lw_ref, k_ref, v_ref, av_ref, bv_ref, tri_ref,
                   aab_ref, mrb_ref, at_ref, akv_ref, rt_ref, bh_ref, yk_ref, kv_ref, egl_ref, *, G):
    C = CHUNK
    rowp = lax.broadcasted_iota(jnp.int32, (C, LANES), 0)
    lane = lax.broadcasted_iota(jnp.int32, (C, LANES), 1)
    strict = lane % C < rowp
    incl = lane % C <= rowp
    head0 = lane < RWKV_HEAD
    bd = (lax.broadcasted_iota(jnp.int32, (LANES, LANES), 0) // RWKV_HEAD ==
          lax.broadcasted_iota(jnp.int32, (LANES, LANES), 1) // RWKV_HEAD)

    def by_head(x):
        return jnp.concatenate([jnp.where(head0, x, 0.0), jnp.where(head0, 0.0, x)], axis=0).astype(BF16)

    lw_all = lw_ref[...]
    gc_all = _dot_exact_lhs(tri_ref[...], lw_all)
    sls = [slice(g * C, (g + 1) * C) for g in range(G)]
    at, rt, vb, egl, gl, xb, xk = [], [], [], [], [], [], []
    for sl in sls:
        gc = gc_all[sl]
        gl.append(gc[C - 1:C, :])
        ieg = jnp.exp(-gc)
        egl.append(jnp.exp(gl[-1] - gc))
        at.append(av_ref[sl, :] * jnp.exp(gc - lw_all[sl]))
        rt.append(r_ref[sl, :] * jnp.exp(gc))
        vb.append(by_head(v_ref[sl, :]))
        lhs = jnp.concatenate([at[-1], rt[-1]], axis=0).astype(BF16)
        xb.append(_dot_nt(lhs, by_head(bv_ref[sl, :] * ieg)))
        xk.append(_dot_nt(lhs, by_head(k_ref[sl, :] * ieg)))
    for g, sl in enumerate(sls):
        aab_ref[g, 0] = jnp.where(strict, xb[g][:C], 0.0)
        mrb_ref[g, 0] = jnp.where(incl, xb[g][C:], 0.0).astype(BF16)
        akv_ref[sl, :] = _dot(jnp.where(strict, xk[g][:C], 0.0).astype(BF16), vb[g]).astype(BF16)
        yk_ref[sl, :] = _dot(jnp.where(incl, xk[g][C:], 0.0).astype(BF16), vb[g])
    for g, sl in enumerate(sls):
        at_ref[sl, :] = at[g].astype(BF16)
        rt_ref[sl, :] = rt[g].astype(BF16)
        bh_ref[sl, :] = (bv_ref[sl, :] * egl[g]).astype(BF16)
        kv_ref[g, 0] = jnp.where(bd, _dot_tn(v_ref[sl, :].astype(BF16), (k_ref[sl, :] * egl[g]).astype(BF16)), 0.0)
        egl_ref[g] = jnp.exp(gl[g])


def _rwkv_c_kernel(at_ref, akv_ref, rt_ref, bh_ref, yk_ref, t_ref, mrb_ref, kv_ref, egl_ref,
                   y_ref, s_ref, *, G):
    C = CHUNK
    npair = RWKV_HEADS // 2

    @pl.when(pl.program_id(1) == 0)
    def _():
        s_ref[...] = jnp.zeros_like(s_ref)

    lane = lax.broadcasted_iota(jnp.int32, (C, LANES), 1)
    head0 = lane < RWKV_HEAD
    head0w = jnp.concatenate([head0, head0], axis=1)
    bd = (lax.broadcasted_iota(jnp.int32, (LANES, LANES), 0) // RWKV_HEAD ==
          lax.broadcasted_iota(jnp.int32, (LANES, LANES), 1) // RWKV_HEAD)
    zero = jnp.zeros((), BF16)

    def by_head(x, m):
        return jnp.concatenate([jnp.where(m, x, zero), jnp.where(m, zero, x)], axis=0)

    def body(g, carry):
        sl = pl.ds(pl.multiple_of(g * C, C), C)
        cols = [slice(p * LANES, (p + 1) * LANES) for p in range(npair)]
        w, s, sb, u, ub, y = [], [], [], [], [], []
        for p, cs in enumerate(cols):
            both = jnp.concatenate([at_ref[sl, cs], akv_ref[sl, cs]], axis=1)
            w.append(_dot(t_ref[g, p].astype(BF16), by_head(both, head0w)))
            s.append(s_ref[p])
            sb.append(s[p].astype(BF16))
        for p, cs in enumerate(cols):
            u.append(_dot_nt(w[p][:, :LANES].astype(BF16), sb[p]) + w[p][:, LANES:])
            ub.append(u[p].astype(BF16))
            y.append(_dot_nt(rt_ref[sl, cs], sb[p]) + yk_ref[sl, cs])
        for p, cs in enumerate(cols):
            y_ref[sl, cs] = y[p] + _dot(mrb_ref[g, p], by_head(ub[p], head0))
            s_ref[p] = (s[p] * egl_ref[g, :, cs] + jnp.where(bd, _dot_tn(ub[p], bh_ref[sl, cs]), 0.0)
                        + kv_ref[g, p])
        return carry

    lax.fori_loop(0, G, body, 0)


def rwkv7_chunked(p, r_blk, v_blk, lw, k, av, bv, batch, G=4, Ga=4):
    T, HD = lw.shape
    C = CHUNK
    seq = T // batch
    nct = T // C
    G = _pick(seq // C, (G, 2, 1))
    Ga = _pick(seq // C, (Ga, 4, 2, 1))
    npair = HD // LANES
    row_spec = pl.BlockSpec((C * Ga, LANES), lambda i, p: (i, p))
    pair_mat = pl.BlockSpec((Ga, 1, C, LANES), lambda i, p: (i, p, 0, 0))
    outs = pl.pallas_call(
        functools.partial(_rwkv_a_kernel, G=Ga),
        grid=(nct // Ga, npair),
        in_specs=[pl.BlockSpec((C * Ga, LANES), lambda i, p: (i, r_blk + p)), row_spec, row_spec,
                  pl.BlockSpec((C * Ga, LANES), lambda i, p: (i, v_blk + p)), row_spec, row_spec,
                  pl.BlockSpec((C * Ga, C * Ga), lambda i, p: (0, 0))],
        out_specs=[pair_mat, pair_mat,
                   row_spec, row_spec, row_spec, row_spec, row_spec,
                   pl.BlockSpec((Ga, 1, LANES, LANES), lambda i, p: (i, p, 0, 0)),
                   pl.BlockSpec((Ga, 1, LANES), lambda i, p: (i, 0, p))],
        out_shape=[jax.ShapeDtypeStruct((nct, npair, C, LANES), F32),
                   jax.ShapeDtypeStruct((nct, npair, C, LANES), BF16),
                   jax.ShapeDtypeStruct((T, HD), BF16),
                   jax.ShapeDtypeStruct((T, HD), BF16),
                   jax.ShapeDtypeStruct((T, HD), BF16),
                   jax.ShapeDtypeStruct((T, HD), BF16),
                   jax.ShapeDtypeStruct((T, HD), F32),
                   jax.ShapeDtypeStruct((nct, npair, LANES, LANES), F32),
                   jax.ShapeDtypeStruct((nct, 1, HD), F32)],
        compiler_params=_cparams(("parallel", "parallel")),
        name="rwkv_a",
    )(p, lw, k, p, av, bv, _chunk_tri(C, Ga))
    aab, mrb, at, akv, rt, bh, yk, kv, egl = outs
    tinv = tri_inv_pairs(aab)
    ncb = seq // C // G
    wide = pl.BlockSpec((C * G, HD), lambda b, n: (b * ncb + n, 0))
    return pl.pallas_call(
        functools.partial(_rwkv_c_kernel, G=G),
        grid=(batch, ncb),
        in_specs=[wide, wide, wide, wide, wide,
                  pl.BlockSpec((G, npair, C, LANES), lambda b, n: (b * ncb + n, 0, 0, 0)),
                  pl.BlockSpec((G, npair, C, LANES), lambda b, n: (b * ncb + n, 0, 0, 0)),
                  pl.BlockSpec((G, npair, LANES, LANES), lambda b, n: (b * ncb + n, 0, 0, 0)),
                  pl.BlockSpec((G, 1, HD), lambda b, n: (b * ncb + n, 0, 0))],
        out_specs=wide,
        out_shape=jax.ShapeDtypeStruct((T, HD), F32),
        scratch_shapes=[pltpu.VMEM((npair, LANES, LANES), F32)],
        compiler_params=_cparams(("arbitrary", "arbitrary")),
        name="rwkv_c",
    )(at, akv, rt, bh, yk, tinv, mrb, kv, egl)


def _gdn_a_kernel(q_ref, k_ref, v_ref, bg_ref, grow_ref, selg_ref, selb_ref, tri_ref, triu_ref,
                  xm_ref, attn_ref, vb_ref, kcp_ref, qd_ref, kd_ref, egl_ref, *, G):
    C = CHUNK
    n = C * G
    rowp = lax.broadcasted_iota(jnp.int32, (C, LANES), 0)
    lane = lax.broadcasted_iota(jnp.int32, (C, LANES), 1)
    strict = lane % C < rowp
    incl = lane % C <= rowp
    head_a = lane < C
    bg3 = _split3(bg_ref[...])
    wide = lambda sel: _dot(bg3[0], sel) + _dot(bg3[1], sel) + _dot(bg3[2], sel)
    gcw = _dot_exact_lhs(tri_ref[...], wide(selg_ref[0]))
    hi, mid, lo = _split3(grow_ref[0, 0])
    triu = triu_ref[...]
    gcr_all = _dot(hi, triu) + _dot(mid, triu) + _dot(lo, triu)
    betaw = wide(selb_ref[0])
    sls = [slice(g * C, (g + 1) * C) for g in range(G)]
    kk, qk = [], []
    for sl in sls:
        kbf = k_ref[sl, :].astype(BF16)
        k2 = jnp.concatenate([kbf, kbf], axis=0)
        kk.append(_dot_nt(kbf, k2))
        qk.append(_dot_nt(q_ref[sl, :].astype(BF16), k2))
    for g, sl in enumerate(sls):
        gc = gcw[sl]
        bw = betaw[sl]
        diff = jnp.where(head_a, gc[:, :LANES], gc[:, LANES:]) - gcr_all[g:g + 1]
        decay = jnp.where(incl, jnp.exp(jnp.where(incl, diff, 0.0)), 0.0)
        bp = jnp.where(head_a, bw[:, :LANES], bw[:, LANES:])
        xm_ref[g, 0] = jnp.where(strict, -(kk[g] * bp) * decay, 0.0)
        attn_ref[g, 0] = jnp.where(incl, qk[g] * decay, 0.0).astype(BF16)
        eg = jnp.exp(gc)
        gl = gc[C - 1:C, :]
        k = k_ref[sl, :]
        q = q_ref[sl, :]
        k2w = jnp.concatenate([k, k], axis=1)
        vb_ref[sl, :] = (v_ref[sl, :] * bw).astype(BF16)
        kcp_ref[sl, :] = (k2w * bw * eg).astype(BF16)
        qd_ref[sl, :] = (jnp.concatenate([q, q], axis=1) * eg).astype(BF16)
        kd_ref[sl, :] = (k2w * jnp.exp(gl - gc)).astype(BF16)
        egl_ref[g] = jnp.exp(gl)


def _gdn_c_kernel(vb_ref, kcp_ref, qd_ref, kd_ref, t_ref, attn_ref, egl_ref, o_ref, s_ref, *, G):
    C = CHUNK
    npair = GDN_V_HEADS // 2

    @pl.when(pl.program_id(1) == 0)
    def _():
        s_ref[...] = jnp.zeros_like(s_ref)

    def blockdiag(x0, x1):
        z = jnp.zeros_like(x0)
        return jnp.concatenate([jnp.concatenate([x0, z], axis=1), jnp.concatenate([z, x1], axis=1)], axis=0)

    def body(g, carry):
        sl = pl.ds(pl.multiple_of(g * C, C), C)
        cols = [slice(h * LANES, (h + 1) * LANES) for h in range(GDN_V_HEADS)]
        both = [jnp.concatenate([vb_ref[sl, cs], kcp_ref[sl, cs]], axis=1) for cs in cols]
        w, s, sb, vnb, o = [], [], [], [], []
        for j in range(npair):
            wj = _dot(t_ref[g, j].astype(BF16), blockdiag(both[2 * j], both[2 * j + 1]))
            w += [wj[:, :2 * LANES], wj[:, 2 * LANES:]]
        for h in range(GDN_V_HEADS):
            s.append(s_ref[h])
            sb.append(s[h].astype(BF16))
        for h, cs in enumerate(cols):
            vnb.append((w[h][:, :LANES] - _dot(w[h][:, LANES:].astype(BF16), sb[h])).astype(BF16))
            o.append(_dot(qd_ref[sl, cs], sb[h]))
        for j in range(npair):
            att = _dot(attn_ref[g, j], blockdiag(vnb[2 * j], vnb[2 * j + 1]))
            for hh in range(2):
                h = 2 * j + hh
                o_ref[sl, cols[h]] = o[h] + att[:, hh * LANES:(hh + 1) * LANES]
                s_ref[h] = s[h] * egl_ref[g, :, cols[h]] + _dot_tn(kd_ref[sl, cols[h]], vnb[h])
        return carry

    lax.fori_loop(0, G, body, 0)


def gdn_chunked(q, k, v, bg, batch, G=4, Ga=8):
    T, VD = v.shape
    C = CHUNK
    seq = T // batch
    nct = T // C
    G = _pick(seq // C, (G, 2, 1))
    Ga = _pick(seq // C, (Ga, 4, 2, 1))
    nh = GDN_V_HEADS
    nq = GDN_QK_HEADS
    assert nh == 2 * nq and 2 * C == LANES
    g = bg[:, nh:2 * nh]
    lrow = jnp.arange(LANES)[None, :, None]
    lcol = jnp.arange(2 * LANES)[None, None, :] // LANES
    jj = jnp.arange(nq)[:, None, None]
    selb = (lrow == 2 * jj + lcol).astype(BF16)
    selg = (lrow == nh + 2 * jj + lcol).astype(BF16)
    grow = g.T.reshape(nq, 2, nct // Ga, Ga, C).transpose(0, 2, 3, 1, 4).reshape(nq, nct // Ga, Ga, LANES)
    grow = jnp.pad(grow, ((0, 0), (0, 0), (0, SUBLANES - Ga), (0, 0)))
    lidx = jnp.arange(LANES)
    triu2 = ((lidx[:, None] <= lidx[None, :]) & (lidx[:, None] // C == lidx[None, :] // C)).astype(BF16)
    qk_spec = pl.BlockSpec((C * Ga, LANES), lambda i, j: (i, j))
    v_spec = pl.BlockSpec((C * Ga, 2 * LANES), lambda i, j: (i, j))
    sel_spec = pl.BlockSpec((1, LANES, 2 * LANES), lambda i, j: (j, 0, 0))
    pair_mat = pl.BlockSpec((Ga, 1, C, LANES), lambda i, j: (i, j, 0, 0))
    outs = pl.pallas_call(
        functools.partial(_gdn_a_kernel, G=Ga),
        grid=(nct // Ga, nq),
        in_specs=[qk_spec, qk_spec, v_spec, pl.BlockSpec((C * Ga, LANES), lambda i, j: (i, 0)),
                  pl.BlockSpec((1, 1, SUBLANES, LANES), lambda i, j: (j, i, 0, 0)), sel_spec, sel_spec,
                  pl.BlockSpec((C * Ga, C * Ga), lambda i, j: (0, 0)),
                  pl.BlockSpec((LANES, LANES), lambda i, j: (0, 0))],
        out_specs=[pair_mat, pair_mat,
                   v_spec, v_spec, v_spec, v_spec,
                   pl.BlockSpec((Ga, 1, 2 * LANES), lambda i, j: (i, 0, j))],
        out_shape=[jax.ShapeDtypeStruct((nct, nq, C, LANES), F32),
                   jax.ShapeDtypeStruct((nct, nq, C, LANES), BF16),
                   jax.ShapeDtypeStruct((T, VD), BF16),
                   jax.ShapeDtypeStruct((T, VD), BF16),
                   jax.ShapeDtypeStruct((T, VD), BF16),
                   jax.ShapeDtypeStruct((T, VD), BF16),
                   jax.ShapeDtypeStruct((nct, 1, VD), F32)],
        compiler_params=_cparams(("parallel", "parallel")),
        name="gdn_a",
    )(q, k, v, bg, grow, selg, selb, _chunk_tri(C, Ga), triu2)
    xm, attn, vb, kcp, qd, kd, egl = outs
    tinv = tri_inv_pairs(xm)
    ncb = seq // C // G
    wide = pl.BlockSpec((C * G, VD), lambda b, n: (b * ncb + n, 0))
    mat = pl.BlockSpec((G, nq, C, LANES), lambda b, n: (b * ncb + n, 0, 0, 0))
    return pl.pallas_call(
        functools.partial(_gdn_c_kernel, G=G),
        grid=(batch, ncb),
        in_specs=[wide, wide, wide, wide, mat, mat,
                  pl.BlockSpec((G, 1, VD), lambda b, n: (b * ncb + n, 0, 0))],
        out_specs=wide,
        out_shape=jax.ShapeDtypeStruct((T, VD), F32),
        scratch_shapes=[pltpu.VMEM((nh, GDN_DK, GDN_DV), F32)],
        compiler_params=_cparams(("arbitrary", "arbitrary")),
        name="gdn_c",
    )(vb, kcp, qd, kd, tinv, attn, egl)


def _rope_pair(x, ct, st, low):
    swapped = jnp.where(low, pltpu.roll(x, LANES - RET_DK // 2, 1), pltpu.roll(x, RET_DK // 2, 1))
    return x * ct + swapped * st


def _ret_kernel(q_ref, k_ref, v_ref, ct_ref, st_ref, o_ref, s_ref, *, G):
    C = CHUNK

    @pl.when(pl.program_id(1) == 0)
    def _():
        s_ref[...] = jnp.zeros_like(s_ref)

    row, col = _chunk_masks()
    incl = col <= row
    dif = jnp.maximum(row - col, 0).astype(F32)
    lane = lax.broadcasted_iota(jnp.int32, (C, LANES), 1)
    head0 = lane < RET_DK
    low = lane % RET_DK < RET_DK // 2
    ridx = lax.broadcasted_iota(jnp.int32, (C, LANES), 0).astype(F32)

    def body(g, carry):
        sl = pl.ds(pl.multiple_of(g * C, C), C)
        ct, st = ct_ref[sl, :], st_ref[sl, :]
        lgs = [math.log(1.0 - 2.0 ** (-5.0 - h)) for h in range(RET_HEADS)]
        vss = [slice(h * RET_DV, (h + 1) * RET_DV) for h in range(RET_HEADS)]
        sc, cross, upd, vb = [], [], [], []
        for p in range(RET_HEADS // 2):
            ps = slice(p * LANES, (p + 1) * LANES)
            qp = _rope_pair(q_ref[sl, ps], ct, st, low) * RET_DK ** -0.5
            kp = _rope_pair(k_ref[sl, ps], ct, st, low)
            kb = kp.astype(BF16)
            for hh in range(2):
                h = 2 * p + hh
                msk = head0 if hh == 0 else jnp.logical_not(head0)
                q = jnp.where(msk, qp, 0.0)
                vb.append(v_ref[sl, vss[h]].astype(BF16))
                sc.append(_dot_nt(q.astype(BF16), kb))
                qdec = (q * jnp.exp((ridx + 1.0) * lgs[h])).astype(BF16)
                cross.append(_dot(qdec, s_ref[h].astype(BF16)))
                kdec = (jnp.where(msk, kp, 0.0) * jnp.exp((C - 1.0 - ridx) * lgs[h])).astype(BF16)
                upd.append(_dot_tn(kdec, vb[h]))
        for h in range(RET_HEADS):
            dmat = jnp.where(incl, jnp.exp(dif * lgs[h]), 0.0)
            o_ref[sl, vss[h]] = _dot((sc[h] * dmat).astype(BF16), vb[h]) + cross[h]
            s_ref[h] = s_ref[h] * math.exp(C * lgs[h]) + upd[h]
        return carry

    lax.fori_loop(0, G, body, 0)


def retention_chunked(p, q_blk, k_blk, v_blk, ct, st, batch, G=4):
    T = p.shape[0]
    QD, VD = RET_HEADS * RET_DK, RET_HEADS * RET_DV
    C = CHUNK
    seq = T // batch
    G = _pick(seq // C, (G, 2, 1))
    ncb = seq // C // G
    tab = pl.BlockSpec((C * G, LANES), lambda b, n: (b * ncb + n, 0))
    wide = pl.BlockSpec((C * G, VD), lambda b, n: (b * ncb + n, 0))
    return pl.pallas_call(
        functools.partial(_ret_kernel, G=G),
        grid=(batch, ncb),
        in_specs=[pl.BlockSpec((C * G, QD), lambda b, n: (b * ncb + n, q_blk)),
                  pl.BlockSpec((C * G, QD), lambda b, n: (b * ncb + n, k_blk)),
                  pl.BlockSpec((C * G, VD), lambda b, n: (b * ncb + n, v_blk)),
                  tab, tab],
        out_specs=wide,
        out_shape=jax.ShapeDtypeStruct((T, VD), F32),
        scratch_shapes=[pltpu.VMEM((RET_HEADS, LANES, RET_DV), F32)],
        compiler_params=_cparams(("arbitrary", "arbitrary")),
        name="retention",
    )(p, p, p, ct, st)


def _rope_tables(positions, d):
    inv = ROPE_THETA ** (-jnp.arange(0, d, 2, dtype=F32) / d)
    ang = positions.astype(F32).reshape(-1, 1) * inv
    cos, sin = jnp.cos(ang), jnp.sin(ang)
    reps = LANES // d
    return jnp.concatenate([cos, cos] * reps, -1), jnp.concatenate([-sin, sin] * reps, -1)


L0_MLA_W = 768
L0_LORA_W = 384
L0_N = 3 * RWKV_DIM + L0_MLA_W + L0_LORA_W
CQ_PAD = 512


def _proj_shift_kernel(x_ref, xh_ref, w_ref, mu_ref, o_ref, *, tiles_per_seq):
    m = pl.program_id(0)
    w = w_ref[...]
    p = _dot(x_ref[...].astype(BF16), w)
    ph = _dot(xh_ref[...].astype(BF16), w)
    ph = jnp.where(m % tiles_per_seq == 0, 0.0, ph)
    rows = lax.broadcasted_iota(jnp.int32, p.shape, 0)
    prev = jnp.where(rows == 0, ph[SUBLANES - 1:SUBLANES], pltpu.roll(p, 1, 0))
    o_ref[...] = p + mu_ref[...] * (prev - p)


def proj_shift(x, w, mu, seq, tm=1024, tn=1408):
    M, D = x.shape
    N = w.shape[1]
    tm = _pick(seq, (tm, 256, 128, 64, 32, 16, 8))
    tn = _pick(N, (tn, 512, 384, 256, 128))
    hb = tm // SUBLANES
    return pl.pallas_call(
        functools.partial(_proj_shift_kernel, tiles_per_seq=seq // tm),
        grid=(M // tm, N // tn),
        in_specs=[pl.BlockSpec((tm, D), lambda i, j: (i, 0)),
                  pl.BlockSpec((SUBLANES, D), lambda i, j: (jnp.maximum(i * hb - 1, 0), 0)),
                  pl.BlockSpec((D, tn), lambda i, j: (0, j)),
                  pl.BlockSpec((1, tn), lambda i, j: (0, j))],
        out_specs=pl.BlockSpec((tm, tn), lambda i, j: (i, j)),
        out_shape=jax.ShapeDtypeStruct((M, N), F32),
        compiler_params=_cparams(("parallel", "parallel")),
        name="proj_shift",
    )(x, x, w, mu.reshape(1, N))


def _mla_pre_kernel(p_ref, ct_ref, st_ref, qn_ref, kn_ref, wq_ref, wkv_ref, q_ref, k_ref, v_ref):
    p = p_ref[...]
    cq = p[:, :CQ_PAD]
    ckv = p[:, CQ_PAD:CQ_PAD + MLA_KV_RANK]
    kp = p[:, CQ_PAD + MLA_KV_RANK:]
    cqn = cq * lax.rsqrt(jnp.sum(cq * cq, -1, keepdims=True) * (1.0 / MLA_Q_RANK) + RMS_EPS) * qn_ref[...]
    ckvn = ckv * lax.rsqrt(jnp.mean(ckv * ckv, -1, keepdims=True) + RMS_EPS) * kn_ref[...]
    qa = _dot(cqn.astype(BF16), wq_ref[...])
    kva = _dot(ckvn.astype(BF16), wkv_ref[...])
    ct, st = ct_ref[...], st_ref[...]
    kr = (kp * ct + pltpu.roll(kp, MLA_ROPE, 1) * st)[:, :MLA_ROPE].astype(BF16)
    ones = jnp.ones((p.shape[0], MLA_V), BF16)
    hw = 2 * LANES
    for h in range(MLA_HEADS):
        qh = qa[:, h * hw:(h + 1) * hw]
        pe = qh[:, MLA_NOPE:]
        q_ref[0, h, :, :MLA_NOPE] = qh[:, :MLA_NOPE].astype(BF16)
        q_ref[0, h, :, MLA_NOPE:] = (pe * ct + pltpu.roll(pe, MLA_ROPE, 1) * st)[:, :MLA_ROPE].astype(BF16)
        kvh = kva[:, h * hw:(h + 1) * hw]
        k_ref[0, h, :, :MLA_NOPE] = kvh[:, :MLA_NOPE].astype(BF16)
        k_ref[0, h, :, MLA_NOPE:] = kr
        v_ref[0, h, :, :MLA_V] = kvh[:, MLA_NOPE:].astype(BF16)
        v_ref[0, h, :, MLA_V:] = ones


def mla_pre(p, mla_blk, ct, st, qn, kn, wq, wkv, batch, tm=512):
    T = p.shape[0]
    seq = T // batch
    tm = _pick(seq, (tm, 256, 128))
    nt = seq // tm
    dqk = MLA_NOPE + MLA_ROPE
    head_spec = lambda d: pl.BlockSpec((1, MLA_HEADS, tm, d), lambda i: (i // nt, 0, i % nt, 0))
    full = lambda a: pl.BlockSpec(a.shape, lambda i: (0,) * a.ndim)
    return pl.pallas_call(
        _mla_pre_kernel,
        grid=(T // tm,),
        in_specs=[pl.BlockSpec((tm, L0_MLA_W), lambda i: (i, mla_blk)),
                  pl.BlockSpec((tm, LANES), lambda i: (i, 0)),
                  pl.BlockSpec((tm, LANES), lambda i: (i, 0)),
                  full(qn), full(kn), full(wq), full(wkv)],
        out_specs=[head_spec(dqk), head_spec(dqk), head_spec(2 * MLA_V)],
        out_shape=[jax.ShapeDtypeStruct((batch, MLA_HEADS, seq, dqk), BF16),
                   jax.ShapeDtypeStruct((batch, MLA_HEADS, seq, dqk), BF16),
                   jax.ShapeDtypeStruct((batch, MLA_HEADS, seq, 2 * MLA_V), BF16)],
        compiler_params=_cparams(("parallel",)),
        name="mla_pre",
    )(p, ct, st, qn, kn, wq, wkv)


def _softplus(x):
    return jnp.maximum(x, 0.0) + jnp.log(1.0 + jnp.exp(-jnp.abs(x)))


def _group_sum(x, ones_bd, terms):
    parts = _split3(x)[:terms]
    cols = []
    for c in range(x.shape[1] // LANES):
        cs = slice(c * LANES, (c + 1) * LANES)
        out = _dot(parts[0][:, cs], ones_bd)
        for t in parts[1:]:
            out = out + _dot(t[:, cs], ones_bd)
        cols.append(out)
    return jnp.concatenate(cols, axis=1)


def _rwkv_pre_kernel(k_ref, lora_ref, wl_ref, g2_ref, w0_ref, a0_ref, kk_ref, ka_ref, ones_ref,
                     lw_ref, k2_ref, av_ref, bv_ref, g_ref):
    lora = lora_ref[...]
    lane = lax.broadcasted_iota(jnp.int32, (lora.shape[0], LANES), 1)
    wa_in = jnp.where(lane < DECAY_LORA, jnp.tanh(lora[:, :LANES]), lora[:, :LANES])
    wa = _dot(wa_in.astype(BF16), wl_ref[...])
    g_ref[...] = _dot(jax.nn.sigmoid(lora[:, LANES:]).astype(BF16), g2_ref[...])
    w_log = -_softplus(-(w0_ref[...] + wa[:, :RWKV_DIM])) - 0.5
    lw_ref[...] = -jnp.exp(w_log)
    a = jax.nn.sigmoid(a0_ref[...] + wa[:, RWKV_DIM:])
    k = k_ref[...]
    kk = k * kk_ref[...]
    kkn = kk * lax.rsqrt(_group_sum(kk * kk, ones_ref[...], 2) + 1e-6)
    k2_ref[...] = k * (1.0 + (a - 1.0) * ka_ref[...])
    av_ref[...] = -kkn
    bv_ref[...] = kkn * a


def rwkv_pre(p, k_blk, lora_blk, wl, g2, w0, a0, kk, ka, ones_bd, tm=512):
    T = p.shape[0]
    tm = _pick(T, (tm, 256, 128, 64))
    row = pl.BlockSpec((tm, RWKV_DIM), lambda i: (i, 0))
    full = lambda a: pl.BlockSpec(a.shape, lambda i: (0,) * a.ndim)
    vec = lambda a: a.reshape(1, RWKV_DIM)
    args = (wl, g2, vec(w0), vec(a0), vec(kk), vec(ka), ones_bd)
    return pl.pallas_call(
        _rwkv_pre_kernel,
        grid=(T // tm,),
        in_specs=[pl.BlockSpec((tm, RWKV_DIM), lambda i: (i, k_blk)),
                  pl.BlockSpec((tm, L0_LORA_W), lambda i: (i, lora_blk))] + [full(a) for a in args],
        out_specs=[row] * 5,
        out_shape=[jax.ShapeDtypeStruct((T, RWKV_DIM), F32)] * 5,
        compiler_params=_cparams(("parallel",)),
        name="rwkv_pre",
    )(p, p, *args)


def _rwkv_post_kernel(y_ref, r_ref, k2_ref, v_ref, g_ref, rk_ref, gw_ref, gb_ref, ones_ref, o_ref):
    ones_bd = ones_ref[...]
    y = y_ref[...]
    yc = y - _group_sum(y, ones_bd, 2) * (1.0 / RWKV_HEAD)
    var = _group_sum(yc * yc, ones_bd, 1) * (1.0 / RWKV_HEAD)
    yn = yc * lax.rsqrt(var + RWKV_GN_EPS) * gw_ref[...] + gb_ref[...]
    bonus = _group_sum(r_ref[...] * k2_ref[...] * rk_ref[...], ones_bd, 2) * v_ref[...]
    o_ref[...] = ((yn + bonus) * g_ref[...]).astype(o_ref.dtype)


def rwkv_post(y, p, r_blk, v_blk, k2, g, rk, gw, gb, ones_bd, tm=512):
    T = y.shape[0]
    tm = _pick(T, (tm, 256, 128, 64))
    row = pl.BlockSpec((tm, RWKV_DIM), lambda i: (i, 0))
    full = lambda a: pl.BlockSpec(a.shape, lambda i: (0,) * a.ndim)
    vec = lambda a: a.reshape(1, RWKV_DIM)
    args = (vec(rk), vec(gw), vec(gb), ones_bd)
    return pl.pallas_call(
        _rwkv_post_kernel,
        grid=(T // tm,),
        in_specs=[row, pl.BlockSpec((tm, RWKV_DIM), lambda i: (i, r_blk)), row,
                  pl.BlockSpec((tm, RWKV_DIM), lambda i: (i, v_blk)), row] + [full(a) for a in args],
        out_specs=row,
        out_shape=jax.ShapeDtypeStruct((T, RWKV_DIM), BF16),
        compiler_params=_cparams(("parallel",)),
        name="rwkv_post",
    )(y, p, k2, p, g, *args)


def _mm2_ln_kernel(a1_ref, a2_ref, w1_ref, w2_ref, x_ref, g_ref, b_ref, o_ref):
    y = ALPHA * x_ref[...] + _dot(a1_ref[...], w1_ref[...]) + _dot(a2_ref[...], w2_ref[...])
    mu = jnp.mean(y, axis=-1, keepdims=True)
    yc = y - mu
    var = jnp.mean(yc * yc, axis=-1, keepdims=True)
    o_ref[...] = yc * lax.rsqrt(var + LN_EPS) * g_ref[...] + b_ref[...]


def mm2_ln(a1, a2, w, x, g, b, tm=512):
    M, K1 = a1.shape
    K2 = a2.shape[1]
    N = w.shape[1]
    tm = _pick(M, (tm, 256, 128, 64, 32, 16, 8))
    w = w.astype(BF16)
    row = lambda k: pl.BlockSpec((tm, k), lambda i: (i, 0))
    return pl.pallas_call(
        _mm2_ln_kernel,
        grid=(M // tm,),
        in_specs=[row(K1), row(K2),
                  pl.BlockSpec((K1, N), lambda i: (0, 0)),
                  pl.BlockSpec((K2, N), lambda i: (0, 0)),
                  row(N),
                  pl.BlockSpec((1, N), lambda i: (0, 0)),
                  pl.BlockSpec((1, N), lambda i: (0, 0))],
        out_specs=row(N),
        out_shape=jax.ShapeDtypeStruct((M, N), F32),
        compiler_params=_cparams(("parallel",)),
        name="mm2_ln",
    )(a1, a2, w[:K1], w[K1:], x, g.reshape(1, N), b.reshape(1, N))


def _half_swap(n):
    return jnp.concatenate([jnp.arange(n // 2, n), jnp.arange(0, n // 2)])


def _mla_rwkv_mixer(xb, B, S, ct, st, w_in, q_norm, w_uq, kv_norm, w_ukv, rwkv_mu, rwkv_w0, rwkv_w2,
                    rwkv_a0, rwkv_a2, rwkv_g2, rwkv_k_k, rwkv_k_a, rwkv_r_k, rwkv_gn_w, rwkv_gn_b,
                    w_out, ln_g, ln_b):
    T = B * S
    D = xb.shape[1]
    o_kv = MLA_Q_RANK
    o_pe = o_kv + MLA_KV_RANK
    o_r = o_pe + MLA_ROPE
    o_k, o_v = o_r + RWKV_DIM, o_r + 2 * RWKV_DIM
    o_xw = o_r + 3 * RWKV_DIM
    o_xg = o_xw + DECAY_LORA + AAA_LORA
    zc = lambda n: jnp.zeros((D, n), F32)
    w_pe = w_in[:, o_pe:o_r]
    w0p = jnp.concatenate([w_in[:, o_r:o_xw], w_in[:, :o_kv], zc(CQ_PAD - MLA_Q_RANK), w_in[:, o_kv:o_pe],
                           w_pe, w_pe[:, _half_swap(MLA_ROPE)], w_in[:, o_xw:], zc(L0_N - L0_MLA_W - 3 * RWKV_DIM
                                                                                   - (w_in.shape[1] - o_xw))],
                          axis=1).astype(BF16)
    mu = rwkv_mu
    mu_p = jnp.concatenate([mu[:3 * RWKV_DIM], jnp.zeros((L0_MLA_W,), F32), mu[3 * RWKV_DIM:],
                            jnp.zeros((L0_N - L0_MLA_W - mu.shape[0],), F32)])
    p = proj_shift(xb, w0p, mu_p, S)
    dqk = MLA_NOPE + MLA_ROPE
    qs = dqk ** -0.5 * math.log2(math.e)
    wq = w_uq.reshape(MLA_Q_RANK, MLA_HEADS, dqk) * qs
    wq = jnp.concatenate([wq, wq[..., MLA_NOPE:][..., _half_swap(MLA_ROPE)]], -1).reshape(MLA_Q_RANK, -1)
    wq = jnp.pad(wq, ((0, CQ_PAD - MLA_Q_RANK), (0, 0))).astype(BF16)
    qn = jnp.pad(q_norm, (0, CQ_PAD - MLA_Q_RANK)).reshape(1, CQ_PAD)
    q, k, v = mla_pre(p, 3 * RWKV_DIM // L0_MLA_W, ct, st, qn, kv_norm.reshape(1, -1), wq,
                      w_ukv.astype(BF16), B)
    o_mla = flash_attention(q, k, v)
    zl = jnp.zeros((DECAY_LORA, RWKV_DIM), F32)
    wl = jnp.concatenate([jnp.concatenate([rwkv_w2, zl], 1), jnp.concatenate([zl, rwkv_a2], 1)], 0).astype(BF16)
    g2 = jnp.pad(rwkv_g2, ((0, L0_LORA_W - LANES - GATE_LORA), (0, 0))).astype(BF16)
    hid = jnp.arange(LANES) // RWKV_HEAD
    ones_bd = (hid[:, None] == hid[None, :]).astype(BF16)
    lw, k2, av, bv, g = rwkv_pre(p, 1, (3 * RWKV_DIM + L0_MLA_W) // L0_LORA_W, wl, g2, rwkv_w0, rwkv_a0,
                                 rwkv_k_k, rwkv_k_a, ones_bd)
    y = rwkv7_chunked(p, 0, 2 * RWKV_DIM // LANES, lw, k2, av, bv, B)
    o_rwkv = rwkv_post(y, p, 0, 2, k2, g, rwkv_r_k.reshape(-1), rwkv_gn_w, rwkv_gn_b, ones_bd)
    return mm2_ln(o_mla, o_rwkv, w_out, xb, ln_g, ln_b)


GDN_QKV_W = 2 * GDN_QK_HEADS * GDN_DK + GDN_V_HEADS * GDN_DV


def _gdn_pre_kernel(x_ref, xh_ref, ba_ref, cw_ref, al_ref, dt_ref, q_ref, k_ref, v_ref, bg_ref, *,
                    tiles_per_seq):
    m = pl.program_id(0)
    x = x_ref[...]
    xh = jnp.where(m % tiles_per_seq == 0, 0.0, xh_ref[...])
    rows = lax.broadcasted_iota(jnp.int32, x.shape, 0)
    cw = cw_ref[...]
    acc = x * cw[GDN_CONV - 1:GDN_CONV]
    for j in range(1, GDN_CONV):
        sh = pltpu.roll(x, j, 0)
        for i in range(j):
            sh = jnp.where(rows == i, xh[SUBLANES - j + i:SUBLANES - j + i + 1], sh)
        acc = acc + sh * cw[GDN_CONV - 1 - j:GDN_CONV - j]
    y = acc * jax.nn.sigmoid(acc)
    qk_w = GDN_QK_HEADS * GDN_DK
    for h in range(GDN_QK_HEADS):
        cs = slice(h * GDN_DK, (h + 1) * GDN_DK)
        yq = y[:, cs]
        q_ref[:, cs] = yq * (lax.rsqrt(jnp.sum(yq * yq, -1, keepdims=True) + 1e-6) * GDN_DK ** -0.5)
        yk = y[:, qk_w + h * GDN_DK:qk_w + (h + 1) * GDN_DK]
        k_ref[:, cs] = yk * lax.rsqrt(jnp.sum(yk * yk, -1, keepdims=True) + 1e-6)
    v_ref[...] = y[:, 2 * qk_w:]
    ba = ba_ref[...]
    lane = lax.broadcasted_iota(jnp.int32, ba.shape, 1)
    g = -jnp.exp(al_ref[...]) * _softplus(ba + dt_ref[...])
    bg_ref[...] = jnp.where(lane < GDN_V_HEADS, jax.nn.sigmoid(ba), g)


def gdn_pre(p, ba_blk, conv_w, a_log, dt_bias, seq, tm=256):
    T = p.shape[0]
    tm = _pick(seq, (tm, 128, 64))
    hb = tm // SUBLANES
    qk_w = GDN_QK_HEADS * GDN_DK
    v_w = GDN_V_HEADS * GDN_DV
    lane_pad = lambda a: jnp.pad(a, (GDN_V_HEADS, LANES - 2 * GDN_V_HEADS)).reshape(1, LANES)
    row = lambda n: pl.BlockSpec((tm, n), lambda i: (i, 0))
    full = lambda shp: pl.BlockSpec(shp, lambda i: (0, 0))
    return pl.pallas_call(
        functools.partial(_gdn_pre_kernel, tiles_per_seq=seq // tm),
        grid=(T // tm,),
        in_specs=[row(GDN_QKV_W),
                  pl.BlockSpec((SUBLANES, GDN_QKV_W), lambda i: (jnp.maximum(i * hb - 1, 0), 0)),
                  pl.BlockSpec((tm, LANES), lambda i: (i, ba_blk)),
                  full((GDN_CONV, GDN_QKV_W)), full((1, LANES)), full((1, LANES))],
        out_specs=[row(qk_w), row(qk_w), row(v_w), row(LANES)],
        out_shape=[jax.ShapeDtypeStruct((T, qk_w), F32), jax.ShapeDtypeStruct((T, qk_w), F32),
                   jax.ShapeDtypeStruct((T, v_w), F32), jax.ShapeDtypeStruct((T, LANES), F32)],
        compiler_params=_cparams(("parallel",)),
        name="gdn_pre",
    )(p, p, p, conv_w, lane_pad(a_log), lane_pad(dt_bias))


def _l1_post_kernel(og_ref, z_ref, or_ref, rg_ref, gn_ref, rw_ref, rb_ref, o_ref):
    v_w = GDN_V_HEADS * GDN_DV
    for h in range(GDN_V_HEADS):
        cs = slice(h * LANES, (h + 1) * LANES)
        o = og_ref[:, cs]
        z = z_ref[:, cs]
        on = o * lax.rsqrt(jnp.mean(o * o, -1, keepdims=True) + RMS_EPS) * gn_ref[...]
        o_ref[:, cs] = (on * (z * jax.nn.sigmoid(z))).astype(o_ref.dtype)
        r = or_ref[:, cs]
        rc = r - jnp.mean(r, -1, keepdims=True)
        rn = rc * lax.rsqrt(jnp.mean(rc * rc, -1, keepdims=True) + LN_EPS) * rw_ref[:, cs] + rb_ref[:, cs]
        g = rg_ref[:, cs]
        o_ref[:, v_w + h * LANES:v_w + (h + 1) * LANES] = (rn * (g * jax.nn.sigmoid(g))).astype(o_ref.dtype)


def l1_post(o_gdn, o_ret, p, z_blk, rg_blk, gdn_norm, ret_gn_w, ret_gn_b, tm=512):
    T, W = o_gdn.shape
    tm = _pick(T, (tm, 256, 128, 64))
    row = pl.BlockSpec((tm, W), lambda i: (i, 0))
    return pl.pallas_call(
        _l1_post_kernel,
        grid=(T // tm,),
        in_specs=[row, pl.BlockSpec((tm, W), lambda i: (i, z_blk)), row,
                  pl.BlockSpec((tm, W), lambda i: (i, rg_blk)),
                  pl.BlockSpec((1, LANES), lambda i: (0, 0)),
                  pl.BlockSpec((1, W), lambda i: (0, 0)),
                  pl.BlockSpec((1, W), lambda i: (0, 0))],
        out_specs=pl.BlockSpec((tm, 2 * W), lambda i: (i, 0)),
        out_shape=jax.ShapeDtypeStruct((T, 2 * W), BF16),
        compiler_params=_cparams(("parallel",)),
        name="l1_post",
    )(o_gdn, p, o_ret, p, gdn_norm.reshape(1, LANES), ret_gn_w.reshape(1, W), ret_gn_b.reshape(1, W))


def _gdn_retention_mixer(xb, B, S, ct, st, w_in, gdn_conv_w, gdn_A_log, gdn_dt_bias, gdn_norm,
                         ret_gn_w, ret_gn_b, w_out, ln_g, ln_b):
    D = xb.shape[1]
    v_w = GDN_V_HEADS * GDN_DV
    n_gdn = GDN_QKV_W + v_w
    w1p = jnp.concatenate([w_in[:, :n_gdn], w_in[:, n_gdn + 2 * GDN_V_HEADS:],
                           w_in[:, n_gdn:n_gdn + 2 * GDN_V_HEADS],
                           jnp.zeros((D, LANES - 2 * GDN_V_HEADS), F32)], axis=1)
    p = mm(xb, w1p, tm=1024, tn=896)
    q, k, v, bg = gdn_pre(p, (w1p.shape[1] - LANES) // LANES, gdn_conv_w, gdn_A_log, gdn_dt_bias, S)
    o_gdn = gdn_chunked(q, k, v, bg, B)
    qd = RET_HEADS * RET_DK
    o_ret = retention_chunked(p, n_gdn // qd, n_gdn // qd + 1, (n_gdn + 2 * qd) // v_w, ct, st, B)
    m1 = l1_post(o_gdn, o_ret, p, GDN_QKV_W // v_w, (n_gdn + 2 * qd + v_w) // v_w, gdn_norm, ret_gn_w, ret_gn_b)
    return mm_ln_full(m1, w_out, xb, ln_g, ln_b, tm=512)


def _conv_ffn_ln(x, S, w_gate, w_val, conv_w, conv_b, w_down, ln_g, ln_b):
    h = ffn1(x, w_gate, w_val, conv_w, conv_b, S)
    return mm_ln_full(h, w_down, x, ln_g, ln_b)


def kernel(x, positions, l0_w_in, l0_q_norm, l0_w_uq, l0_kv_norm, l0_w_ukv, l0_rwkv_mu, l0_rwkv_w0, l0_rwkv_w2, l0_rwkv_a0, l0_rwkv_a2, l0_rwkv_g2, l0_rwkv_k_k, l0_rwkv_k_a, l0_rwkv_r_k, l0_rwkv_gn_w, l0_rwkv_gn_b, l0_w_out, l0_ln1_g, l0_ln1_b, l0_ffn_w_gate, l0_ffn_w_val, l0_ffn_conv_w, l0_ffn_conv_b, l0_ffn_w_down, l0_ln2_g, l0_ln2_b, l1_w_in, l1_gdn_conv_w, l1_gdn_A_log, l1_gdn_dt_bias, l1_gdn_norm, l1_ret_gn_w, l1_ret_gn_b, l1_w_out, l1_ln1_g, l1_ln1_b, l1_ffn_w_gate, l1_ffn_w_val, l1_ffn_conv_w, l1_ffn_conv_b, l1_ffn_w_down, l1_ln2_g, l1_ln2_b):
    B, S, D = x.shape
    T = B * S
    xb = x.reshape(T, D)
    assert MLA_ROPE == RET_DK
    ct, st = _rope_tables(positions, MLA_ROPE)
    xb = _mla_rwkv_mixer(xb, B, S, ct, st, l0_w_in, l0_q_norm, l0_w_uq, l0_kv_norm, l0_w_ukv, l0_rwkv_mu,
                         l0_rwkv_w0, l0_rwkv_w2, l0_rwkv_a0, l0_rwkv_a2, l0_rwkv_g2, l0_rwkv_k_k, l0_rwkv_k_a,
                         l0_rwkv_r_k, l0_rwkv_gn_w, l0_rwkv_gn_b, l0_w_out, l0_ln1_g, l0_ln1_b)
    xb = _conv_ffn_ln(xb, S, l0_ffn_w_gate, l0_ffn_w_val, l0_ffn_conv_w, l0_ffn_conv_b, l0_ffn_w_down,
                      l0_ln2_g, l0_ln2_b)
    xb = _gdn_retention_mixer(xb, B, S, ct, st, l1_w_in, l1_gdn_conv_w, l1_gdn_A_log, l1_gdn_dt_bias,
                              l1_gdn_norm, l1_ret_gn_w, l1_ret_gn_b, l1_w_out, l1_ln1_g, l1_ln1_b)
    xb = _conv_ffn_ln(xb, S, l1_ffn_w_gate, l1_ffn_w_val, l1_ffn_conv_w, l1_ffn_conv_b, l1_ffn_w_down,
                      l1_ln2_g, l1_ln2_b)
    return xb.reshape(B, S, D)
```
